```python
import math
import jax
import jax.numpy as jnp
from jax import lax
import numpy as np

D_MODEL = 2048
BATCH = 2
SEQ = 4096
DEPTH = 2
DEC_BATCH = 8
DEC_SEQ = 8
PAST_LEN = 16384
PAGE_SIZE = 128

N_BRANCH = 4
MIX_WIDTH = D_MODEL // 4
HEAD_DIM = 64
S5_GROUP = 16
S5_GROUPS = MIX_WIDTH // S5_GROUP
S5_STATE = 64
DSA_HEADS = MIX_WIDTH // HEAD_DIM
IDX_HEADS = 4
IDX_DIM = 64
DSA_TOPK = 256
NSA_HEADS = MIX_WIDTH // HEAD_DIM
NSA_KV_HEADS = 2
NSA_REP = NSA_HEADS // NSA_KV_HEADS
NSA_KV_WIDTH = NSA_KV_HEADS * HEAD_DIM
CMP_STRIDE = 16
CMP_LEN = 2 * CMP_STRIDE
SEL_BLOCK = 64
SEL_TOPN = 16
CMP_PER_SEL = SEL_BLOCK // CMP_STRIDE
WINDOW = 512
MLSTM_HEADS = 4
MLSTM_DK = MIX_WIDTH // MLSTM_HEADS
MLSTM_DV = MIX_WIDTH // MLSTM_HEADS
MLSTM_CHUNK = 64
D_FF = 2 * D_MODEL
ROPE_THETA = 10000.0
QUERY_BLOCK = 128
RMS_EPS = 1e-6
NEG_INF = -1e30
FORCE_SCORE = 1e4
PAD_POS = -(2 ** 30)
IN_SIZES = (
    MIX_WIDTH,
    3 * MIX_WIDTH,
    IDX_HEADS * IDX_DIM,
    IDX_DIM,
    IDX_HEADS,
    MIX_WIDTH,
    6 * NSA_KV_WIDTH,
    3 * NSA_HEADS,
    3 * MIX_WIDTH,
    2 * MLSTM_HEADS,
    MIX_WIDTH,
    N_BRANCH * D_MODEL,
)

kernel_name = 'gated_hybrid_s5_dsa_nsa_mlstm_step'


def rms_norm(x, g):
    xf = x.astype(jnp.float32)
    y = xf * lax.rsqrt(jnp.mean(xf * xf, axis=-1, keepdims=True) + RMS_EPS)
    return (y * g.astype(jnp.float32)).astype(x.dtype)


def rope(x, pos):
    half = x.shape[-1] // 2
    inv = ROPE_THETA ** (-jnp.arange(half, dtype=jnp.float32) / half)
    ang = pos.astype(jnp.float32)[:, None] * inv[None, :]
    shape = (pos.shape[0],) + (1,) * (x.ndim - 3) + (half,)
    cos, sin = jnp.cos(ang).reshape(shape), jnp.sin(ang).reshape(shape)
    xf = x.astype(jnp.float32)
    x1, x2 = xf[..., :half], xf[..., half:]
    return jnp.concatenate([x1 * cos - x2 * sin, x1 * sin + x2 * cos], axis=-1).astype(x.dtype)


def split_cols(z):
    return jnp.split(z, np.cumsum(IN_SIZES)[:-1].tolist(), axis=-1)


def swiglu(h, w_in, w_out):
    a, g = jnp.split(h @ w_in, 2, axis=-1)
    return (jax.nn.silu(a) * g) @ w_out


def take_rows(a, idx):
    return jax.vmap(lambda ab, ib: ab[ib])(a, idx)


def gather_pages(cache, layer, page_table):
    g = cache[layer, page_table]
    return g.reshape(g.shape[0], g.shape[1] * g.shape[2], *g.shape[3:])


def paged_rows(cache, layer, page_table, new, idx):
    ps = cache.shape[2]
    past = page_table.shape[1] * ps
    pidx = jnp.minimum(idx, past - 1)
    page = jnp.take_along_axis(page_table, (pidx // ps).reshape(idx.shape[0], -1), axis=1).reshape(idx.shape)
    from_pool = cache[layer, page, pidx % ps]
    from_new = take_rows(new, jnp.clip(idx - past, 0, new.shape[1] - 1))
    keep = (idx < past).reshape(idx.shape + (1,) * (new.ndim - 2))
    return jnp.where(keep, from_pool, from_new)


def over_query_blocks(fn, *arrays):
    t = arrays[0].shape[1]
    qb = QUERY_BLOCK if t % QUERY_BLOCK == 0 else t
    nb = t // qb
    blocks = tuple(jnp.moveaxis(a.reshape(a.shape[0], nb, qb, *a.shape[2:]), 1, 0) for a in arrays)
    out = lax.map(lambda blk: fn(*blk), blocks)
    out = jnp.moveaxis(out, 0, 1)
    return out.reshape(out.shape[0], t, *out.shape[3:])


def stk(group, i):
    return jnp.stack([s[i] for s in group])


def s5_discretize(lam_re, lam_im, log_dt, b_re, b_im):
    lr, li = lam_re.astype(jnp.float32), lam_im.astype(jnp.float32)
    dt = jnp.exp(log_dt.astype(jnp.float32))[:, None]
    mag = jnp.exp(lr * dt)
    a_re, a_im = mag * jnp.cos(li * dt), mag * jnp.sin(li * dt)
    den = lr * lr + li * li
    nr = a_re - 1.0
    coef_re = (nr * lr + a_im * li) / den
    coef_im = (a_im * lr - nr * li) / den
    br, bi = b_re.astype(jnp.float32), b_im.astype(jnp.float32)
    bb_re = coef_re[..., None] * br - coef_im[..., None] * bi
    bb_im = coef_re[..., None] * bi + coef_im[..., None] * br
    return a_re, a_im, bb_re, bb_im


def s5_scan(u, h0_re, h0_im, a_re, a_im, bb_re, bb_im):
    x_re = jnp.einsum('btgc,gpc->btgp', u, bb_re)
    x_im = jnp.einsum('btgc,gpc->btgp', u, bb_im)
    h0r, h0i = h0_re.astype(jnp.float32), h0_im.astype(jnp.float32)
    x_re = x_re.at[:, 0].add(a_re * h0r - a_im * h0i)
    x_im = x_im.at[:, 0].add(a_re * h0i + a_im * h0r)

    def combine(e1, e2):
        a1r, a1i, b1r, b1i = e1
        a2r, a2i, b2r, b2i = e2
        return (a1r * a2r - a1i * a2i, a1r * a2i + a1i * a2r,
                a2r * b1r - a2i * b1i + b2r, a2r * b1i + a2i * b1r + b2i)

    elems = (jnp.broadcast_to(a_re, x_re.shape), jnp.broadcast_to(a_im, x_im.shape), x_re, x_im)
    _, _, h_re, h_im = lax.associative_scan(combine, elems, axis=1)
    return h_re, h_im


def dsa_block(q, qi, wi, qpos, ki_all, fetch, topk):
    f32 = jnp.float32
    tq = qpos[:, :, None]
    causal = jnp.arange(ki_all.shape[1])[None, None, :] <= tq
    rel = jax.nn.relu(jnp.einsum('bqhd,bsd->bqhs', qi.astype(f32), ki_all.astype(f32)))
    score = jnp.where(causal, jnp.einsum('bqh,bqhs->bqs', wi.astype(f32), rel), NEG_INF)
    _, sel = lax.top_k(score, topk)
    kv = fetch(sel).astype(f32)
    logits = jnp.einsum('bqhd,bqkhd->bqhk', q.astype(f32), kv[:, :, :, 0]) / math.sqrt(HEAD_DIM)
    logits = jnp.where((sel <= tq)[:, :, None, :], logits, NEG_INF)
    o = jnp.einsum('bqhk,bqkhd->bqhd', jax.nn.softmax(logits, axis=-1), kv[:, :, :, 1])
    return o.astype(q.dtype)


def block_means(k):
    b, s = k.shape[:2]
    n_sub = s // CMP_STRIDE
    sub = k[:, :n_sub * CMP_STRIDE].astype(jnp.float32).reshape(b, n_sub, CMP_STRIDE, *k.shape[2:]).mean(2)
    return 0.5 * (sub[:, :-1] + sub[:, 1:])


def sel_blocks(k):
    b, s = k.shape[:2]
    ns = -(-s // SEL_BLOCK)
    kp = jnp.pad(k, ((0, 0), (0, ns * SEL_BLOCK - s), (0, 0), (0, 0)))
    return jnp.moveaxis(kp.reshape(b, ns, SEL_BLOCK, NSA_KV_HEADS, HEAD_DIM), 3, 1)


def nsa_cmp_sel_block(q, gate, qpos, kcm, vcm, ksb, vsb):
    f32 = jnp.float32
    b, qb = q.shape[:2]
    scale = 1.0 / math.sqrt(HEAD_DIM)
    qg = q.astype(f32).reshape(b, qb, NSA_KV_HEADS, NSA_REP, HEAD_DIM)
    tq = qpos[:, :, None]
    nc = kcm.shape[1]
    cvalid = (jnp.arange(nc) * CMP_STRIDE + CMP_LEN - 1)[None, None, :] <= tq
    cmask = cvalid[:, :, None, None, :]
    lc = jnp.einsum('bqgrd,bcgd->bqgrc', qg, kcm) * scale
    pc = jax.nn.softmax(jnp.where(cmask, lc, NEG_INF), axis=-1) * cmask
    o_cmp = jnp.einsum('bqgrc,bcgd->bqgrd', pc, vcm)
    ns = ksb.shape[2]
    imp = jnp.pad(pc.sum(3), ((0, 0), (0, 0), (0, 0), (0, ns * CMP_PER_SEL - nc)))
    imp = imp.reshape(b, qb, NSA_KV_HEADS, ns, CMP_PER_SEL).sum(-1)
    blk = jnp.arange(ns)
    admissible = (blk * SEL_BLOCK)[None, None, :] <= tq
    forced = (blk[None, None, :] == tq // SEL_BLOCK) | (blk == 0)[None, None, :]
    imp = jnp.where(forced[:, :, None, :], FORCE_SCORE, imp)
    imp = jnp.where(admissible[:, :, None, :], imp, NEG_INF)
    n_sel = min(SEL_TOPN, ns)
    _, bsel = lax.top_k(imp, n_sel)
    bsel_g = jnp.moveaxis(bsel, 2, 1)
    gat = jax.vmap(jax.vmap(lambda a, i: a[i]))
    kg = gat(ksb, bsel_g).astype(f32)
    vg = gat(vsb, bsel_g).astype(f32)
    tok = bsel_g[..., None] * SEL_BLOCK + jnp.arange(SEL_BLOCK)
    tvalid = tok <= qpos[:, None, :, None, None]
    ls = jnp.einsum('bqgrd,bgqnld->bgqrnl', qg, kg) * scale
    ls = jnp.where(tvalid[:, :, :, None], ls, NEG_INF)
    ps = jax.nn.softmax(ls.reshape(b, NSA_KV_HEADS, qb, NSA_REP, n_sel * SEL_BLOCK), axis=-1)
    o_sel = jnp.einsum('bgqrm,bgqmd->bqgrd', ps, vg.reshape(b, NSA_KV_HEADS, qb, n_sel * SEL_BLOCK, HEAD_DIM))
    o_cmp = o_cmp.reshape(b, qb, NSA_HEADS, HEAD_DIM)
    o_sel = o_sel.reshape(b, qb, NSA_HEADS, HEAD_DIM)
    g = gate.astype(f32)
    return (g[..., 0:1] * o_cmp + g[..., 1:2] * o_sel).astype(q.dtype)


def window_block(q, qpos, kv, kpos):
    f32 = jnp.float32
    b, qb = q.shape[:2]
    qg = q.astype(f32).reshape(b, qb, NSA_KV_HEADS, NSA_REP, HEAD_DIM)
    lg = jnp.einsum('bqgrd,bkgd->bqgrk', qg, kv[:, :, 0].astype(f32)) / math.sqrt(HEAD_DIM)
    kp, tq = kpos[None, None, :], qpos[:, :, None]
    ok = (kp <= tq) & (kp > tq - WINDOW)
    lg = jnp.where(ok[:, :, None, None, :], lg, NEG_INF)
    o = jnp.einsum('bqgrk,bkgd->bqgrd', jax.nn.softmax(lg, axis=-1), kv[:, :, 1].astype(f32))
    return o.reshape(b, qb, NSA_HEADS, HEAD_DIM).astype(q.dtype)


def mlstm_chunk(carry, xs):
    c, n, m = carry
    q, k, v, ig, lf = xs
    L = q.shape[2]
    cum = jnp.cumsum(lf, axis=-1)
    causal = jnp.tril(jnp.ones((L, L), dtype=bool))
    logd = jnp.where(causal, cum[..., :, None] - cum[..., None, :] + ig[..., None, :], NEG_INF)
    log_state = cum + m[..., None]
    m_t = jnp.maximum(log_state, logd.max(-1))
    w_in = jnp.exp(logd - m_t[..., None])
    w_st = jnp.exp(log_state - m_t)
    s = jnp.einsum('bhtd,bhsd->bhts', q, k) * w_in
    num = w_st[..., None] * jnp.einsum('bhtd,bhde->bhte', q, c) + jnp.einsum('bhts,bhse->bhte', s, v)
    den = w_st * jnp.einsum('bhtd,bhd->bht', q, n) + s.sum(-1)
    h = num / jnp.maximum(jnp.abs(den), jnp.exp(-m_t))[..., None]
    total = cum[..., -1]
    log_end = cum[..., -1:] - cum + ig
    m_new = jnp.maximum(total + m, log_end.max(-1))
    a = jnp.exp(total + m - m_new)
    ws = jnp.exp(log_end - m_new[..., None])
    c_new = a[..., None, None] * c + jnp.einsum('bhs,bhsd,bhse->bhde', ws, k, v)
    n_new = a[..., None] * n + jnp.einsum('bhs,bhsd->bhd', ws, k)
    return (c_new, n_new, m_new), h


def mlstm_run(q, k, v, ig, lf, state):
    b, nh, t, _ = q.shape
    ch = MLSTM_CHUNK if t % MLSTM_CHUNK == 0 else t
    nc = t // ch

    def chunks(a):
        return jnp.moveaxis(a.reshape(b, nh, nc, ch, *a.shape[3:]), 2, 0)

    state = tuple(s.astype(jnp.float32) for s in state)
    state, hs = lax.scan(mlstm_chunk, state, tuple(chunks(a) for a in (q, k, v, ig, lf)))
    hs = jnp.moveaxis(hs, 0, 2).reshape(b, nh, t, hs.shape[-1])
    return state, hs


def token_mixer(h, pos, lw, past):
    f32 = jnp.float32
    b, t, _ = h.shape
    if past is not None:
        layer, pt = past['layer'], past['page_table']
    (u, qkv_b, qi, ki, wi, q_n, kv_n, g_n, qkv_m, gif_m, o_m, g_br) = split_cols(h @ lw['w_in'])

    a_re, a_im, bb_re, bb_im = s5_discretize(lw['s5_lam_re'], lw['s5_lam_im'], lw['s5_log_dt'],
                                             lw['s5_b_re'], lw['s5_b_im'])
    uu = u.astype(f32).reshape(b, t, S5_GROUPS, S5_GROUP)
    if past is None:
        h0_re = jnp.zeros((b, S5_GROUPS, S5_STATE), f32)
        h0_im = jnp.zeros((b, S5_GROUPS, S5_STATE), f32)
    else:
        h0_re, h0_im = past['s5']
    s_re, s_im = s5_scan(uu, h0_re, h0_im, a_re, a_im, bb_re, bb_im)
    y = (jnp.einsum('btgp,gcp->btgc', s_re, lw['s5_c_re'].astype(f32))
         - jnp.einsum('btgp,gcp->btgc', s_im, lw['s5_c_im'].astype(f32))
         + lw['s5_d'].astype(f32).reshape(S5_GROUPS, S5_GROUP) * uu)
    y = jax.nn.gelu(y.reshape(b, t, MIX_WIDTH))
    o_s5 = y * jax.nn.sigmoid(y @ lw['w_s5_glu'].astype(f32))

    qkv_b = qkv_b.reshape(b, t, 3, DSA_HEADS, HEAD_DIM)
    q_b = rope(qkv_b[:, :, 0], pos)
    kv_b = jnp.stack([rope(qkv_b[:, :, 1], pos), qkv_b[:, :, 2]], axis=2)
    qi = rope(qi.reshape(b, t, IDX_HEADS, IDX_DIM), pos)
    ki = rope(ki, pos)
    if past is None:
        ki_all = ki
        fetch = lambda sel: take_rows(kv_b, sel)
    else:
        ki_all = jnp.concatenate([gather_pages(past['dsa_kidx'], layer, pt), ki], axis=1)
        fetch = lambda sel: paged_rows(past['dsa_kv'], layer, pt, kv_b, sel)
    topk = min(DSA_TOPK, ki_all.shape[1] // 4)
    o_dsa = over_query_blocks(lambda qq, qqi, ww, qp: dsa_block(qq, qqi, ww, qp, ki_all, fetch, topk),
                              q_b, qi, wi, pos[None])

    q_c = rope(q_n.reshape(b, t, NSA_HEADS, HEAD_DIM), pos)
    kv6 = kv_n.reshape(b, t, 6, NSA_KV_HEADS, HEAD_DIM)
    keys = rope(kv6[:, :, 0::2], pos)
    vals = kv6[:, :, 1::2]
    nsa_rows = jnp.stack([keys[:, :, 0], vals[:, :, 0], keys[:, :, 1], vals[:, :, 1]], axis=2)
    win_rows = jnp.stack([keys[:, :, 2], vals[:, :, 2]], axis=2)
    gates = jax.nn.sigmoid(g_n.astype(f32).reshape(b, t, NSA_HEADS, 3))
    if past is None:
        full = nsa_rows
    else:
        full = jnp.concatenate([gather_pages(past['nsa_kv'], layer, pt), nsa_rows], axis=1)
    kcm, vcm = block_means(full[:, :, 0]), block_means(full[:, :, 1])
    ksb, vsb = sel_blocks(full[:, :, 2]), sel_blocks(full[:, :, 3])
    o_cs = over_query_blocks(lambda qq, gg, qp: nsa_cmp_sel_block(qq, gg, qp, kcm, vcm, ksb, vsb),
                             q_c, gates, pos[None])
    if past is None:
        wpad = jnp.pad(win_rows, ((0, 0), (WINDOW, 0), (0, 0), (0, 0), (0, 0)))
        ppad = jnp.concatenate([jnp.full((WINDOW,), PAD_POS, jnp.int32), pos])

        def win_fn(qq, qp):
            span = WINDOW + qq.shape[1]
            start = qp[0, 0]
            return window_block(qq, qp, lax.dynamic_slice_in_dim(wpad, start, span, axis=1),
                                lax.dynamic_slice_in_dim(ppad, start, span, axis=0))

        o_w = over_query_blocks(win_fn, q_c, pos[None])
        win_state = win_rows[:, t - min(WINDOW, t):]
    else:
        buf = past['nsa_win']
        wb = buf.shape[1]
        kv_w = jnp.concatenate([buf, win_rows], axis=1)
        kpos = jnp.concatenate([past['past_len'] - wb + jnp.arange(wb, dtype=jnp.int32), pos])
        o_w = window_block(q_c, pos[None], kv_w, kpos)
        win_state = kv_w[:, kv_w.shape[1] - wb:]
    o_nsa = o_cs.astype(f32) + gates[..., 2:3] * o_w.astype(f32)

    qkv_m = qkv_m.astype(f32).reshape(b, t, 3, MLSTM_HEADS, MLSTM_DK).transpose(2, 0, 3, 1, 4)
    gif = gif_m.astype(f32).reshape(b, t, 2, MLSTM_HEADS).transpose(2, 0, 3, 1)
    ig = gif[0] + lw['mlstm_b_i'].astype(f32)[None, :, None]
    lf = jax.nn.log_sigmoid(gif[1] + lw['mlstm_b_f'].astype(f32)[None, :, None])
    if past is None:
        st0 = (jnp.zeros((b, MLSTM_HEADS, MLSTM_DK, MLSTM_DV), f32),
               jnp.zeros((b, MLSTM_HEADS, MLSTM_DK), f32),
               jnp.zeros((b, MLSTM_HEADS), f32))
    else:
        st0 = past['mlstm']
    (mc, mn, mm), hm = mlstm_run(qkv_m[0], qkv_m[1] / math.sqrt(MLSTM_DK), qkv_m[2], ig, lf, st0)
    hm = hm.transpose(0, 2, 1, 3)
    hm = hm * lax.rsqrt(jnp.mean(hm * hm, axis=-1, keepdims=True) + RMS_EPS)
    o_ml = hm.reshape(b, t, MIX_WIDTH) * lw['mlstm_norm_g'].astype(f32) * jax.nn.sigmoid(o_m.astype(f32))

    br = jnp.stack([o_s5, o_dsa.reshape(b, t, MIX_WIDTH).astype(f32),
                    o_nsa.reshape(b, t, MIX_WIDTH), o_ml], axis=2).astype(h.dtype)
    proj = jnp.einsum('btkw,kwd->btkd', br, lw['w_branch'])
    gate = jax.nn.sigmoid(g_br.reshape(b, t, N_BRANCH, D_MODEL))
    out = jnp.sum(gate * proj, axis=2) @ lw['w_out']
    dt = h.dtype
    state = (kv_b, ki, nsa_rows, win_state, mc.astype(dt), mn.astype(dt), mm.astype(dt),
             s_re[:, -1].astype(dt), s_im[:, -1].astype(dt))
    return out, state


def trunk_layer(x, c, pos, lw, past):
    b = x.shape[0]
    ada = (jax.nn.silu(c) @ lw['w_ada'] + lw['b_ada']).reshape(b, 3, 3, 1, D_MODEL)

    def modulate(z, j):
        return rms_norm(z, lw['norm_g'][j]) * (1.0 + ada[:, j, 1]) + ada[:, j, 0]

    x = x + 0.5 * ada[:, 0, 2] * swiglu(modulate(x, 0), lw['w_ffn1_in'], lw['w_ffn1_out'])
    mix, state = token_mixer(modulate(x, 1), pos, lw, past)
    x = x + ada[:, 1, 2] * mix
    x = x + 0.5 * ada[:, 2, 2] * swiglu(modulate(x, 2), lw['w_ffn2_in'], lw['w_ffn2_out'])
    return x, state


def setup_inputs(seed: int = 0) -> dict:
    key = jax.random.key(seed)
    ks = iter(jax.random.split(key, 64))
    f32 = jnp.float32

    def nrm(shape, scale=1.0):
        return scale * jax.random.normal(next(ks), shape, f32)

    n_pages = PAST_LEN // PAGE_SIZE
    n_used = DEC_BATCH * n_pages
    n_phys = n_used + max(1, n_used // 4)
    page_table = jax.random.permutation(next(ks), n_phys)[:n_used].reshape(DEC_BATCH, n_pages).astype(jnp.int32)
    wb = min(WINDOW, PAST_LEN)
    d_in = sum(IN_SIZES)
    return {
        'x_prompt': nrm((BATCH, SEQ, D_MODEL)),
        'x_sample': nrm((DEC_BATCH, DEC_SEQ, D_MODEL)),
        'cache_dsa_kv': nrm((DEPTH, n_phys, PAGE_SIZE, 2, DSA_HEADS, HEAD_DIM)),
        'cache_dsa_kidx': nrm((DEPTH, n_phys, PAGE_SIZE, IDX_DIM)),
        'cache_nsa_kv': nrm((DEPTH, n_phys, PAGE_SIZE, 4, NSA_KV_HEADS, HEAD_DIM)),
        'cache_nsa_win': nrm((DEPTH, DEC_BATCH, wb, 2, NSA_KV_HEADS, HEAD_DIM)),
        'state_mlstm_c': nrm((DEPTH, DEC_BATCH, MLSTM_HEADS, MLSTM_DK, MLSTM_DV), 0.1),
        'state_mlstm_n': nrm((DEPTH, DEC_BATCH, MLSTM_HEADS, MLSTM_DK), 0.1),
        'state_mlstm_m': nrm((DEPTH, DEC_BATCH, MLSTM_HEADS), 1.0),
        'state_s5_re': nrm((DEPTH, DEC_BATCH, S5_GROUPS, S5_STATE), 0.3),
        'state_s5_im': nrm((DEPTH, DEC_BATCH, S5_GROUPS, S5_STATE), 0.3),
        'page_table': page_table,
        'c_prompt': nrm((BATCH, D_MODEL)),
        'c_sample': nrm((DEC_BATCH, D_MODEL)),
        'w_ada': nrm((DEPTH, D_MODEL, 9 * D_MODEL), 0.5 * D_MODEL ** -0.5),
        'b_ada': nrm((DEPTH, 9 * D_MODEL), 0.02),
        'norm_g': 1.0 + nrm((DEPTH, 3, D_MODEL), 0.02),
        'w_ffn1_in': nrm((DEPTH, D_MODEL, 2 * D_FF), D_MODEL ** -0.5),
        'w_ffn1_out': nrm((DEPTH, D_FF, D_MODEL), D_FF ** -0.5),
        'w_ffn2_in': nrm((DEPTH, D_MODEL, 2 * D_FF), D_MODEL ** -0.5),
        'w_ffn2_out': nrm((DEPTH, D_FF, D_MODEL), D_FF ** -0.5),
        'w_in': nrm((DEPTH, D_MODEL, d_in), D_MODEL ** -0.5),
        's5_lam_re': -0.5 + nrm((DEPTH, S5_GROUPS, S5_STATE), 0.01),
        's5_lam_im': math.pi * jnp.arange(S5_STATE, dtype=f32) + nrm((DEPTH, S5_GROUPS, S5_STATE), 0.01),
        's5_log_dt': jax.random.uniform(next(ks), (DEPTH, S5_GROUPS), f32, math.log(1e-3), math.log(1e-1)),
        's5_b_re': nrm((DEPTH, S5_GROUPS, S5_STATE, S5_GROUP), (2 * S5_GROUP) ** -0.5),
        's5_b_im': nrm((DEPTH, S5_GROUPS, S5_STATE, S5_GROUP), (2 * S5_GROUP) ** -0.5),
        's5_c_re': nrm((DEPTH, S5_GROUPS, S5_GROUP, S5_STATE), (2 * S5_STATE) ** -0.5),
        's5_c_im': nrm((DEPTH, S5_GROUPS, S5_GROUP, S5_STATE), (2 * S5_STATE) ** -0.5),
        's5_d': nrm((DEPTH, MIX_WIDTH), 1.0),
        'w_s5_glu': nrm((DEPTH, MIX_WIDTH, MIX_WIDTH), MIX_WIDTH ** -0.5),
        'mlstm_b_i': nrm((DEPTH, MLSTM_HEADS), 0.1),
        'mlstm_b_f': jnp.linspace(3.0, 6.0, MLSTM_HEADS, dtype=f32) + nrm((DEPTH, MLSTM_HEADS), 0.01),
        'mlstm_norm_g': 1.0 + nrm((DEPTH, MIX_WIDTH), 0.02),
        'w_branch': nrm((DEPTH, N_BRANCH, MIX_WIDTH, D_MODEL), MIX_WIDTH ** -0.5),
        'w_out': nrm((DEPTH, D_MODEL, D_MODEL), D_MODEL ** -0.5),
        'final_norm_g': 1.0 + nrm((D_MODEL,), 0.02),
    }


def reference(x_prompt, x_sample, cache_dsa_kv, cache_dsa_kidx, cache_nsa_kv, cache_nsa_win,
              state_mlstm_c, state_mlstm_n, state_mlstm_m, state_s5_re, state_s5_im, page_table,
              c_prompt, c_sample, w_ada, b_ada, norm_g, w_ffn1_in, w_ffn1_out, w_ffn2_in, w_ffn2_out,
              w_in, s5_lam_re, s5_lam_im, s5_log_dt, s5_b_re, s5_b_im, s5_c_re, s5_c_im, s5_d, w_s5_glu,
              mlstm_b_i, mlstm_b_f, mlstm_norm_g, w_branch, w_out, final_norm_g):
    past_len = page_table.shape[1] * cache_dsa_kv.shape[2]
    pos_p = jnp.arange(x_prompt.shape[1], dtype=jnp.int32)
    pos_s = past_len + jnp.arange(x_sample.shape[1], dtype=jnp.int32)
    xp, xs = x_prompt, x_sample
    st_p, st_s = [], []
    for l in range(DEPTH):
        lw = {'w_ada': w_ada[l], 'b_ada': b_ada[l], 'norm_g': norm_g[l],
              'w_ffn1_in': w_ffn1_in[l], 'w_ffn1_out': w_ffn1_out[l],
              'w_ffn2_in': w_ffn2_in[l], 'w_ffn2_out': w_ffn2_out[l], 'w_in': w_in[l],
              's5_lam_re': s5_lam_re[l], 's5_lam_im': s5_lam_im[l], 's5_log_dt': s5_log_dt[l],
              's5_b_re': s5_b_re[l], 's5_b_im': s5_b_im[l], 's5_c_re': s5_c_re[l], 's5_c_im': s5_c_im[l],
              's5_d': s5_d[l], 'w_s5_glu': w_s5_glu[l], 'mlstm_b_i': mlstm_b_i[l], 'mlstm_b_f': mlstm_b_f[l],
              'mlstm_norm_g': mlstm_norm_g[l], 'w_branch': w_branch[l], 'w_out': w_out[l]}
        past = {'layer': l, 'page_table': page_table, 'past_len': past_len,
                'dsa_kv': cache_dsa_kv, 'dsa_kidx': cache_dsa_kidx, 'nsa_kv': cache_nsa_kv,
                'nsa_win': cache_nsa_win[l],
                'mlstm': (state_mlstm_c[l], state_mlstm_n[l], state_mlstm_m[l]),
                's5': (state_s5_re[l], state_s5_im[l])}
        xp, sp = trunk_layer(xp, c_prompt, pos_p, lw, None)
        xs, ss = trunk_layer(xs, c_sample, pos_s, lw, past)
        st_p.append(sp)
        st_s.append(ss)
    y_prompt = rms_norm(xp, final_norm_g)
    y_sample = rms_norm(xs, final_norm_g)
    return (y_prompt, y_sample,
            stk(st_p, 0), stk(st_s, 0), stk(st_p, 1), stk(st_s, 1),
            stk(st_p, 2), stk(st_s, 2), stk(st_p, 3), stk(st_s, 3),
            stk(st_p, 4), stk(st_s, 4), stk(st_p, 5), stk(st_s, 5),
            stk(st_p, 6), stk(st_s, 6), stk(st_p, 7), stk(st_s, 7),
            stk(st_p, 8), stk(st_s, 8))
```

```python
import functools
import math

import jax
import jax.numpy as jnp
import numpy as np
from jax import lax
from jax.experimental import pallas as pl
from jax.experimental.pallas import tpu as pltpu

F32 = jnp.float32
BF16 = jnp.bfloat16

D_MODEL = 2048
MIX_WIDTH = D_MODEL // 4
HEAD_DIM = 64
HALF = HEAD_DIM // 2
S5_GROUP = 16
S5_GROUPS = MIX_WIDTH // S5_GROUP
S5_STATE = 64
S5_N = S5_GROUPS * S5_STATE
DSA_HEADS = MIX_WIDTH // HEAD_DIM
IDX_HEADS = 4
IDX_DIM = 64
DSA_TOPK = 256
NSA_HEADS = MIX_WIDTH // HEAD_DIM
NSA_KV_HEADS = 2
NSA_REP = NSA_HEADS // NSA_KV_HEADS
CMP_STRIDE = 16
CMP_LEN = 2 * CMP_STRIDE
SEL_BLOCK = 64
SEL_TOPN = 16
CMP_PER_SEL = SEL_BLOCK // CMP_STRIDE
WINDOW = 512
MLSTM_HEADS = 4
MLSTM_DK = MIX_WIDTH // MLSTM_HEADS
D_FF = 2 * D_MODEL
ROPE_THETA = 10000.0
QUERY_BLOCK = 128
RMS_EPS = 1e-6
NEG_INF = -1e30
FORCE_SCORE = 1e4
N_BRANCH = 4

LANES = 128
VMEM_LIMIT = 56 * 1024 * 1024
HIGHEST = lax.Precision.HIGHEST


def _cp(*sem):
    return pltpu.CompilerParams(dimension_semantics=sem, vmem_limit_bytes=VMEM_LIMIT)


def _dot(a, b):
    return jnp.dot(a, b, preferred_element_type=F32)


def _dot_nt(a, b, precision=None):
    return lax.dot_general(a, b, (((1,), (1,)), ((), ())), preferred_element_type=F32,
                           precision=precision)


def _sigmoid(x):
    return 1.0 / (1.0 + jnp.exp(-x))


def _silu(x):
    return x * _sigmoid(x)


def _ada_kernel(c_ref, w_ref, b_ref, o_ref):
    c = c_ref[...]
    o_ref[...] = _dot(_silu(c).astype(BF16), w_ref[...]) + b_ref[...]


def ada_project(c, w, b):
    r, d = c.shape
    n = w.shape[1]
    tn = 2048
    return pl.pallas_call(
        _ada_kernel,
        grid=(n // tn,),
        in_specs=[pl.BlockSpec((r, d), lambda j: (0, 0)),
                  pl.BlockSpec((d, tn), lambda j: (0, j)),
                  pl.BlockSpec((1, tn), lambda j: (0, j))],
        out_specs=pl.BlockSpec((r, tn), lambda j: (0, j)),
        out_shape=jax.ShapeDtypeStruct((r, n), F32),
        compiler_params=_cp("parallel"),
        name="ada_project",
    )(c, w, b)


def _modnorm_kernel(x_ref, g_ref, ada_ref, o_ref, *, sub):
    x = x_ref[...]
    y = x * lax.rsqrt(jnp.mean(x * x, axis=-1, keepdims=True) + RMS_EPS)
    y = y * g_ref[...]
    if sub is not None:
        shift = ada_ref[3 * sub:3 * sub + 1, :]
        scale = ada_ref[3 * sub + 1:3 * sub + 2, :]
        y = y * (1.0 + scale) + shift
    o_ref[...] = y.astype(o_ref.dtype)


def modnorm(x, g, ada, sub, rows_per_batch, out_dtype):
    m, d = x.shape
    tm = min(512, rows_per_batch)
    nb = rows_per_batch // tm
    return pl.pallas_call(
        functools.partial(_modnorm_kernel, sub=sub),
        grid=(m // tm,),
        in_specs=[pl.BlockSpec((tm, d), lambda i: (i, 0)),
                  pl.BlockSpec((1, d), lambda i: (0, 0)),
                  pl.BlockSpec((None, 9, d), lambda i: (i // nb, 0, 0))],
        out_specs=pl.BlockSpec((tm, d), lambda i: (i, 0)),
        out_shape=jax.ShapeDtypeStruct((m, d), out_dtype),
        compiler_params=_cp("parallel"),
        name="modnorm",
    )(x, g, ada)


def _swiglu_in_kernel(h_ref, wa_ref, wg_ref, o_ref):
    h = h_ref[...]
    a = _dot(h, wa_ref[...])
    g = _dot(h, wg_ref[...])
    o_ref[...] = (_silu(a) * g).astype(o_ref.dtype)


def swiglu_in(h, w):
    m, d = h.shape
    f = w.shape[1] // 2
    tm = min(1024, m)
    tn = 512
    nj = f // tn
    return pl.pallas_call(
        _swiglu_in_kernel,
        grid=(m // tm, nj),
        in_specs=[pl.BlockSpec((tm, d), lambda i, j: (i, 0)),
                  pl.BlockSpec((d, tn), lambda i, j: (0, j)),
                  pl.BlockSpec((d, tn), lambda i, j: (0, j + nj))],
        out_specs=pl.BlockSpec((tm, tn), lambda i, j: (i, j)),
        out_shape=jax.ShapeDtypeStruct((m, f), BF16),
        compiler_params=_cp("parallel", "parallel"),
        name="swiglu_in",
    )(h, w, w)


def _mm_resid_kernel(a_ref, w_ref, x_ref, ada_ref, o_ref, *, gate_row, coef):
    y = _dot(a_ref[...], w_ref[...])
    gate = ada_ref[gate_row:gate_row + 1, :]
    o_ref[...] = x_ref[...] + (coef * gate) * y


def mm_resid(a, w, x, ada, gate_row, coef, rows_per_batch):
    m, k = a.shape
    n = w.shape[1]
    tm = min(512, rows_per_batch)
    tn = 512
    nb = rows_per_batch // tm
    return pl.pallas_call(
        functools.partial(_mm_resid_kernel, gate_row=gate_row, coef=coef),
        grid=(m // tm, n // tn),
        in_specs=[pl.BlockSpec((tm, k), lambda i, j: (i, 0)),
                  pl.BlockSpec((k, tn), lambda i, j: (0, j)),
                  pl.BlockSpec((tm, tn), lambda i, j: (i, j)),
                  pl.BlockSpec((None, 9, tn), lambda i, j: (i // nb, 0, j))],
        out_specs=pl.BlockSpec((tm, tn), lambda i, j: (i, j)),
        out_shape=jax.ShapeDtypeStruct((m, n), F32),
        compiler_params=_cp("parallel", "parallel"),
        name="mm_resid",
    )(a, w, x, ada)


def _mm_kernel(a_ref, w_ref, o_ref):
    o_ref[...] = _dot(a_ref[...].astype(BF16), w_ref[...]).astype(o_ref.dtype)


def mm(a, w, out_dtype=F32, tn=512):
    m = a.shape[0]
    k, n = w.shape
    tm = min(512, m)
    return pl.pallas_call(
        _mm_kernel,
        grid=(m // tm, n // tn),
        in_specs=[pl.BlockSpec((tm, k), lambda i, j: (i, 0)),
                  pl.BlockSpec((k, tn), lambda i, j: (0, j))],
        out_specs=pl.BlockSpec((tm, tn), lambda i, j: (i, j)),
        out_shape=jax.ShapeDtypeStruct((m, n), out_dtype),
        compiler_params=_cp("parallel", "parallel"),
        name="mm",
    )(a, w)


def _mm_nt_kernel(a_ref, wt_ref, o_ref):
    o_ref[...] = _dot_nt(a_ref[...], wt_ref[...]).astype(o_ref.dtype)


def mm_nt(a, wt, tn=512):
    m, k = a.shape
    n = wt.shape[0]
    tm = min(512, m)
    return pl.pallas_call(
        _mm_nt_kernel,
        grid=(m // tm, n // tn),
        in_specs=[pl.BlockSpec((tm, k), lambda i, j: (i, 0)),
                  pl.BlockSpec((tn, k), lambda i, j: (j, 0))],
        out_specs=pl.BlockSpec((tm, tn), lambda i, j: (i, j)),
        out_shape=jax.ShapeDtypeStruct((m, n), F32),
        compiler_params=_cp("parallel", "parallel"),
        name="mm_nt",
    )(a, wt)


def _rope_rows(x, cos, sin_signed):
    w = x.shape[1]
    lane = lax.broadcasted_iota(jnp.int32, x.shape, 1)
    fwd = pltpu.roll(x, w - HALF, axis=1)
    bwd = pltpu.roll(x, HALF, axis=1)
    swapped = jnp.where((lane % HEAD_DIM) < HALF, fwd, bwd)
    return x * cos + swapped * sin_signed


def _proj_rope_rows_kernel(h_ref, wt_ref, cos_ref, sin_ref, o_ref, *, periodic):
    z = _dot_nt(h_ref[...], wt_ref[...])
    if periodic:
        reps = z.shape[1] // LANES
        cos = jnp.concatenate([cos_ref[...]] * reps, axis=1)
        sin = jnp.concatenate([sin_ref[...]] * reps, axis=1)
    else:
        cos, sin = cos_ref[...], sin_ref[...]
    o_ref[...] = _rope_rows(z, cos, sin).astype(o_ref.dtype)


def proj_rope_rows(h, wt, cos, sin, rows_per_batch, out_dtype):
    m, k = h.shape
    n = wt.shape[0]
    tm = min(256, rows_per_batch)
    nb = rows_per_batch // tm
    tw = cos.shape[1]
    return pl.pallas_call(
        functools.partial(_proj_rope_rows_kernel, periodic=(tw != n)),
        grid=(m // tm,),
        in_specs=[pl.BlockSpec((tm, k), lambda i: (i, 0)),
                  pl.BlockSpec((n, k), lambda i: (0, 0)),
                  pl.BlockSpec((tm, tw), lambda i: (i % nb, 0)),
                  pl.BlockSpec((tm, tw), lambda i: (i % nb, 0))],
        out_specs=pl.BlockSpec((tm, n), lambda i: (i, 0)),
        out_shape=jax.ShapeDtypeStruct((m, n), out_dtype),
        compiler_params=_cp("parallel"),
        name="proj_rope_rows",
    )(h, wt, cos, sin)


KV_COLS_ROPE = (True,) * 8 + (False,) * 8 + (True,) + (True, True, False, False) * 3
KV_COLS = HEAD_DIM * len(KV_COLS_ROPE)


def _proj_cols_kernel(h_ref, wt_ref, cos_ref, sin_ref, kv_ref, ki_ref, nsa_ref, win_ref,
                      kvb_ref, kib_ref, nsab_ref, winb_ref):
    zt = _dot_nt(wt_ref[...], h_ref[...])
    cos, sin = cos_ref[...], sin_ref[...]
    parts = []
    for r, rot in enumerate(KV_COLS_ROPE):
        x1 = zt[r * HEAD_DIM:r * HEAD_DIM + HALF]
        x2 = zt[r * HEAD_DIM + HALF:(r + 1) * HEAD_DIM]
        if rot:
            parts += [x1 * cos - x2 * sin, x1 * sin + x2 * cos]
        else:
            parts += [x1, x2]
    out = jnp.concatenate(parts, axis=0)
    bounds = (0, 1024, 1088, 1600, 1856)
    for lo, hi, f_ref, b_ref in zip(bounds[:-1], bounds[1:], (kv_ref, ki_ref, nsa_ref, win_ref),
                                    (kvb_ref, kib_ref, nsab_ref, winb_ref)):
        f_ref[...] = out[lo:hi]
        b_ref[...] = out[lo:hi].astype(BF16)


def proj_cols(h, wt, cos_t, sin_t, batch, t):
    k = h.shape[1]
    tm = min(512, t)
    nt = t // tm
    widths = (1024, 64, 512, 256)
    out_shape = ([jax.ShapeDtypeStruct((batch, w, t), F32) for w in widths]
                 + [jax.ShapeDtypeStruct((batch, w, t), BF16) for w in widths])
    out_specs = [pl.BlockSpec((None, w, tm), lambda b, i: (b, 0, i)) for w in widths] * 2
    return pl.pallas_call(
        _proj_cols_kernel,
        grid=(batch, nt),
        in_specs=[pl.BlockSpec((tm, k), lambda b, i: (b * nt + i, 0)),
                  pl.BlockSpec((KV_COLS, k), lambda b, i: (0, 0)),
                  pl.BlockSpec((HALF, tm), lambda b, i: (0, i)),
                  pl.BlockSpec((HALF, tm), lambda b, i: (0, i))],
        out_specs=out_specs,
        out_shape=out_shape,
        compiler_params=_cp("parallel", "parallel"),
        name="proj_cols",
    )(h, wt, cos_t, sin_t)


S5_ROWS = S5_N // LANES


def _s5_scan_kernel(x_ref, a_ref, h0_ref, s_ref, fin_ref, carry_ref, *, tc):
    j = pl.program_id(1)

    @pl.when(j == 0)
    def _():
        carry_ref[...] = h0_ref[...]

    ar, ai = a_ref[0], a_ref[1]

    def step(t, carry):
        hr, hi = carry
        nr = ar * hr - ai * hi + x_ref[t, 0]
        ni = ar * hi + ai * hr + x_ref[t, 1]
        s_ref[t, 0] = nr
        s_ref[t, 1] = ni
        return nr, ni

    hr, hi = lax.fori_loop(0, tc, step, (carry_ref[0], carry_ref[1]), unroll=8)
    carry_ref[0] = hr
    carry_ref[1] = hi

    @pl.when(j == pl.num_programs(1) - 1)
    def _():
        fin_ref[0] = hr
        fin_ref[1] = hi


def s5_scan(x, a, h0):
    b, t = x.shape[:2]
    tc = min(256, t)
    blk = (None, tc, 2, S5_ROWS, LANES)
    st = (None, 2, S5_ROWS, LANES)
    return pl.pallas_call(
        functools.partial(_s5_scan_kernel, tc=tc),
        grid=(b, t // tc),
        in_specs=[pl.BlockSpec(blk, lambda i, j: (i, j, 0, 0, 0)),
                  pl.BlockSpec((2, S5_ROWS, LANES), lambda i, j: (0, 0, 0)),
                  pl.BlockSpec(st, lambda i, j: (i, 0, 0, 0))],
        out_specs=[pl.BlockSpec(blk, lambda i, j: (i, j, 0, 0, 0)),
                   pl.BlockSpec(st, lambda i, j: (i, 0, 0, 0))],
        out_shape=[jax.ShapeDtypeStruct(x.shape, F32), jax.ShapeDtypeStruct(h0.shape, F32)],
        scratch_shapes=[pltpu.VMEM((2, S5_ROWS, LANES), F32)],
        compiler_params=_cp("parallel", "arbitrary"),
        name="s5_scan",
    )(x, a, h0)


def _gelu_tanh(x):
    return 0.5 * x * (1.0 + jnp.tanh(math.sqrt(2.0 / math.pi) * (x + 0.044715 * (x * x * x))))


def _s5_out_kernel(s_ref, c_ref, u_ref, d_ref, w_ref, o_ref):
    y = _dot(s_ref[...].astype(BF16), c_ref[...]) + d_ref[...] * u_ref[...]
    y = _gelu_tanh(y)
    o_ref[...] = (y * _sigmoid(_dot(y.astype(BF16), w_ref[...]))).astype(o_ref.dtype)


def s5_out(s, cmat, zp, d, wglu):
    m = s.shape[0]
    tm = min(512, m)
    return pl.pallas_call(
        _s5_out_kernel,
        grid=(m // tm,),
        in_specs=[pl.BlockSpec((tm, 2 * S5_N), lambda i: (i, 0)),
                  pl.BlockSpec((2 * S5_N, MIX_WIDTH), lambda i: (0, 0)),
                  pl.BlockSpec((tm, MIX_WIDTH), lambda i: (i, 0)),
                  pl.BlockSpec((1, MIX_WIDTH), lambda i: (0, 0)),
                  pl.BlockSpec((MIX_WIDTH, MIX_WIDTH), lambda i: (0, 0))],
        out_specs=pl.BlockSpec((tm, MIX_WIDTH), lambda i: (i, 0)),
        out_shape=jax.ShapeDtypeStruct((m, MIX_WIDTH), BF16),
        compiler_params=_cp("parallel"),
        name="s5_out",
    )(s, cmat, zp, d, wglu)


WI_LANE = 0
GN_LANE = WI_LANE + IDX_HEADS
IG_LANE = GN_LANE + 3 * NSA_HEADS
FG_LANE = IG_LANE + MLSTM_HEADS


def _mlstm_kernel(q_ref, k_ref, v_ref, gates_ref, om_ref, bias_ref, g_ref, c0_ref, n0_ref, m0_ref,
                  o_ref, c_out, n_out, m_out, c_s, n_s, m_s, *, chunk):
    j = pl.program_id(1)

    @pl.when(j == 0)
    def _():
        c_s[...] = c0_ref[...]
        n_s[...] = n0_ref[...]
        m_s[...] = m0_ref[...]

    gates = gates_ref[...] + bias_ref[...]
    lane = lax.broadcasted_iota(jnp.int32, gates.shape, 1)
    is_f = (lane >= FG_LANE) & (lane < FG_LANE + MLSTM_HEADS)
    logsig = jnp.minimum(gates, 0.0) - jnp.log(1.0 + jnp.exp(-jnp.abs(gates)))
    gl = jnp.where(is_f, logsig, gates)
    row = lax.broadcasted_iota(jnp.int32, (chunk, chunk), 0)
    col = lax.broadcasted_iota(jnp.int32, (chunk, chunk), 1)
    causal = col <= row
    cum = jnp.dot(causal.astype(F32), gl, preferred_element_type=F32, precision=HIGHEST)
    gl_t = gl.T
    cum_t = cum.T
    scale = 1.0 / math.sqrt(MLSTM_DK)
    for h in range(MLSTM_HEADS):
        sl = slice(h * MLSTM_DK, (h + 1) * MLSTM_DK)
        q = q_ref[:, sl]
        k = k_ref[:, sl] * scale
        v = v_ref[:, sl]
        qb, kb, vb = q.astype(BF16), k.astype(BF16), v.astype(BF16)
        c = c_s[h]
        n = n_s[h:h + 1, :]
        m = m_s[h:h + 1, 0:1]
        cum_c = cum[:, FG_LANE + h:FG_LANE + h + 1]
        cum_r = cum_t[FG_LANE + h:FG_LANE + h + 1, :]
        ig_c = gl[:, IG_LANE + h:IG_LANE + h + 1]
        ig_r = gl_t[IG_LANE + h:IG_LANE + h + 1, :]
        logd = jnp.where(causal, cum_c - cum_r + ig_r, NEG_INF)
        log_state = cum_c + m
        m_t = jnp.maximum(log_state, jnp.max(logd, axis=1, keepdims=True))
        w_in = jnp.exp(logd - m_t)
        w_st = jnp.exp(log_state - m_t)
        s = _dot_nt(qb, kb) * w_in
        num = w_st * _dot(qb, c.astype(BF16)) + _dot(s.astype(BF16), vb)
        den = w_st * jnp.sum(q * n, axis=1, keepdims=True) + jnp.sum(s, axis=1, keepdims=True)
        hh = num / jnp.maximum(jnp.abs(den), jnp.exp(-m_t))
        total = cum_r[:, chunk - 1:chunk]
        m_new = jnp.maximum(total + m, jnp.max(total - cum_r + ig_r, axis=1, keepdims=True))
        a = jnp.exp(total + m - m_new)
        ws_c = jnp.exp(total - cum_c + ig_c - m_new)
        kw = k * ws_c
        c_s[h] = a * c + lax.dot_general(kw.astype(BF16), vb, (((0,), (0,)), ((), ())),
                                         preferred_element_type=F32)
        n_s[h:h + 1, :] = a * n + jnp.sum(kw, axis=0, keepdims=True)
        m_s[h:h + 1, :] = jnp.broadcast_to(m_new, (1, LANES))
        hn = hh * lax.rsqrt(jnp.mean(hh * hh, axis=1, keepdims=True) + RMS_EPS)
        o_ref[:, sl] = (hn * g_ref[:, sl] * _sigmoid(om_ref[:, sl])).astype(o_ref.dtype)

    @pl.when(j == pl.num_programs(1) - 1)
    def _():
        c_out[...] = c_s[...]
        n_out[...] = n_s[...]
        m_out[...] = m_s[...]


def mlstm(zp, bias, norm_g, c0, n0, m0, batch, t, chunk, cols):
    nc = t // chunk
    q_col, k_col, v_col, small_col, om_col = cols
    wide = lambda col: pl.BlockSpec((chunk, MIX_WIDTH), lambda b, j: (b * nc + j, col // MIX_WIDTH))
    st = lambda shape: pl.BlockSpec((None,) + shape, lambda b, j: (b,) + (0,) * len(shape))
    return pl.pallas_call(
        functools.partial(_mlstm_kernel, chunk=chunk),
        grid=(batch, nc),
        in_specs=[wide(q_col), wide(k_col), wide(v_col),
                  pl.BlockSpec((chunk, LANES), lambda b, j: (b * nc + j, small_col // LANES)),
                  wide(om_col),
                  pl.BlockSpec((1, LANES), lambda b, j: (0, 0)),
                  pl.BlockSpec((1, MIX_WIDTH), lambda b, j: (0, 0)),
                  st((MLSTM_HEADS, MLSTM_DK, MLSTM_DK)), st((8, LANES)), st((8, LANES))],
        out_specs=[pl.BlockSpec((chunk, MIX_WIDTH), lambda b, j: (b * nc + j, 0)),
                   st((MLSTM_HEADS, MLSTM_DK, MLSTM_DK)), st((8, LANES)), st((8, LANES))],
        out_shape=[jax.ShapeDtypeStruct((batch * t, MIX_WIDTH), BF16),
                   jax.ShapeDtypeStruct((batch, MLSTM_HEADS, MLSTM_DK, MLSTM_DK), F32),
                   jax.ShapeDtypeStruct((batch, 8, LANES), F32),
                   jax.ShapeDtypeStruct((batch, 8, LANES), F32)],
        scratch_shapes=[pltpu.VMEM((MLSTM_HEADS, MLSTM_DK, MLSTM_DK), F32),
                        pltpu.VMEM((8, LANES), F32), pltpu.VMEM((8, LANES), F32)],
        compiler_params=_cp("parallel", "arbitrary"),
        name="mlstm",
    )(zp, zp, zp, zp, zp, bias, norm_g, c0, n0, m0)


INT_MIN = -2 ** 31


def _count(mask):
    return jnp.sum(mask.astype(F32), axis=1, keepdims=True)


def _sort_key(score):
    bits = pltpu.bitcast(score, jnp.int32)
    key = bits ^ ((bits >> 31) & 0x7FFFFFFF)
    return jnp.where(score == 0.0, 0, key)


def _count_paged(mask):
    return jnp.sum(jnp.sum(mask.astype(F32), axis=2, keepdims=True), axis=0, keepdims=True)


def _topk_mask(score, k, paged=False):
    count = _count_paged if paged else _count
    if paged:
        idx = (lax.broadcasted_iota(jnp.int32, score.shape, 0) * LANES
               + lax.broadcasted_iota(jnp.int32, score.shape, 2))
        s = score.shape[0] * LANES
    else:
        idx = lax.broadcasted_iota(jnp.int32, score.shape, 1)
        s = score.shape[1]
    key = _sort_key(score)
    kf = float(k)
    prefix = jnp.where(count(key >= 0) >= kf, 0, INT_MIN).astype(jnp.int32)

    def value_bit(b, prefix):
        cand = prefix | lax.shift_left(jnp.int32(1), 30 - b)
        return jnp.where(count(key >= cand) >= kf, cand, prefix)

    thr = lax.fori_loop(0, 31, value_bit, prefix)
    above = key > thr
    tie = key == thr
    need = kf - count(above)
    nbits = max(1, (s - 1).bit_length())

    def index_bit(b, p):
        cand = p | lax.shift_left(jnp.int32(1), nbits - 1 - b)
        return jnp.where(count(tie & (idx < cand)) < need, cand, p)

    last = lax.fori_loop(0, nbits, index_bit, jnp.zeros_like(prefix))
    return above | (tie & (idx <= last))


def _masked_softmax_pv(logits, keep, vt):
    logits = jnp.where(keep, logits, NEG_INF)
    p = jnp.exp(logits - jnp.max(logits, axis=1, keepdims=True))
    return _dot_nt(p.astype(BF16), vt) / jnp.sum(p, axis=1, keepdims=True)


ATT_SCALE = 1.0 / math.sqrt(HEAD_DIM)


def _dsa_prompt_kernel(q_ref, qi_ref, small_ref, ki_ref, k_ref, v_ref, o_ref, keep_s, o_s, *, topk):
    i = pl.program_id(1)
    qb, s = keep_s.shape
    tq = i * qb + lax.broadcasted_iota(jnp.int32, (qb, 1), 0)
    kpos = lax.broadcasted_iota(jnp.int32, (qb, s), 1)
    causal = kpos <= tq
    ki = ki_ref[...]
    score = jnp.zeros((qb, s), F32)
    for h in range(IDX_HEADS):
        rel = jnp.maximum(_dot(qi_ref[h], ki), 0.0)
        score = score + small_ref[:, WI_LANE + h:WI_LANE + h + 1] * rel
    score = jnp.where(causal, score, NEG_INF)
    keep = _topk_mask(score, topk) & causal
    keep_s[...] = keep.astype(F32)

    def head(h, carry):
        rows = pl.ds(pl.multiple_of(h * HEAD_DIM, HEAD_DIM), HEAD_DIM)
        logits = _dot(q_ref[h], k_ref[rows, :]) * ATT_SCALE
        o_s[h] = _masked_softmax_pv(logits, keep_s[...] > 0.5, v_ref[rows, :])
        return carry

    lax.fori_loop(0, DSA_HEADS, head, 0)
    for h in range(DSA_HEADS):
        o_ref[:, h * HEAD_DIM:(h + 1) * HEAD_DIM] = o_s[h].astype(o_ref.dtype)


def dsa_prompt(qh, zp, small_col, ki_t, kv_t, batch, t):
    qb = QUERY_BLOCK
    nq = t // qb
    topk = min(DSA_TOPK, t // 4)
    return pl.pallas_call(
        functools.partial(_dsa_prompt_kernel, topk=topk),
        grid=(batch, nq),
        in_specs=[pl.BlockSpec((DSA_HEADS, qb, HEAD_DIM), lambda b, i: (0, b * nq + i, 0)),
                  pl.BlockSpec((IDX_HEADS, qb, HEAD_DIM), lambda b, i: (4, b * nq + i, 0)),
                  pl.BlockSpec((qb, LANES), lambda b, i: (b * nq + i, small_col // LANES)),
                  pl.BlockSpec((None, IDX_DIM, t), lambda b, i: (b, 0, 0)),
                  pl.BlockSpec((None, MIX_WIDTH, t), lambda b, i: (b, 0, 0)),
                  pl.BlockSpec((None, MIX_WIDTH, t), lambda b, i: (b, 1, 0))],
        out_specs=pl.BlockSpec((qb, MIX_WIDTH), lambda b, i: (b * nq + i, 0)),
        out_shape=jax.ShapeDtypeStruct((batch * t, MIX_WIDTH), BF16),
        scratch_shapes=[pltpu.VMEM((qb, t), F32), pltpu.VMEM((DSA_HEADS, qb, HEAD_DIM), F32)],
        compiler_params=_cp("parallel", "arbitrary"),
        name="dsa_prompt",
    )(qh, qh, zp, ki_t, kv_t, kv_t)


NSA_GROUP_ROWS = NSA_KV_HEADS * HEAD_DIM


def _nsa_pool_kernel(x_ref, o_ref):
    t = x_ref.shape[1]
    nsub = t // CMP_STRIDE
    tok = lax.broadcasted_iota(jnp.int32, (t, nsub), 0)
    c = lax.broadcasted_iota(jnp.int32, (t, nsub), 1)
    inside = (tok >= c * CMP_STRIDE) & (tok < c * CMP_STRIDE + CMP_LEN) & (c < nsub - 1)
    pool = jnp.where(inside, 1.0 / CMP_LEN, 0.0).astype(F32)
    o_ref[...] = jnp.dot(x_ref[...], pool, preferred_element_type=F32, precision=HIGHEST).astype(o_ref.dtype)


def nsa_pool(nsa_t, batch, t):
    rows = 2 * NSA_GROUP_ROWS
    return pl.pallas_call(
        _nsa_pool_kernel,
        grid=(batch,),
        in_specs=[pl.BlockSpec((None, rows, t), lambda b: (b, 0, 0))],
        out_specs=pl.BlockSpec((None, rows, t // CMP_STRIDE), lambda b: (b, 0, 0)),
        out_shape=jax.ShapeDtypeStruct((batch, rows, t // CMP_STRIDE), BF16),
        compiler_params=_cp("parallel"),
        name="nsa_pool",
    )(nsa_t)


def _nsa_select_blocks(pcsum, tq, ns, n_sel, lanes=None):
    rows, nc = pcsum.shape
    lanes = ns if lanes is None else lanes
    c = lax.broadcasted_iota(jnp.int32, (nc, lanes), 0)
    j = lax.broadcasted_iota(jnp.int32, (nc, lanes), 1)
    pool = ((c >= j * CMP_PER_SEL) & (c < (j + 1) * CMP_PER_SEL)).astype(F32)
    imp = jnp.dot(pcsum, pool, preferred_element_type=F32, precision=HIGHEST)
    blk = lax.broadcasted_iota(jnp.int32, (rows, lanes), 1)
    forced = (blk == tq // SEL_BLOCK) | (blk == 0)
    imp = jnp.where(forced, FORCE_SCORE, imp)
    imp = jnp.where((blk * SEL_BLOCK <= tq) & (blk < ns), imp, NEG_INF)
    return _topk_mask(imp, n_sel) & (blk < ns)


def _nsa_prompt_kernel(q_ref, small_ref, cm_ref, ksel_ref, vsel_ref, w0, w1, w2, w3, w4, o_ref,
                       keep_s, ocmp_s, osel_s, *, ns, n_sel):
    i = pl.program_id(1)
    qb, s = keep_s.shape
    tq = i * qb + lax.broadcasted_iota(jnp.int32, (qb, 1), 0)
    kpos = lax.broadcasted_iota(jnp.int32, (qb, s), 1)
    ncp = cm_ref.shape[1]
    cidx = lax.broadcasted_iota(jnp.int32, (qb, ncp), 1)
    cvalid = (cidx * CMP_STRIDE + CMP_LEN - 1 <= tq) & (cidx < ncp - 1)
    for g in range(NSA_KV_HEADS):
        grows = slice(g * HEAD_DIM, (g + 1) * HEAD_DIM)
        vrows = slice(NSA_GROUP_ROWS + g * HEAD_DIM, NSA_GROUP_ROWS + (g + 1) * HEAD_DIM)
        kcm = cm_ref[grows, :]
        vcm = cm_ref[vrows, :]
        pcsum = jnp.zeros((qb, ncp), F32)
        for r in range(NSA_REP):
            h = g * NSA_REP + r
            lc = jnp.where(cvalid, _dot(q_ref[h], kcm) * ATT_SCALE, NEG_INF)
            p = jnp.exp(lc - jnp.max(lc, axis=1, keepdims=True))
            pc = jnp.where(cvalid, p / jnp.sum(p, axis=1, keepdims=True), 0.0)
            ocmp_s[h] = _dot_nt(pc.astype(BF16), vcm)
            pcsum = pcsum + pc
        sel = _nsa_select_blocks(pcsum, tq, ns, n_sel)
        bj = lax.broadcasted_iota(jnp.int32, (sel.shape[1], s), 0)
        tk = lax.broadcasted_iota(jnp.int32, (sel.shape[1], s), 1)
        expand = ((tk >= bj * SEL_BLOCK) & (tk < (bj + 1) * SEL_BLOCK)).astype(BF16)
        tok_sel = _dot(sel.astype(BF16), expand) > 0.5
        keep_s[...] = (tok_sel & (kpos <= tq)).astype(F32)

        def rep(r, carry, g=g, grows=grows):
            h = g * NSA_REP + r
            logits = _dot(q_ref[h], ksel_ref[grows, :]) * ATT_SCALE
            osel_s[h] = _masked_softmax_pv(logits, keep_s[...] > 0.5, vsel_ref[grows, :])
            return carry

        lax.fori_loop(0, NSA_REP, rep, 0)

    wins = (w0, w1, w2, w3, w4)
    nwin = len(wins)
    kw = jnp.concatenate([w[...] for w in wins], axis=1)
    lane = lax.broadcasted_iota(jnp.int32, (qb, nwin * LANES), 1)
    wpos = (i - (nwin - 1)) * LANES + lane
    wok = (wpos >= 0) & (wpos <= tq) & (wpos > tq - WINDOW)
    gates = _sigmoid(small_ref[...])
    for h in range(NSA_HEADS):
        g = h // NSA_REP
        kwin = kw[g * HEAD_DIM:(g + 1) * HEAD_DIM]
        vwin = kw[NSA_GROUP_ROWS + g * HEAD_DIM:NSA_GROUP_ROWS + (g + 1) * HEAD_DIM]
        o_w = _masked_softmax_pv(_dot(q_ref[h], kwin) * ATT_SCALE, wok, vwin)
        gl = GN_LANE + 3 * h
        o = (gates[:, gl:gl + 1] * ocmp_s[h] + gates[:, gl + 1:gl + 2] * osel_s[h]
             + gates[:, gl + 2:gl + 3] * o_w)
        o_ref[:, h * HEAD_DIM:(h + 1) * HEAD_DIM] = o.astype(o_ref.dtype)


def nsa_prompt(qh, zp, small_col, cm_t, nsa_t, win_t, batch, t):
    qb = QUERY_BLOCK
    nq = t // qb
    ns = -(-t // SEL_BLOCK)
    nwin = WINDOW // qb + 1
    win_specs = [pl.BlockSpec((None, 2 * NSA_GROUP_ROWS, qb),
                              lambda b, i, j=j: (b, 0, jnp.maximum(i - (nwin - 1) + j, 0)))
                 for j in range(nwin)]
    return pl.pallas_call(
        functools.partial(_nsa_prompt_kernel, ns=ns, n_sel=min(SEL_TOPN, ns)),
        grid=(batch, nq),
        in_specs=[pl.BlockSpec((NSA_HEADS, qb, HEAD_DIM), lambda b, i: (1, b * nq + i, 0)),
                  pl.BlockSpec((qb, LANES), lambda b, i: (b * nq + i, small_col // LANES)),
                  pl.BlockSpec((None, 2 * NSA_GROUP_ROWS, cm_t.shape[2]), lambda b, i: (b, 0, 0)),
                  pl.BlockSpec((None, NSA_GROUP_ROWS, t), lambda b, i: (b, 2, 0)),
                  pl.BlockSpec((None, NSA_GROUP_ROWS, t), lambda b, i: (b, 3, 0))] + win_specs,
        out_specs=pl.BlockSpec((qb, MIX_WIDTH), lambda b, i: (b * nq + i, 0)),
        out_shape=jax.ShapeDtypeStruct((batch * t, MIX_WIDTH), BF16),
        scratch_shapes=[pltpu.VMEM((qb, t), F32), pltpu.VMEM((NSA_HEADS, qb, HEAD_DIM), F32),
                        pltpu.VMEM((NSA_HEADS, qb, HEAD_DIM), F32)],
        compiler_params=_cp("parallel", "arbitrary"),
        name="nsa_prompt",
    )(qh, zp, cm_t, nsa_t, nsa_t, *([win_t] * nwin))


PAGES_PER_STEP = 8


def _page_specs(block, layer, n_pages, slot=None, per_step=PAGES_PER_STEP):
    def spec(j):
        def index(b, s, pt):
            page = pt[b, jnp.minimum(s * per_step + j, n_pages - 1)]
            lead = (layer, page) if slot is None else (layer, page, slot)
            return lead + (0,) * (len(block) - len(lead))
        return pl.BlockSpec(block, index)
    return [spec(j) for j in range(per_step)]


def _dsa_sample_scores_kernel(pt_ref, qi_ref, small_ref, kinew_ref, *rest, n_steps):
    pages, o_ref = rest[:PAGES_PER_STEP], rest[PAGES_PER_STEP]
    s = pl.program_id(1)
    tnew = qi_ref.shape[1]

    def score(keys_t=None, keys=None):
        acc = None
        for h in range(IDX_HEADS):
            d = _dot(qi_ref[h], keys_t) if keys is None else _dot_nt(qi_ref[h], keys)
            term = small_ref[:, WI_LANE + h:WI_LANE + h + 1] * jnp.maximum(d, 0.0)
            acc = term if acc is None else acc + term
        return acc

    @pl.when(s < n_steps - 1)
    def _():
        for j, page in enumerate(pages):
            o_ref[j] = score(keys_t=page[...].astype(BF16))

    @pl.when(s == n_steps - 1)
    def _():
        sc = score(keys=kinew_ref[...])
        q = lax.broadcasted_iota(jnp.int32, sc.shape, 0)
        k = lax.broadcasted_iota(jnp.int32, sc.shape, 1)
        o_ref[0] = jnp.where((k <= q) & (k < tnew), sc, NEG_INF)
        for j in range(1, PAGES_PER_STEP):
            o_ref[j] = jnp.full(sc.shape, NEG_INF, F32)


def dsa_sample_scores(page_table, qh, zp, ki_new, kidx_view, layer):
    batch, n_pages = page_table.shape
    tnew = qh.shape[1] // batch
    n_steps = n_pages // PAGES_PER_STEP + 1
    return pl.pallas_call(
        functools.partial(_dsa_sample_scores_kernel, n_steps=n_steps),
        grid_spec=pltpu.PrefetchScalarGridSpec(
            num_scalar_prefetch=1,
            grid=(batch, n_steps),
            in_specs=[pl.BlockSpec((IDX_HEADS, tnew, HEAD_DIM), lambda b, s, pt: (4, b, 0)),
                      pl.BlockSpec((tnew, LANES), lambda b, s, pt: (b, ZP_SMALL // LANES)),
                      pl.BlockSpec((None, LANES, IDX_DIM), lambda b, s, pt: (b, 0, 0))]
            + _page_specs((None, None, IDX_DIM, LANES), layer, n_pages),
            out_specs=pl.BlockSpec((None, PAGES_PER_STEP, tnew, LANES), lambda b, s, pt: (b, s, 0, 0))),
        out_shape=jax.ShapeDtypeStruct((batch, PAGES_PER_STEP * n_steps, tnew, LANES), F32),
        compiler_params=_cp("parallel", "arbitrary"),
        name="dsa_sample_scores",
    )(page_table, qh, zp, ki_new, *([kidx_view] * PAGES_PER_STEP))


def _online_softmax_step(logits, keep, v_t, m_ref, l_ref, acc_ref, v_rows=None):
    lm = jnp.where(keep, logits, NEG_INF)
    m_old = m_ref[...]
    m_new = jnp.maximum(m_old, jnp.max(lm, axis=1, keepdims=True))
    alpha = jnp.exp(m_old - m_new)
    p = jnp.where(keep, jnp.exp(lm - m_new), 0.0)
    pv = _dot_nt(p.astype(BF16), v_t) if v_rows is None else _dot(p.astype(BF16), v_rows)
    l_ref[...] = alpha * l_ref[...] + jnp.sum(p, axis=1, keepdims=True)
    acc_ref[...] = alpha * acc_ref[...] + pv
    m_ref[...] = m_new


def _dsa_sample_attn_kernel(pt_ref, score_ref, qbd_ref, kvnew_ref, *rest, n_steps, topk):
    pages, o_ref = rest[:PAGES_PER_STEP], rest[PAGES_PER_STEP]
    keep_s, m_s, l_s, acc_s = rest[PAGES_PER_STEP + 1:]
    s = pl.program_id(1)
    tnew = score_ref.shape[1]

    @pl.when(s == 0)
    def _():
        keep_s[...] = _topk_mask(score_ref[...], topk, paged=True).astype(F32)
        m_s[...] = jnp.full(m_s.shape, NEG_INF, F32)
        l_s[...] = jnp.zeros(l_s.shape, F32)
        acc_s[...] = jnp.zeros(acc_s.shape, F32)

    qbd = qbd_ref[...]

    def keep_rows(page):
        return jnp.concatenate([keep_s[page]] * DSA_HEADS, axis=0) > 0.5

    @pl.when(s < n_steps - 1)
    def _():
        for j, page in enumerate(pages):
            k_t = page[0].reshape(MIX_WIDTH, LANES).astype(BF16)
            v_t = page[1].reshape(MIX_WIDTH, LANES).astype(BF16)
            logits = _dot(qbd, k_t) * ATT_SCALE
            _online_softmax_step(logits, keep_rows(s * PAGES_PER_STEP + j), v_t, m_s, l_s, acc_s)

    @pl.when(s == n_steps - 1)
    def _():
        kv = kvnew_ref[...]
        logits = _dot_nt(qbd, kv[:, :MIX_WIDTH]) * ATT_SCALE
        q = lax.broadcasted_iota(jnp.int32, logits.shape, 0) % tnew
        k = lax.broadcasted_iota(jnp.int32, logits.shape, 1)
        keep = keep_rows((n_steps - 1) * PAGES_PER_STEP) & (k <= q) & (k < tnew)
        _online_softmax_step(logits, keep, None, m_s, l_s, acc_s, v_rows=kv[:, MIX_WIDTH:])
        out = acc_s[...] / l_s[...]
        for h in range(DSA_HEADS):
            o_ref[:, h * HEAD_DIM:(h + 1) * HEAD_DIM] = (
                out[h * tnew:(h + 1) * tnew, h * HEAD_DIM:(h + 1) * HEAD_DIM].astype(o_ref.dtype))


def dsa_sample_attn(page_table, scores, qbd, kv_new, kv_view, layer):
    batch, n_pages = page_table.shape
    tnew = scores.shape[2]
    n_steps = n_pages // PAGES_PER_STEP + 1
    topk = min(DSA_TOPK, (n_pages * LANES + tnew) // 4)
    rows = DSA_HEADS * tnew
    return pl.pallas_call(
        functools.partial(_dsa_sample_attn_kernel, n_steps=n_steps, topk=topk),
        grid_spec=pltpu.PrefetchScalarGridSpec(
            num_scalar_prefetch=1,
            grid=(batch, n_steps),
            in_specs=[pl.BlockSpec((None,) + scores.shape[1:], lambda b, s, pt: (b, 0, 0, 0)),
                      pl.BlockSpec((None, rows, MIX_WIDTH), lambda b, s, pt: (b, 0, 0)),
                      pl.BlockSpec((None, LANES, 2 * MIX_WIDTH), lambda b, s, pt: (b, 0, 0))]
            + _page_specs((None, None, 2, DSA_HEADS, HEAD_DIM, LANES), layer, n_pages),
            out_specs=pl.BlockSpec((tnew, MIX_WIDTH), lambda b, s, pt: (b, 0)),
            scratch_shapes=[pltpu.VMEM(scores.shape[1:], F32), pltpu.VMEM((rows, 1), F32),
                            pltpu.VMEM((rows, 1), F32), pltpu.VMEM((rows, MIX_WIDTH), F32)]),
        out_shape=jax.ShapeDtypeStruct((batch * tnew, MIX_WIDTH), BF16),
        compiler_params=_cp("parallel", "arbitrary"),
        name="dsa_sample_attn",
    )(page_table, scores, qbd, kv_new, *([kv_view] * PAGES_PER_STEP))


CMP_PAGES_PER_STEP = 16
SUB_PER_PAGE = LANES // CMP_STRIDE


def _nsa_sample_cmp_kernel(pt_ref, q_ref, *rest, n_steps, past, ns, n_sel):
    pages = rest[:CMP_PAGES_PER_STEP]
    ocmp_ref, sel_ref, sub_s = rest[CMP_PAGES_PER_STEP:]
    s = pl.program_id(1)
    tnew = q_ref.shape[1]
    rows = 2 * NSA_GROUP_ROWS
    tok = lax.broadcasted_iota(jnp.int32, (LANES, LANES), 0)
    col = lax.broadcasted_iota(jnp.int32, (LANES, LANES), 1)
    sub = None
    for j, page in enumerate(pages):
        pool = jnp.where(col == j * SUB_PER_PAGE + tok // CMP_STRIDE, 1.0 / CMP_STRIDE, 0.0).astype(F32)
        term = jnp.dot(page[...].reshape(rows, LANES), pool, preferred_element_type=F32, precision=HIGHEST)
        sub = term if sub is None else sub + term
    sub_s[s] = sub

    @pl.when(s == n_steps - 1)
    def _():
        sub_all = jnp.concatenate([sub_s[i] for i in range(n_steps)], axis=1)
        ncp = sub_all.shape[1]
        cm = (0.5 * (sub_all + pltpu.roll(sub_all, ncp - 1, axis=1))).astype(BF16)
        nc = (past + tnew) // CMP_STRIDE - 1
        tq = past + lax.broadcasted_iota(jnp.int32, (tnew, 1), 0)
        cidx = lax.broadcasted_iota(jnp.int32, (tnew, ncp), 1)
        cvalid = (cidx * CMP_STRIDE + CMP_LEN - 1 <= tq) & (cidx < nc)
        for g in range(NSA_KV_HEADS):
            kcm = cm[g * HEAD_DIM:(g + 1) * HEAD_DIM]
            vcm = cm[NSA_GROUP_ROWS + g * HEAD_DIM:NSA_GROUP_ROWS + (g + 1) * HEAD_DIM]
            pcsum = jnp.zeros((tnew, ncp), F32)
            for r in range(NSA_REP):
                h = g * NSA_REP + r
                lc = jnp.where(cvalid, _dot(q_ref[h], kcm) * ATT_SCALE, NEG_INF)
                p = jnp.exp(lc - jnp.max(lc, axis=1, keepdims=True))
                pc = jnp.where(cvalid, p / jnp.sum(p, axis=1, keepdims=True), 0.0)
                ocmp_ref[h] = _dot_nt(pc.astype(BF16), vcm)
                pcsum = pcsum + pc
            sel_ref[g] = _nsa_select_blocks(pcsum, tq, ns, n_sel, lanes=sel_ref.shape[-1]).astype(F32)


def nsa_sample_cmp(page_table, qh, nsa_view, layer):
    batch, n_pages = page_table.shape
    tnew = qh.shape[1] // batch
    past = n_pages * LANES
    assert tnew < CMP_STRIDE and n_pages % CMP_PAGES_PER_STEP == 0
    n_steps = n_pages // CMP_PAGES_PER_STEP
    ns = -(-(past + tnew) // SEL_BLOCK)
    ns_lanes = -(-ns // LANES) * LANES
    return pl.pallas_call(
        functools.partial(_nsa_sample_cmp_kernel, n_steps=n_steps, past=past, ns=ns, n_sel=min(SEL_TOPN, ns)),
        grid_spec=pltpu.PrefetchScalarGridSpec(
            num_scalar_prefetch=1,
            grid=(batch, n_steps),
            in_specs=[pl.BlockSpec((NSA_HEADS, tnew, HEAD_DIM), lambda b, s, pt: (1, b, 0))]
            + _page_specs((None, None, 2, NSA_KV_HEADS, HEAD_DIM, LANES), layer, n_pages, slot=0,
                          per_step=CMP_PAGES_PER_STEP),
            out_specs=[pl.BlockSpec((None, NSA_HEADS, tnew, HEAD_DIM), lambda b, s, pt: (b, 0, 0, 0)),
                       pl.BlockSpec((None, NSA_KV_HEADS, tnew, ns_lanes), lambda b, s, pt: (b, 0, 0, 0))],
            scratch_shapes=[pltpu.VMEM((n_steps, 2 * NSA_GROUP_ROWS, LANES), F32)]),
        out_shape=[jax.ShapeDtypeStruct((batch, NSA_HEADS, tnew, HEAD_DIM), F32),
                   jax.ShapeDtypeStruct((batch, NSA_KV_HEADS, tnew, ns_lanes), F32)],
        compiler_params=_cp("parallel", "arbitrary"),
        name="nsa_sample_cmp",
    )(page_table, qh, *([nsa_view] * CMP_PAGES_PER_STEP))


def _nsa_sample_sel_kernel(pt_ref, q_ref, sel_ref, ocmp_ref, small_ref, new_ref, wbuf_ref, wnew_ref, *rest,
                           n_steps, past):
    pages, o_ref = rest[:PAGES_PER_STEP], rest[PAGES_PER_STEP]
    m_s, l_s, acc_s = rest[PAGES_PER_STEP + 1:]
    s = pl.program_id(1)
    tnew = q_ref.shape[1]
    grp_rows = NSA_REP * tnew

    @pl.when(s == 0)
    def _():
        m_s[...] = jnp.full(m_s.shape, NEG_INF, F32)
        l_s[...] = jnp.zeros(l_s.shape, F32)
        acc_s[...] = jnp.zeros(acc_s.shape, F32)

    q_all = q_ref[...].reshape(NSA_HEADS * tnew, HEAD_DIM)

    @pl.when(s < n_steps - 1)
    def _():
        step_tokens = PAGES_PER_STEP * LANES
        nsl = sel_ref.shape[2]
        bj = lax.broadcasted_iota(jnp.int32, (nsl, step_tokens), 0)
        tk = lax.broadcasted_iota(jnp.int32, (nsl, step_tokens), 1)
        expand = (bj == s * (step_tokens // SEL_BLOCK) + tk // SEL_BLOCK).astype(BF16)
        for g in range(NSA_KV_HEADS):
            keep_g = _dot(sel_ref[g].astype(BF16), expand) > 0.5
            qg = q_all[g * grp_rows:(g + 1) * grp_rows]
            for j, page in enumerate(pages):
                keep = jnp.concatenate([keep_g[:, j * LANES:(j + 1) * LANES]] * NSA_REP, axis=0)
                logits = _dot(qg, page[0, g].astype(BF16)) * ATT_SCALE
                _online_softmax_step(logits, keep, page[1, g].astype(BF16), m_s.at[g], l_s.at[g], acc_s.at[g])

    @pl.when(s == n_steps - 1)
    def _():
        new = new_ref[...]
        new_block = past // SEL_BLOCK
        osel = []
        for g in range(NSA_KV_HEADS):
            qg = q_all[g * grp_rows:(g + 1) * grp_rows]
            k_new = new[:, (2 * NSA_KV_HEADS + g) * HEAD_DIM:(2 * NSA_KV_HEADS + g + 1) * HEAD_DIM]
            v_new = new[:, (3 * NSA_KV_HEADS + g) * HEAD_DIM:(3 * NSA_KV_HEADS + g + 1) * HEAD_DIM]
            logits = _dot_nt(qg, k_new) * ATT_SCALE
            q = lax.broadcasted_iota(jnp.int32, logits.shape, 0) % tnew
            k = lax.broadcasted_iota(jnp.int32, logits.shape, 1)
            chosen = jnp.concatenate([sel_ref[g][:, new_block:new_block + 1]] * NSA_REP, axis=0) > 0.5
            _online_softmax_step(logits, chosen & (k <= q) & (k < tnew), None, m_s.at[g], l_s.at[g],
                                 acc_s.at[g], v_rows=v_new)
            osel.append(acc_s[g] / l_s[g])
        wbuf = wbuf_ref[...].reshape(2 * NSA_GROUP_ROWS, wbuf_ref.shape[-1]).astype(BF16)
        wnew = wnew_ref[...]
        wb = wbuf.shape[1]
        tq = past + lax.broadcasted_iota(jnp.int32, (tnew, 1), 0)
        pos_buf = past - wb + lax.broadcasted_iota(jnp.int32, (tnew, wb), 1)
        kn = lax.broadcasted_iota(jnp.int32, (tnew, LANES), 1)
        ok = jnp.concatenate([(pos_buf <= tq) & (pos_buf > tq - WINDOW),
                              (kn < tnew) & (past + kn <= tq) & (past + kn > tq - WINDOW)], axis=1)
        gates = _sigmoid(small_ref[...])
        for h in range(NSA_HEADS):
            g, r = divmod(h, NSA_REP)
            qh_ = q_ref[h]
            k_buf = wbuf[g * HEAD_DIM:(g + 1) * HEAD_DIM]
            v_buf = wbuf[NSA_GROUP_ROWS + g * HEAD_DIM:NSA_GROUP_ROWS + (g + 1) * HEAD_DIM]
            k_new = wnew[:, g * HEAD_DIM:(g + 1) * HEAD_DIM]
            v_new = wnew[:, NSA_GROUP_ROWS + g * HEAD_DIM:NSA_GROUP_ROWS + (g + 1) * HEAD_DIM]
            logits = jnp.concatenate([_dot(qh_, k_buf), _dot_nt(qh_, k_new)], axis=1) * ATT_SCALE
            logits = jnp.where(ok, logits, NEG_INF)
            p = jnp.exp(logits - jnp.max(logits, axis=1, keepdims=True))
            pb = p.astype(BF16)
            o_w = (_dot_nt(pb[:, :wb], v_buf) + _dot(pb[:, wb:], v_new)) / jnp.sum(p, axis=1, keepdims=True)
            gl = GN_LANE + 3 * h
            o = (gates[:, gl:gl + 1] * ocmp_ref[h] + gates[:, gl + 1:gl + 2] * osel[g][r * tnew:(r + 1) * tnew]
                 + gates[:, gl + 2:gl + 3] * o_w)
            o_ref[:, h * HEAD_DIM:(h + 1) * HEAD_DIM] = o.astype(o_ref.dtype)


def nsa_sample_sel(page_table, qh, sel, ocmp, zp, nsa_new, win_view, win_new, nsa_view, layer):
    batch, n_pages = page_table.shape
    tnew = qh.shape[1] // batch
    past = n_pages * LANES
    assert past % SEL_BLOCK == 0 and tnew <= SEL_BLOCK
    n_steps = n_pages // PAGES_PER_STEP + 1
    grp_rows = NSA_REP * tnew
    full = lambda a: pl.BlockSpec((None,) + a.shape[1:], lambda b, s, pt: (b,) + (0,) * (a.ndim - 1))
    return pl.pallas_call(
        functools.partial(_nsa_sample_sel_kernel, n_steps=n_steps, past=past),
        grid_spec=pltpu.PrefetchScalarGridSpec(
            num_scalar_prefetch=1,
            grid=(batch, n_steps),
            in_specs=[pl.BlockSpec((NSA_HEADS, tnew, HEAD_DIM), lambda b, s, pt: (1, b, 0)),
                      full(sel), full(ocmp),
                      pl.BlockSpec((tnew, LANES), lambda b, s, pt: (b, ZP_SMALL // LANES)),
                      full(nsa_new),
                      pl.BlockSpec((None, None) + win_view.shape[2:], lambda b, s, pt: (layer, b, 0, 0, 0, 0)),
                      full(win_new)]
            + _page_specs((None, None, 2, NSA_KV_HEADS, HEAD_DIM, LANES), layer, n_pages, slot=1),
            out_specs=pl.BlockSpec((tnew, MIX_WIDTH), lambda b, s, pt: (b, 0)),
            scratch_shapes=[pltpu.VMEM((NSA_KV_HEADS, grp_rows, 1), F32), pltpu.VMEM((NSA_KV_HEADS, grp_rows, 1), F32),
                            pltpu.VMEM((NSA_KV_HEADS, grp_rows, HEAD_DIM), F32)]),
        out_shape=jax.ShapeDtypeStruct((batch * tnew, MIX_WIDTH), BF16),
        compiler_params=_cp("parallel", "arbitrary"),
        name="nsa_sample_sel",
    )(page_table, qh, sel, ocmp, zp, nsa_new, win_view, win_new, *([nsa_view] * PAGES_PER_STEP))


def _merge_kernel(b0, b1, b2, b3, w_ref, g0, g1, g2, g3, o_ref):
    acc = None
    for k, (b_ref, g_ref) in enumerate(zip((b0, b1, b2, b3), (g0, g1, g2, g3))):
        term = _sigmoid(g_ref[...]) * _dot(b_ref[...], w_ref[k])
        acc = term if acc is None else acc + term
    o_ref[...] = acc.astype(o_ref.dtype)


def merge_branches(branches, w, zp, gate_col):
    m = branches[0].shape[0]
    tm = min(512, m)
    tn = 512
    nj = D_MODEL // tn
    gate_specs = [pl.BlockSpec((tm, tn), lambda i, j, k=k: (i, gate_col // tn + k * nj + j))
                  for k in range(N_BRANCH)]
    return pl.pallas_call(
        _merge_kernel,
        grid=(m // tm, nj),
        in_specs=[pl.BlockSpec((tm, MIX_WIDTH), lambda i, j: (i, 0))] * N_BRANCH
        + [pl.BlockSpec((N_BRANCH, MIX_WIDTH, tn), lambda i, j: (0, 0, j))] + gate_specs,
        out_specs=pl.BlockSpec((tm, tn), lambda i, j: (i, j)),
        out_shape=jax.ShapeDtypeStruct((m, D_MODEL), BF16),
        compiler_params=_cp("parallel", "parallel"),
        name="merge_branches",
    )(*branches, w, *([zp] * N_BRANCH))


IN_SIZES = (MIX_WIDTH, 3 * MIX_WIDTH, IDX_HEADS * IDX_DIM, IDX_DIM, IDX_HEADS, MIX_WIDTH,
            6 * NSA_KV_HEADS * HEAD_DIM, 3 * NSA_HEADS, 3 * MIX_WIDTH, 2 * MLSTM_HEADS, MIX_WIDTH,
            N_BRANCH * D_MODEL)
(OFF_U, OFF_QKVB, OFF_QI, OFF_KI, OFF_WI, OFF_QN, OFF_KVN, OFF_GN, OFF_QKVM, OFF_GIF, OFF_OM,
 OFF_GBR) = np.concatenate([[0], np.cumsum(IN_SIZES)[:-1]]).tolist()

ZP_U, ZP_OM, ZP_Q, ZP_K, ZP_V, ZP_SMALL, ZP_GBR = 0, 512, 1024, 1536, 2048, 2560, 3072
ZP_WIDTH = ZP_GBR + N_BRANCH * D_MODEL
Q_WIDTH = 2 * MIX_WIDTH + IDX_HEADS * IDX_DIM
Q_HEADS = Q_WIDTH // HEAD_DIM


def _block_diag(blocks):
    g, r, c = blocks.shape
    eye = jnp.eye(g, dtype=blocks.dtype)
    return (blocks[:, :, None, :] * eye[:, None, :, None]).reshape(g * r, g * c)


def _s5_discretize(lam_re, lam_im, log_dt, b_re, b_im):
    dt = jnp.exp(log_dt)[:, None]
    mag = jnp.exp(lam_re * dt)
    a_re, a_im = mag * jnp.cos(lam_im * dt), mag * jnp.sin(lam_im * dt)
    den = lam_re * lam_re + lam_im * lam_im
    nr = a_re - 1.0
    coef_re = (nr * lam_re + a_im * lam_im) / den
    coef_im = (a_im * lam_re - nr * lam_im) / den
    bb_re = coef_re[..., None] * b_re - coef_im[..., None] * b_im
    bb_im = coef_re[..., None] * b_im + coef_im[..., None] * b_re
    return a_re, a_im, bb_re, bb_im


def _prep_layer(l, p):
    wt = jnp.transpose(p['w_in'], (2, 0, 1))[:, l, :]
    seg = lambda off, n: wt[off:off + n]
    wt_q = jnp.concatenate([seg(OFF_QKVB, MIX_WIDTH), seg(OFF_QN, MIX_WIDTH),
                            seg(OFF_QI, IDX_HEADS * IDX_DIM)]).astype(BF16)
    wt_kv = jnp.concatenate([seg(OFF_QKVB + MIX_WIDTH, 2 * MIX_WIDTH), seg(OFF_KI, IDX_DIM),
                             seg(OFF_KVN, 6 * NSA_KV_HEADS * HEAD_DIM)]).astype(BF16)
    small = jnp.concatenate([seg(OFF_WI, IDX_HEADS), seg(OFF_GN, 3 * NSA_HEADS), seg(OFF_GIF, 2 * MLSTM_HEADS)])
    pad = jnp.zeros((ZP_GBR - ZP_SMALL - small.shape[0], D_MODEL), F32)
    wt_plain = jnp.concatenate([seg(OFF_U, MIX_WIDTH), seg(OFF_OM, MIX_WIDTH), seg(OFF_QKVM, 3 * MIX_WIDTH),
                                small, pad, seg(OFF_GBR, N_BRANCH * D_MODEL)]).astype(BF16)
    a_re, a_im, bb_re, bb_im = _s5_discretize(p['s5_lam_re'][l], p['s5_lam_im'][l], p['s5_log_dt'][l],
                                              p['s5_b_re'][l], p['s5_b_im'][l])
    s5_b = jnp.concatenate([_block_diag(bb_re.transpose(0, 2, 1)), _block_diag(bb_im.transpose(0, 2, 1))],
                           axis=1).astype(BF16)
    s5_c = jnp.concatenate([_block_diag(p['s5_c_re'][l].transpose(0, 2, 1)),
                            -_block_diag(p['s5_c_im'][l].transpose(0, 2, 1))], axis=0).astype(BF16)
    gate_bias = jnp.zeros((1, LANES), F32)
    gate_bias = gate_bias.at[0, IG_LANE:IG_LANE + MLSTM_HEADS].set(p['mlstm_b_i'][l])
    gate_bias = gate_bias.at[0, FG_LANE:FG_LANE + MLSTM_HEADS].set(p['mlstm_b_f'][l])
    return dict(
        norm_g=p['norm_g'][l][:, None, :],
        w_ffn1_in=p['w_ffn1_in'][l].astype(BF16), w_ffn1_out=p['w_ffn1_out'][l].astype(BF16),
        w_ffn2_in=p['w_ffn2_in'][l].astype(BF16), w_ffn2_out=p['w_ffn2_out'][l].astype(BF16),
        wt_q=wt_q, wt_kv=wt_kv, wt_plain=wt_plain,
        s5_a=jnp.stack([a_re, a_im]).reshape(2, S5_ROWS, LANES), s5_b=s5_b, s5_c=s5_c,
        s5_d=p['s5_d'][l][None, :], w_s5_glu=p['w_s5_glu'][l].astype(BF16),
        gate_bias=gate_bias, mlstm_norm_g=p['mlstm_norm_g'][l][None, :],
        w_branch=p['w_branch'][l].astype(BF16), w_out=p['w_out'][l].astype(BF16))


def _rope_tables(pos):
    inv = ROPE_THETA ** (-jnp.arange(HALF, dtype=F32) / HALF)
    ang = pos.astype(F32)[:, None] * inv[None, :]
    return jnp.cos(ang), jnp.sin(ang)


def _row_tables(cos, sin, rotated):
    one, zero = jnp.ones_like(cos), jnp.zeros_like(sin)
    c = jnp.concatenate([x for r in rotated for x in ((cos, cos) if r else (one, one))], axis=1)
    s = jnp.concatenate([x for r in rotated for x in ((-sin, sin) if r else (zero, zero))], axis=1)
    return c, s


def _ffn(x, ada, sub, g, w_in, w_out, t):
    h = modnorm(x, g, ada, sub, t, BF16)
    return mm_resid(swiglu_in(h, w_in), w_out, x, ada, 3 * sub + 2, 0.5, t)


def _s5_mixer(zp, lw, h0, batch, t):
    xs = mm(zp, lw['s5_b'])
    s, fin = s5_scan(xs.reshape(batch, t, 2, S5_ROWS, LANES), lw['s5_a'], h0)
    o = s5_out(s.reshape(batch * t, 2 * S5_N), lw['s5_c'], zp, lw['s5_d'], lw['w_s5_glu'])
    return o, fin[:, 0].reshape(batch, S5_GROUPS, S5_STATE), fin[:, 1].reshape(batch, S5_GROUPS, S5_STATE)


def _finish_layer(x, ada, lw, zp, branches, t):
    merged = merge_branches(branches, lw['w_branch'], zp, ZP_GBR)
    x = mm_resid(merged, lw['w_out'], x, ada, 5, 1.0, t)
    return _ffn(x, ada, 2, lw['norm_g'][2], lw['w_ffn2_in'], lw['w_ffn2_out'], t)


def _layer_prompt(x, ada, lw, batch, t):
    m = batch * t
    x = _ffn(x, ada, 0, lw['norm_g'][0], lw['w_ffn1_in'], lw['w_ffn1_out'], t)
    h = modnorm(x, lw['norm_g'][1], ada, 1, t, BF16)
    cos, sin = _rope_tables(jnp.arange(t, dtype=jnp.int32))
    cq, sq = _row_tables(cos, sin, (True, True))
    qr = proj_rope_rows(h, lw['wt_q'], cq, sq, t, BF16)
    qh = qr.reshape(m, Q_HEADS, HEAD_DIM).transpose(1, 0, 2)
    kv_t, ki_t, nsa_t, win_t, kv_tb, ki_tb, nsa_tb, win_tb = proj_cols(h, lw['wt_kv'], cos.T, sin.T, batch, t)
    zp = mm_nt(h, lw['wt_plain'])
    o_s5, s5_re, s5_im = _s5_mixer(zp, lw, jnp.zeros((batch, 2, S5_ROWS, LANES), F32), batch, t)
    o_dsa = dsa_prompt(qh, zp, ZP_SMALL, ki_tb, kv_tb, batch, t)
    o_nsa = nsa_prompt(qh, zp, ZP_SMALL, nsa_pool(nsa_t, batch, t), nsa_tb, win_tb, batch, t)
    chunk = 64 if t % 64 == 0 else t
    o_ml, mc, mn, mm_ = mlstm(zp, lw['gate_bias'], lw['mlstm_norm_g'],
                              jnp.zeros((batch, MLSTM_HEADS, MLSTM_DK, MLSTM_DK), F32),
                              jnp.zeros((batch, 8, LANES), F32), jnp.zeros((batch, 8, LANES), F32),
                              batch, t, chunk, (ZP_Q, ZP_K, ZP_V, ZP_SMALL, ZP_OM))
    x = _finish_layer(x, ada, lw, zp, [o_s5, o_dsa, o_nsa, o_ml], t)
    tokens_last = lambda a, shape: jnp.moveaxis(a.reshape((batch,) + shape + (a.shape[-1],)), -1, 1)
    wb = min(WINDOW, t)
    state = (tokens_last(kv_t, (2, DSA_HEADS, HEAD_DIM)), jnp.swapaxes(ki_t, 1, 2),
             tokens_last(nsa_t, (4, NSA_KV_HEADS, HEAD_DIM)),
             tokens_last(win_t[:, :, t - wb:], (2, NSA_KV_HEADS, HEAD_DIM)),
             mc, mn[:, :MLSTM_HEADS], mm_[:, :MLSTM_HEADS, 0], s5_re, s5_im)
    return x, state


def _layer_sample(x, ada, lw, layer, batch, t, page_table, views, past):
    m = batch * t
    kidx_view, kv_view, nsa_view, win_view = views
    past_len = page_table.shape[1] * LANES
    x = _ffn(x, ada, 0, lw['norm_g'][0], lw['w_ffn1_in'], lw['w_ffn1_out'], t)
    h = modnorm(x, lw['norm_g'][1], ada, 1, t, BF16)
    cos, sin = _rope_tables(past_len + jnp.arange(t, dtype=jnp.int32))
    cq, sq = _row_tables(cos, sin, (True, True))
    qr = proj_rope_rows(h, lw['wt_q'], cq, sq, t, BF16)
    qh = qr.reshape(m, Q_HEADS, HEAD_DIM).transpose(1, 0, 2)
    ckv, skv = _row_tables(cos, sin, KV_COLS_ROPE + (False,))
    wt_kv = jnp.concatenate([lw['wt_kv'], jnp.zeros((HEAD_DIM, D_MODEL), BF16)])
    kvr = proj_rope_rows(h, wt_kv, ckv, skv, t, F32)
    kv_rows, ki_rows = kvr[:, :2 * MIX_WIDTH], kvr[:, 2 * MIX_WIDTH:2 * MIX_WIDTH + IDX_DIM]
    nsa_rows = kvr[:, 2 * MIX_WIDTH + IDX_DIM:3 * MIX_WIDTH + IDX_DIM]
    win_rows = kvr[:, 3 * MIX_WIDTH + IDX_DIM:KV_COLS]
    zp = mm_nt(h, lw['wt_plain'])
    h0 = jnp.stack([past['s5_re'][layer], past['s5_im'][layer]], axis=1).reshape(batch, 2, S5_ROWS, LANES)
    o_s5, s5_re, s5_im = _s5_mixer(zp, lw, h0, batch, t)

    def new_rows(a):
        a = a.reshape(batch, t, a.shape[-1])
        return jnp.pad(a, ((0, 0), (0, LANES - t), (0, 0))).astype(BF16)

    scores = dsa_sample_scores(page_table, qh, zp, new_rows(ki_rows), kidx_view, layer)
    q_dsa = qr[:, :MIX_WIDTH].reshape(batch, t, DSA_HEADS, HEAD_DIM)
    qbd = jnp.einsum('bqhd,hg->bhqgd', q_dsa, jnp.eye(DSA_HEADS, dtype=BF16)).reshape(batch, DSA_HEADS * t, MIX_WIDTH)
    o_dsa = dsa_sample_attn(page_table, scores, qbd, new_rows(kv_rows), kv_view, layer)
    ocmp, sel = nsa_sample_cmp(page_table, qh, nsa_view, layer)
    o_nsa = nsa_sample_sel(page_table, qh, sel, ocmp, zp, new_rows(nsa_rows), win_view, new_rows(win_rows),
                           nsa_view, layer)
    n0 = jnp.pad(past['mlstm_n'][layer], ((0, 0), (0, 8 - MLSTM_HEADS), (0, 0)))
    m0 = jnp.pad(jnp.broadcast_to(past['mlstm_m'][layer][:, :, None], (batch, MLSTM_HEADS, LANES)),
                 ((0, 0), (0, 8 - MLSTM_HEADS), (0, 0)))
    chunk = 64 if t % 64 == 0 else t
    o_ml, mc, mn, mm_ = mlstm(zp, lw['gate_bias'], lw['mlstm_norm_g'], past['mlstm_c'][layer], n0, m0,
                              batch, t, chunk, (ZP_Q, ZP_K, ZP_V, ZP_SMALL, ZP_OM))
    x = _finish_layer(x, ada, lw, zp, [o_s5, o_dsa, o_nsa, o_ml], t)
    win_buf = past['nsa_win'][layer]
    wb = win_buf.shape[1]
    win_all = jnp.concatenate([win_buf, win_rows.reshape(batch, t, 2, NSA_KV_HEADS, HEAD_DIM)], axis=1)
    state = (kv_rows.reshape(batch, t, 2, DSA_HEADS, HEAD_DIM), ki_rows.reshape(batch, t, IDX_DIM),
             nsa_rows.reshape(batch, t, 4, NSA_KV_HEADS, HEAD_DIM), win_all[:, win_all.shape[1] - wb:],
             mc, mn[:, :MLSTM_HEADS], mm_[:, :MLSTM_HEADS, 0], s5_re, s5_im)
    return x, state


def kernel(x_prompt, x_sample, cache_dsa_kv, cache_dsa_kidx, cache_nsa_kv, cache_nsa_win, state_mlstm_c,
           state_mlstm_n, state_mlstm_m, state_s5_re, state_s5_im, page_table, c_prompt, c_sample, w_ada, b_ada,
           norm_g, w_ffn1_in, w_ffn1_out, w_ffn2_in, w_ffn2_out, w_in, s5_lam_re, s5_lam_im, s5_log_dt, s5_b_re,
           s5_b_im, s5_c_re, s5_c_im, s5_d, w_s5_glu, mlstm_b_i, mlstm_b_f, mlstm_norm_g, w_branch, w_out,
           final_norm_g):
    params = dict(norm_g=norm_g, w_ffn1_in=w_ffn1_in, w_ffn1_out=w_ffn1_out, w_ffn2_in=w_ffn2_in,
                  w_ffn2_out=w_ffn2_out, w_in=w_in, s5_lam_re=s5_lam_re, s5_lam_im=s5_lam_im, s5_log_dt=s5_log_dt,
                  s5_b_re=s5_b_re, s5_b_im=s5_b_im, s5_c_re=s5_c_re, s5_c_im=s5_c_im, s5_d=s5_d,
                  w_s5_glu=w_s5_glu, mlstm_b_i=mlstm_b_i, mlstm_b_f=mlstm_b_f, mlstm_norm_g=mlstm_norm_g,
                  w_branch=w_branch, w_out=w_out)
    bp, tp, d = x_prompt.shape
    bs, ts, _ = x_sample.shape
    depth = w_ada.shape[0]
    views = (jnp.transpose(cache_dsa_kidx, (0, 1, 3, 2)), jnp.transpose(cache_dsa_kv, (0, 1, 3, 4, 5, 2)),
             jnp.transpose(cache_nsa_kv, (0, 1, 3, 4, 5, 2)), jnp.transpose(cache_nsa_win, (0, 1, 3, 4, 5, 2)))
    past = dict(nsa_win=cache_nsa_win, mlstm_c=state_mlstm_c, mlstm_n=state_mlstm_n, mlstm_m=state_mlstm_m,
                s5_re=state_s5_re, s5_im=state_s5_im)
    ada_rows = -(-(bp + bs) // 8) * 8
    c_all = jnp.pad(jnp.concatenate([c_prompt, c_sample]), ((0, ada_rows - bp - bs), (0, 0)))
    xp = x_prompt.reshape(bp * tp, d)
    xs = x_sample.reshape(bs * ts, d)
    st_p, st_s = [], []
    for l in range(depth):
        lw = _prep_layer(l, params)
        ada = ada_project(c_all, w_ada[l].astype(BF16), b_ada[l][None]).reshape(ada_rows, 9, d)
        xp, sp = _layer_prompt(xp, ada[:bp], lw, bp, tp)
        xs, ss = _layer_sample(xs, ada[bp:bp + bs], lw, l, bs, ts, page_table, views, past)
        st_p.append(sp)
        st_s.append(ss)
    g = final_norm_g[None, :]
    y_p = modnorm(xp, g, ada[:bp], None, tp, F32).reshape(bp, tp, d)
    y_s = modnorm(xs, g, ada[bp:bp + bs], None, ts, F32).reshape(bs, ts, d)
    outs = [y_p, y_s]
    for i in range(9):
        outs.append(jnp.stack([s[i] for s in st_p]))
        outs.append(jnp.stack([s[i] for s in st_s]))
    return tuple(outs)
```

```python
import functools
import math

import jax
import jax.numpy as jnp
import numpy as np
from jax import lax
from jax.experimental import pallas as pl
from jax.experimental.pallas import tpu as pltpu

F32 = jnp.float32
BF16 = jnp.bfloat16

D_MODEL = 2048
MIX_WIDTH = D_MODEL // 4
HEAD_DIM = 64
HALF = HEAD_DIM // 2
S5_GROUP = 16
S5_GROUPS = MIX_WIDTH // S5_GROUP
S5_STATE = 64
S5_N = S5_GROUPS * S5_STATE
DSA_HEADS = MIX_WIDTH // HEAD_DIM
IDX_HEADS = 4
IDX_DIM = 64
DSA_TOPK = 256
NSA_HEADS = MIX_WIDTH // HEAD_DIM
NSA_KV_HEADS = 2
NSA_REP = NSA_HEADS // NSA_KV_HEADS
CMP_STRIDE = 16
CMP_LEN = 2 * CMP_STRIDE
SEL_BLOCK = 64
SEL_TOPN = 16
CMP_PER_SEL = SEL_BLOCK // CMP_STRIDE
WINDOW = 512
MLSTM_HEADS = 4
MLSTM_DK = MIX_WIDTH // MLSTM_HEADS
D_FF = 2 * D_MODEL
ROPE_THETA = 10000.0
QUERY_BLOCK = 128
RMS_EPS = 1e-6
NEG_INF = -1e30
FORCE_SCORE = 1e4
N_BRANCH = 4

LANES = 128
VMEM_LIMIT = 56 * 1024 * 1024
HIGHEST = lax.Precision.HIGHEST


def _cp(*sem):
    return pltpu.CompilerParams(dimension_semantics=sem, vmem_limit_bytes=VMEM_LIMIT)


def _dot(a, b):
    return jnp.dot(a, b, preferred_element_type=F32)


def _dot_nt(a, b, precision=None):
    return lax.dot_general(a, b, (((1,), (1,)), ((), ())), preferred_element_type=F32,
                           precision=precision)


def _sigmoid(x):
    return 1.0 / (1.0 + jnp.exp(-x))


def _silu(x):
    return x * _sigmoid(x)


def _ada_kernel(c_ref, w_ref, b_ref, o_ref):
    c = c_ref[...]
    o_ref[...] = _dot(_silu(c).astype(BF16), w_ref[...]) + b_ref[...]


def ada_project(c, w, b):
    r, d = c.shape
    n = w.shape[1]
    tn = 2048
    return pl.pallas_call(
        _ada_kernel,
        grid=(n // tn,),
        in_specs=[pl.BlockSpec((r, d), lambda j: (0, 0)),
                  pl.BlockSpec((d, tn), lambda j: (0, j)),
                  pl.BlockSpec((1, tn), lambda j: (0, j))],
        out_specs=pl.BlockSpec((r, tn), lambda j: (0, j)),
        out_shape=jax.ShapeDtypeStruct((r, n), F32),
        compiler_params=_cp("parallel"),
        name="ada_project",
    )(c, w, b)


def _modnorm_kernel(x_ref, g_ref, ada_ref, o_ref, *, sub):
    x = x_ref[...]
    y = x * lax.rsqrt(jnp.mean(x * x, axis=-1, keepdims=True) + RMS_EPS)
    y = y * g_ref[...]
    if sub is not None:
        shift = ada_ref[3 * sub:3 * sub + 1, :]
        scale = ada_ref[3 * sub + 1:3 * sub + 2, :]
        y = y * (1.0 + scale) + shift
    o_ref[...] = y.astype(o_ref.dtype)


def modnorm(x, g, ada, sub, rows_per_batch, out_dtype):
    m, d = x.shape
    tm = min(512, rows_per_batch)
    nb = rows_per_batch // tm
    return pl.pallas_call(
        functools.partial(_modnorm_kernel, sub=sub),
        grid=(m // tm,),
        in_specs=[pl.BlockSpec((tm, d), lambda i: (i, 0)),
                  pl.BlockSpec((1, d), lambda i: (0, 0)),
                  pl.BlockSpec((None, 9, d), lambda i: (i // nb, 0, 0))],
        out_specs=pl.BlockSpec((tm, d), lambda i: (i, 0)),
        out_shape=jax.ShapeDtypeStruct((m, d), out_dtype),
        compiler_params=_cp("parallel"),
        name="modnorm",
    )(x, g, ada)


def _swiglu_in_kernel(h_ref, wa_ref, wg_ref, o_ref):
    h = h_ref[...]
    a = _dot(h, wa_ref[...])
    g = _dot(h, wg_ref[...])
    o_ref[...] = (_silu(a) * g).astype(o_ref.dtype)


def swiglu_in(h, w):
    m, d = h.shape
    f = w.shape[1] // 2
    tm = min(1024, m)
    tn = 512
    nj = f // tn
    return pl.pallas_call(
        _swiglu_in_kernel,
        grid=(m // tm, nj),
        in_specs=[pl.BlockSpec((tm, d), lambda i, j: (i, 0)),
                  pl.BlockSpec((d, tn), lambda i, j: (0, j)),
                  pl.BlockSpec((d, tn), lambda i, j: (0, j + nj))],
        out_specs=pl.BlockSpec((tm, tn), lambda i, j: (i, j)),
        out_shape=jax.ShapeDtypeStruct((m, f), BF16),
        compiler_params=_cp("parallel", "parallel"),
        name="swiglu_in",
    )(h, w, w)


def _mm_resid_kernel(a_ref, w_ref, x_ref, ada_ref, o_ref, *, gate_row, coef):
    y = _dot(a_ref[...], w_ref[...])
    gate = ada_ref[gate_row:gate_row + 1, :]
    o_ref[...] = x_ref[...] + (coef * gate) * y


def mm_resid(a, w, x, ada, gate_row, coef, rows_per_batch):
    m, k = a.shape
    n = w.shape[1]
    tm = min(1024, rows_per_batch)
    tn = 512
    nb = rows_per_batch // tm
    return pl.pallas_call(
        functools.partial(_mm_resid_kernel, gate_row=gate_row, coef=coef),
        grid=(m // tm, n // tn),
        in_specs=[pl.BlockSpec((tm, k), lambda i, j: (i, 0)),
                  pl.BlockSpec((k, tn), lambda i, j: (0, j)),
                  pl.BlockSpec((tm, tn), lambda i, j: (i, j)),
                  pl.BlockSpec((None, 9, tn), lambda i, j: (i // nb, 0, j))],
        out_specs=pl.BlockSpec((tm, tn), lambda i, j: (i, j)),
        out_shape=jax.ShapeDtypeStruct((m, n), F32),
        compiler_params=_cp("parallel", "parallel"),
        name="mm_resid",
    )(a, w, x, ada)


def _mm_kernel(a_ref, w_ref, o_ref):
    o_ref[...] = _dot(a_ref[...].astype(BF16), w_ref[...]).astype(o_ref.dtype)


def mm(a, w, out_dtype=F32, tn=2048):
    m = a.shape[0]
    k, n = w.shape
    tm = min(1024, m)
    return pl.pallas_call(
        _mm_kernel,
        grid=(m // tm, n // tn),
        in_specs=[pl.BlockSpec((tm, k), lambda i, j: (i, 0)),
                  pl.BlockSpec((k, tn), lambda i, j: (0, j))],
        out_specs=pl.BlockSpec((tm, tn), lambda i, j: (i, j)),
        out_shape=jax.ShapeDtypeStruct((m, n), out_dtype),
        compiler_params=_cp("parallel", "parallel"),
        name="mm",
    )(a, w)


def _mm_nt_kernel(a_ref, wt_ref, o_ref):
    o_ref[...] = _dot_nt(a_ref[...], wt_ref[...]).astype(o_ref.dtype)


def mm_nt(a, wt, tn=1024):
    m, k = a.shape
    n = wt.shape[0]
    tm = min(1024, m)
    return pl.pallas_call(
        _mm_nt_kernel,
        grid=(m // tm, n // tn),
        in_specs=[pl.BlockSpec((tm, k), lambda i, j: (i, 0)),
                  pl.BlockSpec((tn, k), lambda i, j: (j, 0))],
        out_specs=pl.BlockSpec((tm, tn), lambda i, j: (i, j)),
        out_shape=jax.ShapeDtypeStruct((m, n), F32),
        compiler_params=_cp("parallel", "parallel"),
        name="mm_nt",
    )(a, wt)


def _rope_rows(x, cos, sin_signed):
    w = x.shape[1]
    lane = lax.broadcasted_iota(jnp.int32, x.shape, 1)
    fwd = pltpu.roll(x, w - HALF, axis=1)
    bwd = pltpu.roll(x, HALF, axis=1)
    swapped = jnp.where((lane % HEAD_DIM) < HALF, fwd, bwd)
    return x * cos + swapped * sin_signed


def _proj_rope_rows_kernel(h_ref, wt_ref, cos_ref, sin_ref, o_ref, *, periodic):
    z = _dot_nt(h_ref[...], wt_ref[...])
    if periodic:
        reps = z.shape[1] // LANES
        cos = jnp.concatenate([cos_ref[...]] * reps, axis=1)
        sin = jnp.concatenate([sin_ref[...]] * reps, axis=1)
    else:
        cos, sin = cos_ref[...], sin_ref[...]
    o_ref[...] = _rope_rows(z, cos, sin).astype(o_ref.dtype)


def proj_rope_rows(h, wt, cos, sin, rows_per_batch, out_dtype):
    m, k = h.shape
    n = wt.shape[0]
    tm = min(512, rows_per_batch)
    nb = rows_per_batch // tm
    tw = cos.shape[1]
    return pl.pallas_call(
        functools.partial(_proj_rope_rows_kernel, periodic=(tw != n)),
        grid=(m // tm,),
        in_specs=[pl.BlockSpec((tm, k), lambda i: (i, 0)),
                  pl.BlockSpec((n, k), lambda i: (0, 0)),
                  pl.BlockSpec((tm, tw), lambda i: (i % nb, 0)),
                  pl.BlockSpec((tm, tw), lambda i: (i % nb, 0))],
        out_specs=pl.BlockSpec((tm, n), lambda i: (i, 0)),
        out_shape=jax.ShapeDtypeStruct((m, n), out_dtype),
        compiler_params=_cp("parallel"),
        name="proj_rope_rows",
    )(h, wt, cos, sin)


KV_COLS_ROPE = (True,) * 8 + (False,) * 8 + (True,) + (True, True, False, False) * 3
KV_COLS = HEAD_DIM * len(KV_COLS_ROPE)


def _proj_cols_kernel(h_ref, wt_ref, cos_ref, sin_ref, kv_ref, ki_ref, nsa_ref, win_ref,
                      kvb_ref, kib_ref, nsab_ref, winb_ref):
    zt = _dot_nt(wt_ref[...], h_ref[...])
    cos, sin = cos_ref[...], sin_ref[...]
    parts = []
    for r, rot in enumerate(KV_COLS_ROPE):
        x1 = zt[r * HEAD_DIM:r * HEAD_DIM + HALF]
        x2 = zt[r * HEAD_DIM + HALF:(r + 1) * HEAD_DIM]
        if rot:
            parts += [x1 * cos - x2 * sin, x1 * sin + x2 * cos]
        else:
            parts += [x1, x2]
    out = jnp.concatenate(parts, axis=0)
    bounds = (0, 1024, 1088, 1600, 1856)
    for lo, hi, f_ref, b_ref in zip(bounds[:-1], bounds[1:], (kv_ref, ki_ref, nsa_ref, win_ref),
                                    (kvb_ref, kib_ref, nsab_ref, winb_ref)):
        f_ref[...] = out[lo:hi]
        b_ref[...] = out[lo:hi].astype(BF16)


def proj_cols(h, wt, cos_t, sin_t, batch, t):
    k = h.shape[1]
    tm = min(512, t)
    nt = t // tm
    widths = (1024, 64, 512, 256)
    out_shape = ([jax.ShapeDtypeStruct((batch, w, t), F32) for w in widths]
                 + [jax.ShapeDtypeStruct((batch, w, t), BF16) for w in widths])
    out_specs = [pl.BlockSpec((None, w, tm), lambda b, i: (b, 0, i)) for w in widths] * 2
    return pl.pallas_call(
        _proj_cols_kernel,
        grid=(batch, nt),
        in_specs=[pl.BlockSpec((tm, k), lambda b, i: (b * nt + i, 0)),
                  pl.BlockSpec((KV_COLS, k), lambda b, i: (0, 0)),
                  pl.BlockSpec((HALF, tm), lambda b, i: (0, i)),
                  pl.BlockSpec((HALF, tm), lambda b, i: (0, i))],
        out_specs=out_specs,
        out_shape=out_shape,
        compiler_params=_cp("parallel", "parallel"),
        name="proj_cols",
    )(h, wt, cos_t, sin_t)


S5_ROWS = S5_N // LANES


def _s5_scan_kernel(x_ref, a_ref, h0_ref, s_ref, fin_ref, carry_ref, *, tc):
    j = pl.program_id(1)

    @pl.when(j == 0)
    def _():
        carry_ref[...] = h0_ref[...]

    ar, ai = a_ref[0], a_ref[1]

    def step(t, carry):
        hr, hi = carry
        nr = ar * hr - ai * hi + x_ref[t, 0]
        ni = ar * hi + ai * hr + x_ref[t, 1]
        s_ref[t, 0] = nr
        s_ref[t, 1] = ni
        return nr, ni

    hr, hi = lax.fori_loop(0, tc, step, (carry_ref[0], carry_ref[1]), unroll=8)
    carry_ref[0] = hr
    carry_ref[1] = hi

    @pl.when(j == pl.num_programs(1) - 1)
    def _():
        fin_ref[0] = hr
        fin_ref[1] = hi


def s5_scan(x, a, h0):
    b, t = x.shape[:2]
    tc = min(256, t)
    blk = (None, tc, 2, S5_ROWS, LANES)
    st = (None, 2, S5_ROWS, LANES)
    return pl.pallas_call(
        functools.partial(_s5_scan_kernel, tc=tc),
        grid=(b, t // tc),
        in_specs=[pl.BlockSpec(blk, lambda i, j: (i, j, 0, 0, 0)),
                  pl.BlockSpec((2, S5_ROWS, LANES), lambda i, j: (0, 0, 0)),
                  pl.BlockSpec(st, lambda i, j: (i, 0, 0, 0))],
        out_specs=[pl.BlockSpec(blk, lambda i, j: (i, j, 0, 0, 0)),
                   pl.BlockSpec(st, lambda i, j: (i, 0, 0, 0))],
        out_shape=[jax.ShapeDtypeStruct(x.shape, F32), jax.ShapeDtypeStruct(h0.shape, F32)],
        scratch_shapes=[pltpu.VMEM((2, S5_ROWS, LANES), F32)],
        compiler_params=_cp("parallel", "arbitrary"),
        name="s5_scan",
    )(x, a, h0)


def _gelu_tanh(x):
    return 0.5 * x * (1.0 + jnp.tanh(math.sqrt(2.0 / math.pi) * (x + 0.044715 * (x * x * x))))


def _s5_out_kernel(s_ref, c_ref, u_ref, d_ref, w_ref, o_ref):
    y = _dot(s_ref[...].astype(BF16), c_ref[...]) + d_ref[...] * u_ref[...]
    y = _gelu_tanh(y)
    o_ref[...] = (y * _sigmoid(_dot(y.astype(BF16), w_ref[...]))).astype(o_ref.dtype)


def s5_out(s, cmat, zp, d, wglu):
    m = s.shape[0]
    tm = min(512, m)
    return pl.pallas_call(
        _s5_out_kernel,
        grid=(m // tm,),
        in_specs=[pl.BlockSpec((tm, 2 * S5_N), lambda i: (i, 0)),
                  pl.BlockSpec((2 * S5_N, MIX_WIDTH), lambda i: (0, 0)),
                  pl.BlockSpec((tm, MIX_WIDTH), lambda i: (i, 0)),
                  pl.BlockSpec((1, MIX_WIDTH), lambda i: (0, 0)),
                  pl.BlockSpec((MIX_WIDTH, MIX_WIDTH), lambda i: (0, 0))],
        out_specs=pl.BlockSpec((tm, MIX_WIDTH), lambda i: (i, 0)),
        out_shape=jax.ShapeDtypeStruct((m, MIX_WIDTH), BF16),
        compiler_params=_cp("parallel"),
        name="s5_out",
    )(s, cmat, zp, d, wglu)


WI_LANE = 0
GN_LANE = WI_LANE + IDX_HEADS
IG_LANE = GN_LANE + 3 * NSA_HEADS
FG_LANE = IG_LANE + MLSTM_HEADS


def _mlstm_kernel(q_ref, k_ref, v_ref, gates_ref, om_ref, bias_ref, g_ref, c0_ref, n0_ref, m0_ref,
                  o_ref, c_out, n_out, m_out, c_s, n_s, m_s, *, chunk):
    j = pl.program_id(1)

    @pl.when(j == 0)
    def _():
        c_s[...] = c0_ref[...]
        n_s[...] = n0_ref[...]
        m_s[...] = m0_ref[...]

    gates = gates_ref[...] + bias_ref[...]
    lane = lax.broadcasted_iota(jnp.int32, gates.shape, 1)
    is_f = (lane >= FG_LANE) & (lane < FG_LANE + MLSTM_HEADS)
    logsig = jnp.minimum(gates, 0.0) - jnp.log(1.0 + jnp.exp(-jnp.abs(gates)))
    gl = jnp.where(is_f, logsig, gates)
    row = lax.broadcasted_iota(jnp.int32, (chunk, chunk), 0)
    col = lax.broadcasted_iota(jnp.int32, (chunk, chunk), 1)
    causal = col <= row
    cum = jnp.dot(causal.astype(F32), gl, preferred_element_type=F32, precision=HIGHEST)
    gl_t = gl.T
    cum_t = cum.T
    scale = 1.0 / math.sqrt(MLSTM_DK)
    for h in range(MLSTM_HEADS):
        sl = slice(h * MLSTM_DK, (h + 1) * MLSTM_DK)
        q = q_ref[:, sl]
        k = k_ref[:, sl] * scale
        v = v_ref[:, sl]
        qb, kb, vb = q.astype(BF16), k.astype(BF16), v.astype(BF16)
        c = c_s[h]
        n = n_s[h:h + 1, :]
        m = m_s[h:h + 1, 0:1]
        cum_c = cum[:, FG_LANE + h:FG_LANE + h + 1]
        cum_r = cum_t[FG_LANE + h:FG_LANE + h + 1, :]
        ig_c = gl[:, IG_LANE + h:IG_LANE + h + 1]
        ig_r = gl_t[IG_LANE + h:IG_LANE + h + 1, :]
        logd = jnp.where(causal, cum_c - cum_r + ig_r, NEG_INF)
        log_state = cum_c + m
        m_t = jnp.maximum(log_state, jnp.max(logd, axis=1, keepdims=True))
        w_in = jnp.exp(logd - m_t)
        w_st = jnp.exp(log_state - m_t)
        s = _dot_nt(qb, kb) * w_in
        num = w_st * _dot(qb, c.astype(BF16)) + _dot(s.astype(BF16), vb)
        den = w_st * jnp.sum(q * n, axis=1, keepdims=True) + jnp.sum(s, axis=1, keepdims=True)
        hh = num / jnp.maximum(jnp.abs(den), jnp.exp(-m_t))
        total = cum_r[:, chunk - 1:chunk]
        m_new = jnp.maximum(total + m, jnp.max(total - cum_r + ig_r, axis=1, keepdims=True))
        a = jnp.exp(total + m - m_new)
        ws_c = jnp.exp(total - cum_c + ig_c - m_new)
        kw = k * ws_c
        c_s[h] = a * c + lax.dot_general(kw.astype(BF16), vb, (((0,), (0,)), ((), ())),
                                         preferred_element_type=F32)
        n_s[h:h + 1, :] = a * n + jnp.sum(kw, axis=0, keepdims=True)
        m_s[h:h + 1, :] = jnp.broadcast_to(m_new, (1, LANES))
        hn = hh * lax.rsqrt(jnp.mean(hh * hh, axis=1, keepdims=True) + RMS_EPS)
        o_ref[:, sl] = (hn * g_ref[:, sl] * _sigmoid(om_ref[:, sl])).astype(o_ref.dtype)

    @pl.when(j == pl.num_programs(1) - 1)
    def _():
        c_out[...] = c_s[...]
        n_out[...] = n_s[...]
        m_out[...] = m_s[...]


def mlstm(zp, bias, norm_g, c0, n0, m0, batch, t, chunk, cols):
    nc = t // chunk
    q_col, k_col, v_col, small_col, om_col = cols
    wide = lambda col: pl.BlockSpec((chunk, MIX_WIDTH), lambda b, j: (b * nc + j, col // MIX_WIDTH))
    st = lambda shape: pl.BlockSpec((None,) + shape, lambda b, j: (b,) + (0,) * len(shape))
    return pl.pallas_call(
        functools.partial(_mlstm_kernel, chunk=chunk),
        grid=(batch, nc),
        in_specs=[wide(q_col), wide(k_col), wide(v_col),
                  pl.BlockSpec((chunk, LANES), lambda b, j: (b * nc + j, small_col // LANES)),
                  wide(om_col),
                  pl.BlockSpec((1, LANES), lambda b, j: (0, 0)),
                  pl.BlockSpec((1, MIX_WIDTH), lambda b, j: (0, 0)),
                  st((MLSTM_HEADS, MLSTM_DK, MLSTM_DK)), st((8, LANES)), st((8, LANES))],
        out_specs=[pl.BlockSpec((chunk, MIX_WIDTH), lambda b, j: (b * nc + j, 0)),
                   st((MLSTM_HEADS, MLSTM_DK, MLSTM_DK)), st((8, LANES)), st((8, LANES))],
        out_shape=[jax.ShapeDtypeStruct((batch * t, MIX_WIDTH), BF16),
                   jax.ShapeDtypeStruct((batch, MLSTM_HEADS, MLSTM_DK, MLSTM_DK), F32),
                   jax.ShapeDtypeStruct((batch, 8, LANES), F32),
                   jax.ShapeDtypeStruct((batch, 8, LANES), F32)],
        scratch_shapes=[pltpu.VMEM((MLSTM_HEADS, MLSTM_DK, MLSTM_DK), F32),
                        pltpu.VMEM((8, LANES), F32), pltpu.VMEM((8, LANES), F32)],
        compiler_params=_cp("parallel", "arbitrary"),
        name="mlstm",
    )(zp, zp, zp, zp, zp, bias, norm_g, c0, n0, m0)


INT_MIN = -2 ** 31


def _count(mask):
    return jnp.sum(mask.astype(F32), axis=1, keepdims=True)


def _sort_key(score):
    bits = pltpu.bitcast(score, jnp.int32)
    key = bits ^ ((bits >> 31) & 0x7FFFFFFF)
    return jnp.where(score == 0.0, 0, key)


def _count_paged(mask):
    x = mask.astype(F32)
    pages = x.shape[0]
    if pages % 8 == 0:
        x = jnp.sum(x.reshape((pages // 8, 8) + x.shape[1:]), axis=0)
    return jnp.sum(jnp.sum(x, axis=0, keepdims=True), axis=2, keepdims=True)


def _topk_mask(score, k, paged=False):
    count = _count_paged if paged else _count
    if paged:
        idx = (lax.broadcasted_iota(jnp.int32, score.shape, 0) * LANES
               + lax.broadcasted_iota(jnp.int32, score.shape, 2))
        s = score.shape[0] * LANES
    else:
        idx = lax.broadcasted_iota(jnp.int32, score.shape, 1)
        s = score.shape[1]
    key = _sort_key(score)
    kf = float(k)
    prefix = jnp.where(count(key >= 0) >= kf, 0, INT_MIN).astype(jnp.int32)

    def value_bit(b, prefix):
        cand = prefix | lax.shift_left(jnp.int32(1), 30 - b)
        return jnp.where(count(key >= cand) >= kf, cand, prefix)

    thr = lax.fori_loop(0, 31, value_bit, prefix)
    above = key > thr
    tie = key == thr
    need = kf - count(above)
    nbits = max(1, (s - 1).bit_length())

    def index_bit(b, p):
        cand = p | lax.shift_left(jnp.int32(1), nbits - 1 - b)
        return jnp.where(count(tie & (idx < cand)) < need, cand, p)

    last = lax.fori_loop(0, nbits, index_bit, jnp.zeros_like(prefix))
    return above | (tie & (idx <= last))


def _masked_softmax_pv(logits, keep, vt):
    logits = jnp.where(keep, logits, NEG_INF)
    p = jnp.exp(logits - jnp.max(logits, axis=1, keepdims=True))
    return _dot_nt(p.astype(BF16), vt) / jnp.sum(p, axis=1, keepdims=True)


ATT_SCALE = 1.0 / math.sqrt(HEAD_DIM)


KEY_CHUNK = 512


def _chunk(c, ck):
    return pl.ds(pl.multiple_of(c * ck, ck), ck)


def _flash_work(n_heads, qb, ck):
    return [pltpu.VMEM((n_heads, qb, ck), F32), pltpu.VMEM((n_heads, qb, ck), BF16),
            pltpu.VMEM((n_heads, qb, 1), F32)]


def _flash_chunks(q_ref, heads, kv_rows, k_ref, v_ref, bias_s, m_s, l_s, acc_s, work, n_act, ck):
    s_s, p_s, a_s = work
    for h in heads:
        m_s[h] = jnp.full(m_s.shape[1:], NEG_INF, F32)
        l_s[h] = jnp.zeros(l_s.shape[1:], F32)
        acc_s[h] = jnp.zeros(acc_s.shape[1:], F32)

    def chunk_step(sl):
        bias = bias_s[:, sl]
        for j, h in enumerate(heads):
            s_s[j] = _dot(q_ref[h] * ATT_SCALE, k_ref[kv_rows(h), sl]) + bias
        row_max = [jnp.max(s_s[j], axis=1, keepdims=True) for j, _ in enumerate(heads)]
        m_new = [jnp.maximum(m_s[h], row_max[j]) for j, h in enumerate(heads)]
        row_sum = []
        for j, h in enumerate(heads):
            p = jnp.exp(s_s[j] - m_new[j])
            p_s[j] = p.astype(BF16)
            row_sum.append(jnp.sum(p, axis=1, keepdims=True))
        for j, h in enumerate(heads):
            alpha = jnp.exp(m_s[h] - m_new[j])
            l_s[h] = alpha * l_s[h] + row_sum[j]
            m_s[h] = m_new[j]
            a_s[j] = alpha
        for j, h in enumerate(heads):
            acc_s[h] = a_s[j] * acc_s[h] + _dot_nt(p_s[j], v_ref[kv_rows(h), sl])

    for c in range(bias_s.shape[1] // ck):
        pl.when(c < n_act)(functools.partial(chunk_step, slice(c * ck, (c + 1) * ck)))
    for h in heads:
        acc_s[h] = acc_s[h] / l_s[h]


def _dsa_prompt_kernel(q_ref, qi_ref, small_ref, ki_ref, k_ref, v_ref, o_ref, key_s, bias_s, o_s, m_s, l_s,
                       s_s, p_s, a_s, *, topk, ck):
    i = pl.program_id(1)
    qb, t = key_s.shape
    n_act = (i * qb + qb - 1) // ck + 1
    tq = i * qb + lax.broadcasted_iota(jnp.int32, (qb, 1), 0)
    lane = lax.broadcasted_iota(jnp.int32, (qb, ck), 1)
    kf = float(topk)
    wi = [small_ref[:, WI_LANE + h:WI_LANE + h + 1] for h in range(IDX_HEADS)]

    def score_chunk(c, carry):
        sl = _chunk(c, ck)
        ki = ki_ref[:, sl]
        score = None
        for h in range(IDX_HEADS):
            term = wi[h] * jnp.maximum(_dot(qi_ref[h], ki), 0.0)
            score = term if score is None else score + term
        key_s[:, sl] = _sort_key(jnp.where(c * ck + lane <= tq, score, NEG_INF))
        return carry

    lax.fori_loop(0, n_act, score_chunk, 0)

    def count_keys(pred):
        def body(c, acc):
            hit = pred(key_s[:, _chunk(c, ck)], c).astype(F32)
            for j in range(ck // LANES):
                acc = acc + hit[:, j * LANES:(j + 1) * LANES]
            return acc
        return _count(lax.fori_loop(0, n_act, body, jnp.zeros((qb, LANES), F32)))

    prefix = jnp.where(count_keys(lambda k, c: k >= 0) >= kf, 0, INT_MIN).astype(jnp.int32)

    def value_bit(b, prefix):
        cand = prefix | lax.shift_left(jnp.int32(1), 30 - b)
        return jnp.where(count_keys(lambda k, c: k >= cand) >= kf, cand, prefix)

    thr = lax.fori_loop(0, 31, value_bit, prefix)
    need = kf - count_keys(lambda k, c: k > thr)
    n_tie = count_keys(lambda k, c: k == thr)
    nbits = max(1, (t - 1).bit_length())

    def resolve_ties():
        def index_bit(b, p):
            cand = p | lax.shift_left(jnp.int32(1), nbits - 1 - b)
            below = count_keys(lambda k, c: (k == thr) & (c * ck + lane < cand))
            return jnp.where(below < need, cand, p)
        return lax.fori_loop(0, nbits, index_bit, jnp.zeros((qb, 1), jnp.int32))

    last = lax.cond(jnp.max(n_tie - need) > 0.0, resolve_ties, lambda: jnp.full((qb, 1), t, jnp.int32))

    def bias_chunk(c, carry):
        sl = _chunk(c, ck)
        k = key_s[:, sl]
        kpos = c * ck + lane
        keep = ((k > thr) | ((k == thr) & (kpos <= last))) & (kpos <= tq)
        bias_s[:, sl] = jnp.where(keep, 0.0, NEG_INF)
        return carry

    lax.fori_loop(0, n_act, bias_chunk, 0)

    _flash_chunks(q_ref, range(DSA_HEADS), lambda h: slice(h * HEAD_DIM, (h + 1) * HEAD_DIM), k_ref, v_ref,
                  bias_s, m_s, l_s, o_s, (s_s, p_s, a_s), n_act, ck)
    for h in range(DSA_HEADS):
        o_ref[:, h * HEAD_DIM:(h + 1) * HEAD_DIM] = o_s[h].astype(o_ref.dtype)


def dsa_prompt(qh, zp, small_col, ki_t, kv_t, batch, t):
    qb = QUERY_BLOCK
    nq = t // qb
    topk = min(DSA_TOPK, t // 4)
    ck = min(KEY_CHUNK, t)
    assert t % ck == 0 and ck >= topk and ck % qb == 0
    return pl.pallas_call(
        functools.partial(_dsa_prompt_kernel, topk=topk, ck=ck),
        grid=(batch, nq),
        in_specs=[pl.BlockSpec((DSA_HEADS, qb, HEAD_DIM), lambda b, i: (0, b * nq + i, 0)),
                  pl.BlockSpec((IDX_HEADS, qb, HEAD_DIM), lambda b, i: (4, b * nq + i, 0)),
                  pl.BlockSpec((qb, LANES), lambda b, i: (b * nq + i, small_col // LANES)),
                  pl.BlockSpec((None, IDX_DIM, t), lambda b, i: (b, 0, 0)),
                  pl.BlockSpec((None, MIX_WIDTH, t), lambda b, i: (b, 0, 0)),
                  pl.BlockSpec((None, MIX_WIDTH, t), lambda b, i: (b, 1, 0))],
        out_specs=pl.BlockSpec((qb, MIX_WIDTH), lambda b, i: (b * nq + i, 0)),
        out_shape=jax.ShapeDtypeStruct((batch * t, MIX_WIDTH), BF16),
        scratch_shapes=[pltpu.VMEM((qb, t), jnp.int32), pltpu.VMEM((qb, t), F32),
                        pltpu.VMEM((DSA_HEADS, qb, HEAD_DIM), F32),
                        pltpu.VMEM((DSA_HEADS, qb, 1), F32), pltpu.VMEM((DSA_HEADS, qb, 1), F32)]
        + _flash_work(DSA_HEADS, qb, ck),
        compiler_params=_cp("parallel", "arbitrary"),
        name="dsa_prompt",
    )(qh, qh, zp, ki_t, kv_t, kv_t)


NSA_GROUP_ROWS = NSA_KV_HEADS * HEAD_DIM


def _nsa_pool_kernel(x_ref, o_ref):
    t = x_ref.shape[1]
    nsub = t // CMP_STRIDE
    tok = lax.broadcasted_iota(jnp.int32, (t, nsub), 0)
    c = lax.broadcasted_iota(jnp.int32, (t, nsub), 1)
    inside = (tok >= c * CMP_STRIDE) & (tok < c * CMP_STRIDE + CMP_LEN) & (c < nsub - 1)
    pool = jnp.where(inside, 1.0 / CMP_LEN, 0.0).astype(F32)
    o_ref[...] = jnp.dot(x_ref[...], pool, preferred_element_type=F32, precision=HIGHEST).astype(o_ref.dtype)


def nsa_pool(nsa_t, batch, t):
    rows = 2 * NSA_GROUP_ROWS
    return pl.pallas_call(
        _nsa_pool_kernel,
        grid=(batch,),
        in_specs=[pl.BlockSpec((None, rows, t), lambda b: (b, 0, 0))],
        out_specs=pl.BlockSpec((None, rows, t // CMP_STRIDE), lambda b: (b, 0, 0)),
        out_shape=jax.ShapeDtypeStruct((batch, rows, t // CMP_STRIDE), BF16),
        compiler_params=_cp("parallel"),
        name="nsa_pool",
    )(nsa_t)


def _nsa_select_blocks(pcsum, tq, ns, n_sel, lanes=None):
    rows, nc = pcsum.shape
    lanes = ns if lanes is None else lanes
    c = lax.broadcasted_iota(jnp.int32, (nc, lanes), 0)
    j = lax.broadcasted_iota(jnp.int32, (nc, lanes), 1)
    pool = ((c >= j * CMP_PER_SEL) & (c < (j + 1) * CMP_PER_SEL)).astype(F32)
    imp = jnp.dot(pcsum, pool, preferred_element_type=F32, precision=HIGHEST)
    blk = lax.broadcasted_iota(jnp.int32, (rows, lanes), 1)
    forced = (blk == tq // SEL_BLOCK) | (blk == 0)
    imp = jnp.where(forced, FORCE_SCORE, imp)
    imp = jnp.where((blk * SEL_BLOCK <= tq) & (blk < ns), imp, NEG_INF)
    return _topk_mask(imp, n_sel) & (blk < ns)


def _top_rows(v, n):
    ns = v.shape[0]
    row = lax.broadcasted_iota(jnp.int32, v.shape, 0)
    rank = jnp.zeros(v.shape, F32)
    for i in range(ns):
        vi = v[i:i + 1, :]
        rank = rank + ((vi > v) | ((vi == v) & (row > i))).astype(F32)
    return rank < float(n)


def _nsa_prompt_kernel(q_ref, small_ref, cm_ref, ksel_ref, vsel_ref, w0, w1, w2, w3, w4, o_ref,
                       bias_s, ocmp_s, osel_s, m_s, l_s, s_s, p_s, a_s, *, ns, n_sel, ck):
    i = pl.program_id(1)
    qb, s = bias_s.shape
    n_act = (i * qb + qb - 1) // ck + 1
    tq = i * qb + lax.broadcasted_iota(jnp.int32, (qb, 1), 0)
    tq_row = i * qb + lax.broadcasted_iota(jnp.int32, (ns, qb), 1)
    blk = lax.broadcasted_iota(jnp.int32, (ns, qb), 0)
    lane = lax.broadcasted_iota(jnp.int32, (qb, ck), 1)
    ncp = cm_ref.shape[1]
    cidx = lax.broadcasted_iota(jnp.int32, (qb, ncp), 1)
    cvalid = (cidx * CMP_STRIDE + CMP_LEN - 1 <= tq) & (cidx < ncp - 1)
    for g in range(NSA_KV_HEADS):
        grows = slice(g * HEAD_DIM, (g + 1) * HEAD_DIM)
        vrows = slice(NSA_GROUP_ROWS + g * HEAD_DIM, NSA_GROUP_ROWS + (g + 1) * HEAD_DIM)
        kcm = cm_ref[grows, :]
        vcm = cm_ref[vrows, :]
        pcsum = jnp.zeros((qb, ncp), F32)
        for r in range(NSA_REP):
            h = g * NSA_REP + r
            lc = jnp.where(cvalid, _dot(q_ref[h], kcm) * ATT_SCALE, NEG_INF)
            p = jnp.exp(lc - jnp.max(lc, axis=1, keepdims=True))
            pc = jnp.where(cvalid, p / jnp.sum(p, axis=1, keepdims=True), 0.0)
            ocmp_s[h] = _dot_nt(pc.astype(BF16), vcm)
            pcsum = pcsum + pc
        pj = lax.broadcasted_iota(jnp.int32, (ns, ncp), 0)
        pc_ = lax.broadcasted_iota(jnp.int32, (ns, ncp), 1)
        pool_t = ((pc_ >= pj * CMP_PER_SEL) & (pc_ < (pj + 1) * CMP_PER_SEL)).astype(F32)
        imp = _dot_nt(pool_t, pcsum, precision=HIGHEST)
        imp = jnp.where((blk == tq_row // SEL_BLOCK) | (blk == 0), FORCE_SCORE, imp)
        imp = jnp.where(blk * SEL_BLOCK <= tq_row, imp, NEG_INF)
        sel = _top_rows(imp, n_sel).astype(F32).T.astype(BF16)

        def bias_chunk(c, carry, sel=sel):
            bj = lax.broadcasted_iota(jnp.int32, (ns, ck), 0)
            tk = c * ck + lax.broadcasted_iota(jnp.int32, (ns, ck), 1)
            expand = (tk // SEL_BLOCK == bj).astype(BF16)
            keep = (_dot(sel, expand) > 0.5) & (c * ck + lane <= tq)
            bias_s[:, _chunk(c, ck)] = jnp.where(keep, 0.0, NEG_INF)
            return carry

        lax.fori_loop(0, n_act, bias_chunk, 0)

        _flash_chunks(q_ref, range(g * NSA_REP, (g + 1) * NSA_REP), lambda h, grows=grows: grows,
                      ksel_ref, vsel_ref, bias_s, m_s, l_s, osel_s, (s_s, p_s, a_s), n_act, ck)

    wins = (w0, w1, w2, w3, w4)
    nwin = len(wins)
    kw = jnp.concatenate([w[...] for w in wins], axis=1)
    lane = lax.broadcasted_iota(jnp.int32, (qb, nwin * LANES), 1)
    wpos = (i - (nwin - 1)) * LANES + lane
    wok = (wpos >= 0) & (wpos <= tq) & (wpos > tq - WINDOW)
    gates = _sigmoid(small_ref[...])
    for h in range(NSA_HEADS):
        g = h // NSA_REP
        kwin = kw[g * HEAD_DIM:(g + 1) * HEAD_DIM]
        vwin = kw[NSA_GROUP_ROWS + g * HEAD_DIM:NSA_GROUP_ROWS + (g + 1) * HEAD_DIM]
        o_w = _masked_softmax_pv(_dot(q_ref[h], kwin) * ATT_SCALE, wok, vwin)
        gl = GN_LANE + 3 * h
        o = (gates[:, gl:gl + 1] * ocmp_s[h] + gates[:, gl + 1:gl + 2] * osel_s[h]
             + gates[:, gl + 2:gl + 3] * o_w)
        o_ref[:, h * HEAD_DIM:(h + 1) * HEAD_DIM] = o.astype(o_ref.dtype)


def nsa_prompt(qh, zp, small_col, cm_t, nsa_t, win_t, batch, t):
    qb = QUERY_BLOCK
    nq = t // qb
    ns = -(-t // SEL_BLOCK)
    nwin = WINDOW // qb + 1
    win_specs = [pl.BlockSpec((None, 2 * NSA_GROUP_ROWS, qb),
                              lambda b, i, j=j: (b, 0, jnp.maximum(i - (nwin - 1) + j, 0)))
                 for j in range(nwin)]
    ck = min(KEY_CHUNK, t)
    assert t % ck == 0 and ck % qb == 0 and t % SEL_BLOCK == 0
    return pl.pallas_call(
        functools.partial(_nsa_prompt_kernel, ns=ns, n_sel=min(SEL_TOPN, ns), ck=ck),
        grid=(batch, nq),
        in_specs=[pl.BlockSpec((NSA_HEADS, qb, HEAD_DIM), lambda b, i: (1, b * nq + i, 0)),
                  pl.BlockSpec((qb, LANES), lambda b, i: (b * nq + i, small_col // LANES)),
                  pl.BlockSpec((None, 2 * NSA_GROUP_ROWS, cm_t.shape[2]), lambda b, i: (b, 0, 0)),
                  pl.BlockSpec((None, NSA_GROUP_ROWS, t), lambda b, i: (b, 2, 0)),
                  pl.BlockSpec((None, NSA_GROUP_ROWS, t), lambda b, i: (b, 3, 0))] + win_specs,
        out_specs=pl.BlockSpec((qb, MIX_WIDTH), lambda b, i: (b * nq + i, 0)),
        out_shape=jax.ShapeDtypeStruct((batch * t, MIX_WIDTH), BF16),
        scratch_shapes=[pltpu.VMEM((qb, t), F32), pltpu.VMEM((NSA_HEADS, qb, HEAD_DIM), F32),
                        pltpu.VMEM((NSA_HEADS, qb, HEAD_DIM), F32),
                        pltpu.VMEM((NSA_HEADS, qb, 1), F32), pltpu.VMEM((NSA_HEADS, qb, 1), F32)]
        + _flash_work(NSA_REP, qb, ck),
        compiler_params=_cp("parallel", "arbitrary"),
        name="nsa_prompt",
    )(qh, zp, cm_t, nsa_t, nsa_t, *([win_t] * nwin))


PAGES_PER_STEP = 8


def _page_specs(block, layer, n_pages, slot=None, per_step=PAGES_PER_STEP):
    def spec(j):
        def index(b, s, pt):
            page = pt[b, jnp.minimum(s * per_step + j, n_pages - 1)]
            lead = (layer, page) if slot is None else (layer, page, slot)
            return lead + (0,) * (len(block) - len(lead))
        return pl.BlockSpec(block, index)
    return [spec(j) for j in range(per_step)]


def _dsa_sample_scores_kernel(pt_ref, qi_ref, small_ref, kinew_ref, *rest, n_steps):
    pages, o_ref = rest[:PAGES_PER_STEP], rest[PAGES_PER_STEP]
    s = pl.program_id(1)
    tnew = qi_ref.shape[1]

    def score(keys_t=None, keys=None):
        acc = None
        for h in range(IDX_HEADS):
            d = _dot(qi_ref[h], keys_t) if keys is None else _dot_nt(qi_ref[h], keys)
            term = small_ref[:, WI_LANE + h:WI_LANE + h + 1] * jnp.maximum(d, 0.0)
            acc = term if acc is None else acc + term
        return acc

    @pl.when(s < n_steps - 1)
    def _():
        for j, page in enumerate(pages):
            o_ref[j] = score(keys_t=page[...].astype(BF16))

    @pl.when(s == n_steps - 1)
    def _():
        sc = score(keys=kinew_ref[...])
        q = lax.broadcasted_iota(jnp.int32, sc.shape, 0)
        k = lax.broadcasted_iota(jnp.int32, sc.shape, 1)
        o_ref[0] = jnp.where((k <= q) & (k < tnew), sc, NEG_INF)
        for j in range(1, PAGES_PER_STEP):
            o_ref[j] = jnp.full(sc.shape, NEG_INF, F32)


def dsa_sample_scores(page_table, qh, zp, ki_new, kidx_view, layer):
    batch, n_pages = page_table.shape
    tnew = qh.shape[1] // batch
    n_steps = n_pages // PAGES_PER_STEP + 1
    return pl.pallas_call(
        functools.partial(_dsa_sample_scores_kernel, n_steps=n_steps),
        grid_spec=pltpu.PrefetchScalarGridSpec(
            num_scalar_prefetch=1,
            grid=(batch, n_steps),
            in_specs=[pl.BlockSpec((IDX_HEADS, tnew, HEAD_DIM), lambda b, s, pt: (4, b, 0)),
                      pl.BlockSpec((tnew, LANES), lambda b, s, pt: (b, ZP_SMALL // LANES)),
                      pl.BlockSpec((None, LANES, IDX_DIM), lambda b, s, pt: (b, 0, 0))]
            + _page_specs((None, None, IDX_DIM, LANES), layer, n_pages),
            out_specs=pl.BlockSpec((None, PAGES_PER_STEP, tnew, LANES), lambda b, s, pt: (b, s, 0, 0))),
        out_shape=jax.ShapeDtypeStruct((batch, PAGES_PER_STEP * n_steps, tnew, LANES), F32),
        compiler_params=_cp("parallel", "arbitrary"),
        name="dsa_sample_scores",
    )(page_table, qh, zp, ki_new, *([kidx_view] * PAGES_PER_STEP))


def _online_softmax_step(logits, keep, v_t, m_ref, l_ref, acc_ref, v_rows=None):
    lm = jnp.where(keep, logits, NEG_INF)
    m_old = m_ref[...]
    m_new = jnp.maximum(m_old, jnp.max(lm, axis=1, keepdims=True))
    alpha = jnp.exp(m_old - m_new)
    p = jnp.where(keep, jnp.exp(lm - m_new), 0.0)
    pv = _dot_nt(p.astype(BF16), v_t) if v_rows is None else _dot(p.astype(BF16), v_rows)
    l_ref[...] = alpha * l_ref[...] + jnp.sum(p, axis=1, keepdims=True)
    acc_ref[...] = alpha * acc_ref[...] + pv
    m_ref[...] = m_new


def _dsa_sample_attn_kernel(pt_ref, score_ref, qbd_ref, kvnew_ref, *rest, n_steps, topk):
    pages, o_ref = rest[:PAGES_PER_STEP], rest[PAGES_PER_STEP]
    keep_s, m_s, l_s, acc_s = rest[PAGES_PER_STEP + 1:]
    s = pl.program_id(1)
    tnew = score_ref.shape[1]

    @pl.when(s == 0)
    def _():
        keep_s[...] = _topk_mask(score_ref[...], topk, paged=True).astype(F32)
        m_s[...] = jnp.full(m_s.shape, NEG_INF, F32)
        l_s[...] = jnp.zeros(l_s.shape, F32)
        acc_s[...] = jnp.zeros(acc_s.shape, F32)

    qbd = qbd_ref[...] * ATT_SCALE

    def keep_rows(page):
        return jnp.concatenate([keep_s[page]] * DSA_HEADS, axis=0) > 0.5

    @pl.when(s < n_steps - 1)
    def _():
        k_t = jnp.concatenate([p[0].reshape(MIX_WIDTH, LANES).astype(BF16) for p in pages], axis=1)
        v_t = jnp.concatenate([p[1].reshape(MIX_WIDTH, LANES).astype(BF16) for p in pages], axis=1)
        keep = jnp.concatenate([keep_rows(s * PAGES_PER_STEP + j) for j in range(PAGES_PER_STEP)], axis=1)
        _online_softmax_step(_dot(qbd, k_t), keep, v_t, m_s, l_s, acc_s)

    @pl.when(s == n_steps - 1)
    def _():
        kv = kvnew_ref[...]
        logits = _dot_nt(qbd, kv[:, :MIX_WIDTH])
        q = lax.broadcasted_iota(jnp.int32, logits.shape, 0) % tnew
        k = lax.broadcasted_iota(jnp.int32, logits.shape, 1)
        keep = keep_rows((n_steps - 1) * PAGES_PER_STEP) & (k <= q) & (k < tnew)
        _online_softmax_step(logits, keep, None, m_s, l_s, acc_s, v_rows=kv[:, MIX_WIDTH:])
        out = acc_s[...] / l_s[...]
        for h in range(DSA_HEADS):
            o_ref[:, h * HEAD_DIM:(h + 1) * HEAD_DIM] = (
                out[h * tnew:(h + 1) * tnew, h * HEAD_DIM:(h + 1) * HEAD_DIM].astype(o_ref.dtype))


def dsa_sample_attn(page_table, scores, qbd, kv_new, kv_view, layer):
    batch, n_pages = page_table.shape
    tnew = scores.shape[2]
    n_steps = n_pages // PAGES_PER_STEP + 1
    topk = min(DSA_TOPK, (n_pages * LANES + tnew) // 4)
    rows = DSA_HEADS * tnew
    return pl.pallas_call(
        functools.partial(_dsa_sample_attn_kernel, n_steps=n_steps, topk=topk),
        grid_spec=pltpu.PrefetchScalarGridSpec(
            num_scalar_prefetch=1,
            grid=(batch, n_steps),
            in_specs=[pl.BlockSpec((None,) + scores.shape[1:], lambda b, s, pt: (b, 0, 0, 0)),
                      pl.BlockSpec((None, rows, MIX_WIDTH), lambda b, s, pt: (b, 0, 0)),
                      pl.BlockSpec((None, LANES, 2 * MIX_WIDTH), lambda b, s, pt: (b, 0, 0))]
            + _page_specs((None, None, 2, DSA_HEADS, HEAD_DIM, LANES), layer, n_pages),
            out_specs=pl.BlockSpec((tnew, MIX_WIDTH), lambda b, s, pt: (b, 0)),
            scratch_shapes=[pltpu.VMEM(scores.shape[1:], F32), pltpu.VMEM((rows, 1), F32),
                            pltpu.VMEM((rows, 1), F32), pltpu.VMEM((rows, MIX_WIDTH), F32)]),
        out_shape=jax.ShapeDtypeStruct((batch * tnew, MIX_WIDTH), BF16),
        compiler_params=_cp("parallel", "arbitrary"),
        name="dsa_sample_attn",
    )(page_table, scores, qbd, kv_new, *([kv_view] * PAGES_PER_STEP))


CMP_PAGES_PER_STEP = 16
SUB_PER_PAGE = LANES // CMP_STRIDE


def _nsa_sample_cmp_kernel(pt_ref, q_ref, *rest, n_steps, past, ns, n_sel):
    pages = rest[:CMP_PAGES_PER_STEP]
    ocmp_ref, sel_ref, sub_s = rest[CMP_PAGES_PER_STEP:]
    s = pl.program_id(1)
    tnew = q_ref.shape[1]
    rows = 2 * NSA_GROUP_ROWS
    tok = lax.broadcasted_iota(jnp.int32, (LANES, LANES), 0)
    col = lax.broadcasted_iota(jnp.int32, (LANES, LANES), 1)
    sub = None
    for j, page in enumerate(pages):
        pool = jnp.where(col == j * SUB_PER_PAGE + tok // CMP_STRIDE, 1.0 / CMP_STRIDE, 0.0).astype(F32)
        term = jnp.dot(page[...].reshape(rows, LANES), pool, preferred_element_type=F32, precision=HIGHEST)
        sub = term if sub is None else sub + term
    sub_s[s] = sub

    @pl.when(s == n_steps - 1)
    def _():
        sub_all = jnp.concatenate([sub_s[i] for i in range(n_steps)], axis=1)
        ncp = sub_all.shape[1]
        cm = (0.5 * (sub_all + pltpu.roll(sub_all, ncp - 1, axis=1))).astype(BF16)
        nc = (past + tnew) // CMP_STRIDE - 1
        tq = past + lax.broadcasted_iota(jnp.int32, (tnew, 1), 0)
        cidx = lax.broadcasted_iota(jnp.int32, (tnew, ncp), 1)
        cvalid = (cidx * CMP_STRIDE + CMP_LEN - 1 <= tq) & (cidx < nc)
        for g in range(NSA_KV_HEADS):
            kcm = cm[g * HEAD_DIM:(g + 1) * HEAD_DIM]
            vcm = cm[NSA_GROUP_ROWS + g * HEAD_DIM:NSA_GROUP_ROWS + (g + 1) * HEAD_DIM]
            pcsum = jnp.zeros((tnew, ncp), F32)
            for r in range(NSA_REP):
                h = g * NSA_REP + r
                lc = jnp.where(cvalid, _dot(q_ref[h], kcm) * ATT_SCALE, NEG_INF)
                p = jnp.exp(lc - jnp.max(lc, axis=1, keepdims=True))
                pc = jnp.where(cvalid, p / jnp.sum(p, axis=1, keepdims=True), 0.0)
                ocmp_ref[h] = _dot_nt(pc.astype(BF16), vcm)
                pcsum = pcsum + pc
            sel_ref[g] = _nsa_select_blocks(pcsum, tq, ns, n_sel, lanes=sel_ref.shape[-1]).astype(F32)


def nsa_sample_cmp(page_table, qh, nsa_view, layer):
    batch, n_pages = page_table.shape
    tnew = qh.shape[1] // batch
    past = n_pages * LANES
    assert tnew < CMP_STRIDE and n_pages % CMP_PAGES_PER_STEP == 0
    n_steps = n_pages // CMP_PAGES_PER_STEP
    ns = -(-(past + tnew) // SEL_BLOCK)
    ns_lanes = -(-ns // LANES) * LANES
    return pl.pallas_call(
        functools.partial(_nsa_sample_cmp_kernel, n_steps=n_steps, past=past, ns=ns, n_sel=min(SEL_TOPN, ns)),
        grid_spec=pltpu.PrefetchScalarGridSpec(
            num_scalar_prefetch=1,
            grid=(batch, n_steps),
            in_specs=[pl.BlockSpec((NSA_HEADS, tnew, HEAD_DIM), lambda b, s, pt: (1, b, 0))]
            + _page_specs((None, None, 2, NSA_KV_HEADS, HEAD_DIM, LANES), layer, n_pages, slot=0,
                          per_step=CMP_PAGES_PER_STEP),
            out_specs=[pl.BlockSpec((None, NSA_HEADS, tnew, HEAD_DIM), lambda b, s, pt: (b, 0, 0, 0)),
                       pl.BlockSpec((None, NSA_KV_HEADS, tnew, ns_lanes), lambda b, s, pt: (b, 0, 0, 0))],
            scratch_shapes=[pltpu.VMEM((n_steps, 2 * NSA_GROUP_ROWS, LANES), F32)]),
        out_shape=[jax.ShapeDtypeStruct((batch, NSA_HEADS, tnew, HEAD_DIM), F32),
                   jax.ShapeDtypeStruct((batch, NSA_KV_HEADS, tnew, ns_lanes), F32)],
        compiler_params=_cp("parallel", "arbitrary"),
        name="nsa_sample_cmp",
    )(page_table, qh, *([nsa_view] * CMP_PAGES_PER_STEP))


def _nsa_sample_sel_kernel(pt_ref, q_ref, sel_ref, ocmp_ref, small_ref, new_ref, wbuf_ref, wnew_ref, *rest,
                           n_steps, past):
    pages, o_ref = rest[:PAGES_PER_STEP], rest[PAGES_PER_STEP]
    m_s, l_s, acc_s = rest[PAGES_PER_STEP + 1:]
    s = pl.program_id(1)
    tnew = q_ref.shape[1]
    grp_rows = NSA_REP * tnew

    @pl.when(s == 0)
    def _():
        m_s[...] = jnp.full(m_s.shape, NEG_INF, F32)
        l_s[...] = jnp.zeros(l_s.shape, F32)
        acc_s[...] = jnp.zeros(acc_s.shape, F32)

    q_all = q_ref[...].reshape(NSA_HEADS * tnew, HEAD_DIM) * ATT_SCALE

    @pl.when(s < n_steps - 1)
    def _():
        step_tokens = PAGES_PER_STEP * LANES
        nsl = sel_ref.shape[2]
        bj = lax.broadcasted_iota(jnp.int32, (nsl, step_tokens), 0)
        tk = lax.broadcasted_iota(jnp.int32, (nsl, step_tokens), 1)
        expand = (bj == s * (step_tokens // SEL_BLOCK) + tk // SEL_BLOCK).astype(BF16)
        for g in range(NSA_KV_HEADS):
            keep_g = _dot(sel_ref[g].astype(BF16), expand) > 0.5
            keep = jnp.concatenate([keep_g] * NSA_REP, axis=0)
            qg = q_all[g * grp_rows:(g + 1) * grp_rows]
            k_t = jnp.concatenate([p[0, g].astype(BF16) for p in pages], axis=1)
            v_t = jnp.concatenate([p[1, g].astype(BF16) for p in pages], axis=1)
            _online_softmax_step(_dot(qg, k_t), keep, v_t, m_s.at[g], l_s.at[g], acc_s.at[g])

    @pl.when(s == n_steps - 1)
    def _():
        new = new_ref[...]
        new_block = past // SEL_BLOCK
        osel = []
        for g in range(NSA_KV_HEADS):
            qg = q_all[g * grp_rows:(g + 1) * grp_rows]
            k_new = new[:, (2 * NSA_KV_HEADS + g) * HEAD_DIM:(2 * NSA_KV_HEADS + g + 1) * HEAD_DIM]
            v_new = new[:, (3 * NSA_KV_HEADS + g) * HEAD_DIM:(3 * NSA_KV_HEADS + g + 1) * HEAD_DIM]
            logits = _dot_nt(qg, k_new)
            q = lax.broadcasted_iota(jnp.int32, logits.shape, 0) % tnew
            k = lax.broadcasted_iota(jnp.int32, logits.shape, 1)
            chosen = jnp.concatenate([sel_ref[g][:, new_block:new_block + 1]] * NSA_REP, axis=0) > 0.5
            _online_softmax_step(logits, chosen & (k <= q) & (k < tnew), None, m_s.at[g], l_s.at[g],
                                 acc_s.at[g], v_rows=v_new)
            osel.append(acc_s[g] / l_s[g])
        wbuf = wbuf_ref[...].reshape(2 * NSA_GROUP_ROWS, wbuf_ref.shape[-1]).astype(BF16)
        wnew = wnew_ref[...]
        wb = wbuf.shape[1]
        tq = past + lax.broadcasted_iota(jnp.int32, (tnew, 1), 0)
        pos_buf = past - wb + lax.broadcasted_iota(jnp.int32, (tnew, wb), 1)
        kn = lax.broadcasted_iota(jnp.int32, (tnew, LANES), 1)
        ok = jnp.concatenate([(pos_buf <= tq) & (pos_buf > tq - WINDOW),
                              (kn < tnew) & (past + kn <= tq) & (past + kn > tq - WINDOW)], axis=1)
        gates = _sigmoid(small_ref[...])
        for h in range(NSA_HEADS):
            g, r = divmod(h, NSA_REP)
            qh_ = q_ref[h]
            k_buf = wbuf[g * HEAD_DIM:(g + 1) * HEAD_DIM]
            v_buf = wbuf[NSA_GROUP_ROWS + g * HEAD_DIM:NSA_GROUP_ROWS + (g + 1) * HEAD_DIM]
            k_new = wnew[:, g * HEAD_DIM:(g + 1) * HEAD_DIM]
            v_new = wnew[:, NSA_GROUP_ROWS + g * HEAD_DIM:NSA_GROUP_ROWS + (g + 1) * HEAD_DIM]
            logits = jnp.concatenate([_dot(qh_, k_buf), _dot_nt(qh_, k_new)], axis=1) * ATT_SCALE
            logits = jnp.where(ok, logits, NEG_INF)
            p = jnp.exp(logits - jnp.max(logits, axis=1, keepdims=True))
            pb = p.astype(BF16)
            o_w = (_dot_nt(pb[:, :wb], v_buf) + _dot(pb[:, wb:], v_new)) / jnp.sum(p, axis=1, keepdims=True)
            gl = GN_LANE + 3 * h
            o = (gates[:, gl:gl + 1] * ocmp_ref[h] + gates[:, gl + 1:gl + 2] * osel[g][r * tnew:(r + 1) * tnew]
                 + gates[:, gl + 2:gl + 3] * o_w)
            o_ref[:, h * HEAD_DIM:(h + 1) * HEAD_DIM] = o.astype(o_ref.dtype)


def nsa_sample_sel(page_table, qh, sel, ocmp, zp, nsa_new, win_view, win_new, nsa_view, layer):
    batch, n_pages = page_table.shape
    tnew = qh.shape[1] // batch
    past = n_pages * LANES
    assert past % SEL_BLOCK == 0 and tnew <= SEL_BLOCK
    n_steps = n_pages // PAGES_PER_STEP + 1
    grp_rows = NSA_REP * tnew
    full = lambda a: pl.BlockSpec((None,) + a.shape[1:], lambda b, s, pt: (b,) + (0,) * (a.ndim - 1))
    return pl.pallas_call(
        functools.partial(_nsa_sample_sel_kernel, n_steps=n_steps, past=past),
        grid_spec=pltpu.PrefetchScalarGridSpec(
            num_scalar_prefetch=1,
            grid=(batch, n_steps),
            in_specs=[pl.BlockSpec((NSA_HEADS, tnew, HEAD_DIM), lambda b, s, pt: (1, b, 0)),
                      full(sel), full(ocmp),
                      pl.BlockSpec((tnew, LANES), lambda b, s, pt: (b, ZP_SMALL // LANES)),
                      full(nsa_new),
                      pl.BlockSpec((None, None) + win_view.shape[2:], lambda b, s, pt: (layer, b, 0, 0, 0, 0)),
                      full(win_new)]
            + _page_specs((None, None, 2, NSA_KV_HEADS, HEAD_DIM, LANES), layer, n_pages, slot=1),
            out_specs=pl.BlockSpec((tnew, MIX_WIDTH), lambda b, s, pt: (b, 0)),
            scratch_shapes=[pltpu.VMEM((NSA_KV_HEADS, grp_rows, 1), F32), pltpu.VMEM((NSA_KV_HEADS, grp_rows, 1), F32),
                            pltpu.VMEM((NSA_KV_HEADS, grp_rows, HEAD_DIM), F32)]),
        out_shape=jax.ShapeDtypeStruct((batch * tnew, MIX_WIDTH), BF16),
        compiler_params=_cp("parallel", "arbitrary"),
        name="nsa_sample_sel",
    )(page_table, qh, sel, ocmp, zp, nsa_new, win_view, win_new, *([nsa_view] * PAGES_PER_STEP))


def _merge_kernel(b0, b1, b2, b3, w_ref, g0, g1, g2, g3, o_ref):
    acc = None
    for k, (b_ref, g_ref) in enumerate(zip((b0, b1, b2, b3), (g0, g1, g2, g3))):
        term = _sigmoid(g_ref[...]) * _dot(b_ref[...], w_ref[k])
        acc = term if acc is None else acc + term
    o_ref[...] = acc.astype(o_ref.dtype)


def merge_branches(branches, w, zp, gate_col):
    m = branches[0].shape[0]
    tm = min(1024, m)
    tn = 512
    nj = D_MODEL // tn
    gate_specs = [pl.BlockSpec((tm, tn), lambda i, j, k=k: (i, gate_col // tn + k * nj + j))
                  for k in range(N_BRANCH)]
    return pl.pallas_call(
        _merge_kernel,
        grid=(m // tm, nj),
        in_specs=[pl.BlockSpec((tm, MIX_WIDTH), lambda i, j: (i, 0))] * N_BRANCH
        + [pl.BlockSpec((N_BRANCH, MIX_WIDTH, tn), lambda i, j: (0, 0, j))] + gate_specs,
        out_specs=pl.BlockSpec((tm, tn), lambda i, j: (i, j)),
        out_shape=jax.ShapeDtypeStruct((m, D_MODEL), BF16),
        compiler_params=_cp("parallel", "parallel"),
        name="merge_branches",
    )(*branches, w, *([zp] * N_BRANCH))


IN_SIZES = (MIX_WIDTH, 3 * MIX_WIDTH, IDX_HEADS * IDX_DIM, IDX_DIM, IDX_HEADS, MIX_WIDTH,
            6 * NSA_KV_HEADS * HEAD_DIM, 3 * NSA_HEADS, 3 * MIX_WIDTH, 2 * MLSTM_HEADS, MIX_WIDTH,
            N_BRANCH * D_MODEL)
(OFF_U, OFF_QKVB, OFF_QI, OFF_KI, OFF_WI, OFF_QN, OFF_KVN, OFF_GN, OFF_QKVM, OFF_GIF, OFF_OM,
 OFF_GBR) = np.concatenate([[0], np.cumsum(IN_SIZES)[:-1]]).tolist()

ZP_U, ZP_OM, ZP_Q, ZP_K, ZP_V, ZP_SMALL, ZP_GBR = 0, 512, 1024, 1536, 2048, 2560, 3072
ZP_WIDTH = ZP_GBR + N_BRANCH * D_MODEL
Q_WIDTH = 2 * MIX_WIDTH + IDX_HEADS * IDX_DIM
Q_HEADS = Q_WIDTH // HEAD_DIM


def _block_diag(blocks):
    g, r, c = blocks.shape
    eye = jnp.eye(g, dtype=blocks.dtype)
    return (blocks[:, :, None, :] * eye[:, None, :, None]).reshape(g * r, g * c)


def _s5_discretize(lam_re, lam_im, log_dt, b_re, b_im):
    dt = jnp.exp(log_dt)[:, None]
    mag = jnp.exp(lam_re * dt)
    a_re, a_im = mag * jnp.cos(lam_im * dt), mag * jnp.sin(lam_im * dt)
    den = lam_re * lam_re + lam_im * lam_im
    nr = a_re - 1.0
    coef_re = (nr * lam_re + a_im * lam_im) / den
    coef_im = (a_im * lam_re - nr * lam_im) / den
    bb_re = coef_re[..., None] * b_re - coef_im[..., None] * b_im
    bb_im = coef_re[..., None] * b_im + coef_im[..., None] * b_re
    return a_re, a_im, bb_re, bb_im


def _prep_layer(l, p):
    wt = jnp.transpose(p['w_in'], (2, 0, 1))[:, l, :]
    seg = lambda off, n: wt[off:off + n]
    wt_q = jnp.concatenate([seg(OFF_QKVB, MIX_WIDTH), seg(OFF_QN, MIX_WIDTH),
                            seg(OFF_QI, IDX_HEADS * IDX_DIM)]).astype(BF16)
    wt_kv = jnp.concatenate([seg(OFF_QKVB + MIX_WIDTH, 2 * MIX_WIDTH), seg(OFF_KI, IDX_DIM),
                             seg(OFF_KVN, 6 * NSA_KV_HEADS * HEAD_DIM)]).astype(BF16)
    small = jnp.concatenate([seg(OFF_WI, IDX_HEADS), seg(OFF_GN, 3 * NSA_HEADS), seg(OFF_GIF, 2 * MLSTM_HEADS)])
    pad = jnp.zeros((ZP_GBR - ZP_SMALL - small.shape[0], D_MODEL), F32)
    wt_plain = jnp.concatenate([seg(OFF_U, MIX_WIDTH), seg(OFF_OM, MIX_WIDTH), seg(OFF_QKVM, 3 * MIX_WIDTH),
                                small, pad, seg(OFF_GBR, N_BRANCH * D_MODEL)]).astype(BF16)
    a_re, a_im, bb_re, bb_im = _s5_discretize(p['s5_lam_re'][l], p['s5_lam_im'][l], p['s5_log_dt'][l],
                                              p['s5_b_re'][l], p['s5_b_im'][l])
    s5_b = jnp.concatenate([_block_diag(bb_re.transpose(0, 2, 1)), _block_diag(bb_im.transpose(0, 2, 1))],
                           axis=1).astype(BF16)
    s5_c = jnp.concatenate([_block_diag(p['s5_c_re'][l].transpose(0, 2, 1)),
                            -_block_diag(p['s5_c_im'][l].transpose(0, 2, 1))], axis=0).astype(BF16)
    gate_bias = jnp.zeros((1, LANES), F32)
    gate_bias = gate_bias.at[0, IG_LANE:IG_LANE + MLSTM_HEADS].set(p['mlstm_b_i'][l])
    gate_bias = gate_bias.at[0, FG_LANE:FG_LANE + MLSTM_HEADS].set(p['mlstm_b_f'][l])
    return dict(
        norm_g=p['norm_g'][l][:, None, :],
        w_ffn1_in=p['w_ffn1_in'][l].astype(BF16), w_ffn1_out=p['w_ffn1_out'][l].astype(BF16),
        w_ffn2_in=p['w_ffn2_in'][l].astype(BF16), w_ffn2_out=p['w_ffn2_out'][l].astype(BF16),
        wt_q=wt_q, wt_kv=wt_kv, wt_plain=wt_plain,
        s5_a=jnp.stack([a_re, a_im]).reshape(2, S5_ROWS, LANES), s5_b=s5_b, s5_c=s5_c,
        s5_d=p['s5_d'][l][None, :], w_s5_glu=p['w_s5_glu'][l].astype(BF16),
        gate_bias=gate_bias, mlstm_norm_g=p['mlstm_norm_g'][l][None, :],
        w_branch=p['w_branch'][l].astype(BF16), w_out=p['w_out'][l].astype(BF16))


def _rope_tables(pos):
    inv = ROPE_THETA ** (-jnp.arange(HALF, dtype=F32) / HALF)
    ang = pos.astype(F32)[:, None] * inv[None, :]
    return jnp.cos(ang), jnp.sin(ang)


def _row_tables(cos, sin, rotated):
    one, zero = jnp.ones_like(cos), jnp.zeros_like(sin)
    c = jnp.concatenate([x for r in rotated for x in ((cos, cos) if r else (one, one))], axis=1)
    s = jnp.concatenate([x for r in rotated for x in ((-sin, sin) if r else (zero, zero))], axis=1)
    return c, s


def _ffn(x, ada, sub, g, w_in, w_out, t):
    h = modnorm(x, g, ada, sub, t, BF16)
    return mm_resid(swiglu_in(h, w_in), w_out, x, ada, 3 * sub + 2, 0.5, t)


def _s5_mixer(zp, lw, h0, batch, t):
    xs = mm(zp, lw['s5_b'])
    s, fin = s5_scan(xs.reshape(batch, t, 2, S5_ROWS, LANES), lw['s5_a'], h0)
    o = s5_out(s.reshape(batch * t, 2 * S5_N), lw['s5_c'], zp, lw['s5_d'], lw['w_s5_glu'])
    return o, fin[:, 0].reshape(batch, S5_GROUPS, S5_STATE), fin[:, 1].reshape(batch, S5_GROUPS, S5_STATE)


def _finish_layer(x, ada, lw, zp, branches, t):
    merged = merge_branches(branches, lw['w_branch'], zp, ZP_GBR)
    x = mm_resid(merged, lw['w_out'], x, ada, 5, 1.0, t)
    return _ffn(x, ada, 2, lw['norm_g'][2], lw['w_ffn2_in'], lw['w_ffn2_out'], t)


def _layer_prompt(x, ada, lw, batch, t):
    m = batch * t
    x = _ffn(x, ada, 0, lw['norm_g'][0], lw['w_ffn1_in'], lw['w_ffn1_out'], t)
    h = modnorm(x, lw['norm_g'][1], ada, 1, t, BF16)
    cos, sin = _rope_tables(jnp.arange(t, dtype=jnp.int32))
    cq, sq = _row_tables(cos, sin, (True, True))
    qr = proj_rope_rows(h, lw['wt_q'], cq, sq, t, BF16)
    qh = qr.reshape(m, Q_HEADS, HEAD_DIM).transpose(1, 0, 2)
    kv_t, ki_t, nsa_t, win_t, kv_tb, ki_tb, nsa_tb, win_tb = proj_cols(h, lw['wt_kv'], cos.T, sin.T, batch, t)
    zp = mm_nt(h, lw['wt_plain'])
    o_s5, s5_re, s5_im = _s5_mixer(zp, lw, jnp.zeros((batch, 2, S5_ROWS, LANES), F32), batch, t)
    o_dsa = dsa_prompt(qh, zp, ZP_SMALL, ki_tb, kv_tb, batch, t)
    o_nsa = nsa_prompt(qh, zp, ZP_SMALL, nsa_pool(nsa_t, batch, t), nsa_tb, win_tb, batch, t)
    chunk = math.gcd(t, 256)
    o_ml, mc, mn, mm_ = mlstm(zp, lw['gate_bias'], lw['mlstm_norm_g'],
                              jnp.zeros((batch, MLSTM_HEADS, MLSTM_DK, MLSTM_DK), F32),
                              jnp.zeros((batch, 8, LANES), F32), jnp.zeros((batch, 8, LANES), F32),
                              batch, t, chunk, (ZP_Q, ZP_K, ZP_V, ZP_SMALL, ZP_OM))
    x = _finish_layer(x, ada, lw, zp, [o_s5, o_dsa, o_nsa, o_ml], t)
    tokens_last = lambda a, shape: jnp.moveaxis(a.reshape((batch,) + shape + (a.shape[-1],)), -1, 1)
    wb = min(WINDOW, t)
    state = (tokens_last(kv_t, (2, DSA_HEADS, HEAD_DIM)), jnp.swapaxes(ki_t, 1, 2),
             tokens_last(nsa_t, (4, NSA_KV_HEADS, HEAD_DIM)),
             tokens_last(win_t[:, :, t - wb:], (2, NSA_KV_HEADS, HEAD_DIM)),
             mc, mn[:, :MLSTM_HEADS], mm_[:, :MLSTM_HEADS, 0], s5_re, s5_im)
    return x, state


def _layer_sample(x, ada, lw, layer, batch, t, page_table, views, past):
    m = batch * t
    kidx_view, kv_view, nsa_view, win_view = views
    past_len = page_table.shape[1] * LANES
    x = _ffn(x, ada, 0, lw['norm_g'][0], lw['w_ffn1_in'], lw['w_ffn1_out'], t)
    h = modnorm(x, lw['norm_g'][1], ada, 1, t, BF16)
    cos, sin = _rope_tables(past_len + jnp.arange(t, dtype=jnp.int32))
    cq, sq = _row_tables(cos, sin, (True, True))
    qr = proj_rope_rows(h, lw['wt_q'], cq, sq, t, BF16)
    qh = qr.reshape(m, Q_HEADS, HEAD_DIM).transpose(1, 0, 2)
    ckv, skv = _row_tables(cos, sin, KV_COLS_ROPE + (False,))
    wt_kv = jnp.concatenate([lw['wt_kv'], jnp.zeros((HEAD_DIM, D_MODEL), BF16)])
    kvr = proj_rope_rows(h, wt_kv, ckv, skv, t, F32)
    kv_rows, ki_rows = kvr[:, :2 * MIX_WIDTH], kvr[:, 2 * MIX_WIDTH:2 * MIX_WIDTH + IDX_DIM]
    nsa_rows = kvr[:, 2 * MIX_WIDTH + IDX_DIM:3 * MIX_WIDTH + IDX_DIM]
    win_rows = kvr[:, 3 * MIX_WIDTH + IDX_DIM:KV_COLS]
    zp = mm_nt(h, lw['wt_plain'])
    h0 = jnp.stack([past['s5_re'][layer], past['s5_im'][layer]], axis=1).reshape(batch, 2, S5_ROWS, LANES)
    o_s5, s5_re, s5_im = _s5_mixer(zp, lw, h0, batch, t)

    def new_rows(a):
        a = a.reshape(batch, t, a.shape[-1])
        return jnp.pad(a, ((0, 0), (0, LANES - t), (0, 0))).astype(BF16)

    scores = dsa_sample_scores(page_table, qh, zp, new_rows(ki_rows), kidx_view, layer)
    q_dsa = qr[:, :MIX_WIDTH].reshape(batch, t, DSA_HEADS, HEAD_DIM)
    qbd = jnp.einsum('bqhd,hg->bhqgd', q_dsa, jnp.eye(DSA_HEADS, dtype=BF16)).reshape(batch, DSA_HEADS * t, MIX_WIDTH)
    o_dsa = dsa_sample_attn(page_table, scores, qbd, new_rows(kv_rows), kv_view, layer)
    ocmp, sel = nsa_sample_cmp(page_table, qh, nsa_view, layer)
    o_nsa = nsa_sample_sel(page_table, qh, sel, ocmp, zp, new_rows(nsa_rows), win_view, new_rows(win_rows),
                           nsa_view, layer)
    n0 = jnp.pad(past['mlstm_n'][layer], ((0, 0), (0, 8 - MLSTM_HEADS), (0, 0)))
    m0 = jnp.pad(jnp.broadcast_to(past['mlstm_m'][layer][:, :, None], (batch, MLSTM_HEADS, LANES)),
                 ((0, 0), (0, 8 - MLSTM_HEADS), (0, 0)))
    chunk = 64 if t % 64 == 0 else t
    o_ml, mc, mn, mm_ = mlstm(zp, lw['gate_bias'], lw['mlstm_norm_g'], past['mlstm_c'][layer], n0, m0,
                              batch, t, chunk, (ZP_Q, ZP_K, ZP_V, ZP_SMALL, ZP_OM))
    x = _finish_layer(x, ada, lw, zp, [o_s5, o_dsa, o_nsa, o_ml], t)
    win_buf = past['nsa_win'][layer]
    wb = win_buf.shape[1]
    win_all = jnp.concatenate([win_buf, win_rows.reshape(batch, t, 2, NSA_KV_HEADS, HEAD_DIM)], axis=1)
    state = (kv_rows.reshape(batch, t, 2, DSA_HEADS, HEAD_DIM), ki_rows.reshape(batch, t, IDX_DIM),
             nsa_rows.reshape(batch, t, 4, NSA_KV_HEADS, HEAD_DIM), win_all[:, win_all.shape[1] - wb:],
             mc, mn[:, :MLSTM_HEADS], mm_[:, :MLSTM_HEADS, 0], s5_re, s5_im)
    return x, state


def kernel(x_prompt, x_sample, cache_dsa_kv, cache_dsa_kidx, cache_nsa_kv, cache_nsa_win, state_mlstm_c,
           state_mlstm_n, state_mlstm_m, state_s5_re, state_s5_im, page_table, c_prompt, c_sample, w_ada, b_ada,
           norm_g, w_ffn1_in, w_ffn1_out, w_ffn2_in, w_ffn2_out, w_in, s5_lam_re, s5_lam_im, s5_log_dt, s5_b_re,
           s5_b_im, s5_c_re, s5_c_im, s5_d, w_s5_glu, mlstm_b_i, mlstm_b_f, mlstm_norm_g, w_branch, w_out,
           final_norm_g):
    params = dict(norm_g=norm_g, w_ffn1_in=w_ffn1_in, w_ffn1_out=w_ffn1_out, w_ffn2_in=w_ffn2_in,
                  w_ffn2_out=w_ffn2_out, w_in=w_in, s5_lam_re=s5_lam_re, s5_lam_im=s5_lam_im, s5_log_dt=s5_log_dt,
                  s5_b_re=s5_b_re, s5_b_im=s5_b_im, s5_c_re=s5_c_re, s5_c_im=s5_c_im, s5_d=s5_d,
                  w_s5_glu=w_s5_glu, mlstm_b_i=mlstm_b_i, mlstm_b_f=mlstm_b_f, mlstm_norm_g=mlstm_norm_g,
                  w_branch=w_branch, w_out=w_out)
    bp, tp, d = x_prompt.shape
    bs, ts, _ = x_sample.shape
    depth = w_ada.shape[0]
    views = (jnp.transpose(cache_dsa_kidx, (0, 1, 3, 2)), jnp.transpose(cache_dsa_kv, (0, 1, 3, 4, 5, 2)),
             jnp.transpose(cache_nsa_kv, (0, 1, 3, 4, 5, 2)), jnp.transpose(cache_nsa_win, (0, 1, 3, 4, 5, 2)))
    past = dict(nsa_win=cache_nsa_win, mlstm_c=state_mlstm_c, mlstm_n=state_mlstm_n, mlstm_m=state_mlstm_m,
                s5_re=state_s5_re, s5_im=state_s5_im)
    ada_rows = -(-(bp + bs) // 8) * 8
    c_all = jnp.pad(jnp.concatenate([c_prompt, c_sample]), ((0, ada_rows - bp - bs), (0, 0)))
    xp = x_prompt.reshape(bp * tp, d)
    xs = x_sample.reshape(bs * ts, d)
    st_p, st_s = [], []
    for l in range(depth):
        lw = _prep_layer(l, params)
        ada = ada_project(c_all, w_ada[l].astype(BF16), b_ada[l][None]).reshape(ada_rows, 9, d)
        xp, sp = _layer_prompt(xp, ada[:bp], lw, bp, tp)
        xs, ss = _layer_sample(xs, ada[bp:bp + bs], lw, l, bs, ts, page_table, views, past)
        st_p.append(sp)
        st_s.append(ss)
    g = final_norm_g[None, :]
    y_p = modnorm(xp, g, ada[:bp], None, tp, F32).reshape(bp, tp, d)
    y_s = modnorm(xs, g, ada[bp:bp + bs], None, ts, F32).reshape(bs, ts, d)
    outs = [y_p, y_s]
    for i in range(9):
        outs.append(jnp.stack([s[i] for s in st_p]))
        outs.append(jnp.stack([s[i] for s in st_s]))
    return tuple(outs)
```

```python
import functools
import math

import jax
import jax.numpy as jnp
import numpy as np
from jax import lax
from jax.experimental import pallas as pl
from jax.experimental.pallas import tpu as pltpu

F32 = jnp.float32
BF16 = jnp.bfloat16

D_MODEL = 2048
MIX_WIDTH = D_MODEL // 4
HEAD_DIM = 64
HALF = HEAD_DIM // 2
S5_GROUP = 16
S5_GROUPS = MIX_WIDTH // S5_GROUP
S5_STATE = 64
S5_N = S5_GROUPS * S5_STATE
DSA_HEADS = MIX_WIDTH // HEAD_DIM
IDX_HEADS = 4
IDX_DIM = 64
DSA_TOPK = 256
NSA_HEADS = MIX_WIDTH // HEAD_DIM
NSA_KV_HEADS = 2
NSA_REP = NSA_HEADS // NSA_KV_HEADS
CMP_STRIDE = 16
CMP_LEN = 2 * CMP_STRIDE
SEL_BLOCK = 64
SEL_TOPN = 16
CMP_PER_SEL = SEL_BLOCK // CMP_STRIDE
WINDOW = 512
MLSTM_HEADS = 4
MLSTM_DK = MIX_WIDTH // MLSTM_HEADS
D_FF = 2 * D_MODEL
ROPE_THETA = 10000.0
QUERY_BLOCK = 128
RMS_EPS = 1e-6
NEG_INF = -1e30
FORCE_SCORE = 1e4
N_BRANCH = 4

LANES = 128
VMEM_LIMIT = 56 * 1024 * 1024
HIGHEST = lax.Precision.HIGHEST


def _cp(*sem):
    return pltpu.CompilerParams(dimension_semantics=sem, vmem_limit_bytes=VMEM_LIMIT)


def _dot(a, b):
    return jnp.dot(a, b, preferred_element_type=F32)


def _dot_nt(a, b, precision=None):
    return lax.dot_general(a, b, (((1,), (1,)), ((), ())), preferred_element_type=F32,
                           precision=precision)


def _sigmoid(x):
    return 1.0 / (1.0 + jnp.exp(-x))


def _silu(x):
    return x * _sigmoid(x)


def _ada_kernel(c_ref, w_ref, b_ref, o_ref):
    c = c_ref[...]
    o_ref[...] = _dot(_silu(c).astype(BF16), w_ref[...]) + b_ref[...]


def ada_project(c, w, b):
    r, d = c.shape
    n = w.shape[1]
    tn = 2048
    return pl.pallas_call(
        _ada_kernel,
        grid=(n // tn,),
        in_specs=[pl.BlockSpec((r, d), lambda j: (0, 0)),
                  pl.BlockSpec((d, tn), lambda j: (0, j)),
                  pl.BlockSpec((1, tn), lambda j: (0, j))],
        out_specs=pl.BlockSpec((r, tn), lambda j: (0, j)),
        out_shape=jax.ShapeDtypeStruct((r, n), F32),
        compiler_params=_cp("parallel"),
        name="ada_project",
    )(c, w, b)


def _modnorm_kernel(x_ref, g_ref, ada_ref, o_ref, *, sub):
    x = x_ref[...]
    y = x * lax.rsqrt(jnp.mean(x * x, axis=-1, keepdims=True) + RMS_EPS)
    y = y * g_ref[...]
    if sub is not None:
        shift = ada_ref[3 * sub:3 * sub + 1, :]
        scale = ada_ref[3 * sub + 1:3 * sub + 2, :]
        y = y * (1.0 + scale) + shift
    o_ref[...] = y.astype(o_ref.dtype)


def modnorm(x, g, ada, sub, rows_per_batch, out_dtype):
    m, d = x.shape
    tm = min(512, rows_per_batch)
    nb = rows_per_batch // tm
    return pl.pallas_call(
        functools.partial(_modnorm_kernel, sub=sub),
        grid=(m // tm,),
        in_specs=[pl.BlockSpec((tm, d), lambda i: (i, 0)),
                  pl.BlockSpec((1, d), lambda i: (0, 0)),
                  pl.BlockSpec((None, 9, d), lambda i: (i // nb, 0, 0))],
        out_specs=pl.BlockSpec((tm, d), lambda i: (i, 0)),
        out_shape=jax.ShapeDtypeStruct((m, d), out_dtype),
        compiler_params=_cp("parallel"),
        name="modnorm",
    )(x, g, ada)


def _swiglu_in_kernel(h_ref, wa_ref, wg_ref, o_ref):
    h = h_ref[...]
    a = _dot(h, wa_ref[...])
    g = _dot(h, wg_ref[...])
    o_ref[...] = (_silu(a) * g).astype(o_ref.dtype)


def swiglu_in(h, w):
    m, d = h.shape
    f = w.shape[1] // 2
    tm = min(1024, m)
    tn = 512
    nj = f // tn
    return pl.pallas_call(
        _swiglu_in_kernel,
        grid=(m // tm, nj),
        in_specs=[pl.BlockSpec((tm, d), lambda i, j: (i, 0)),
                  pl.BlockSpec((d, tn), lambda i, j: (0, j)),
                  pl.BlockSpec((d, tn), lambda i, j: (0, j + nj))],
        out_specs=pl.BlockSpec((tm, tn), lambda i, j: (i, j)),
        out_shape=jax.ShapeDtypeStruct((m, f), BF16),
        compiler_params=_cp("parallel", "parallel"),
        name="swiglu_in",
    )(h, w, w)


def _mm_resid_kernel(a_ref, w_ref, x_ref, ada_ref, o_ref, *, gate_row, coef):
    y = _dot(a_ref[...], w_ref[...])
    gate = ada_ref[gate_row:gate_row + 1, :]
    o_ref[...] = x_ref[...] + (coef * gate) * y


def mm_resid(a, w, x, ada, gate_row, coef, rows_per_batch):
    m, k = a.shape
    n = w.shape[1]
    tm = min(1024, rows_per_batch)
    tn = 512
    nb = rows_per_batch // tm
    return pl.pallas_call(
        functools.partial(_mm_resid_kernel, gate_row=gate_row, coef=coef),
        grid=(m // tm, n // tn),
        in_specs=[pl.BlockSpec((tm, k), lambda i, j: (i, 0)),
                  pl.BlockSpec((k, tn), lambda i, j: (0, j)),
                  pl.BlockSpec((tm, tn), lambda i, j: (i, j)),
                  pl.BlockSpec((None, 9, tn), lambda i, j: (i // nb, 0, j))],
        out_specs=pl.BlockSpec((tm, tn), lambda i, j: (i, j)),
        out_shape=jax.ShapeDtypeStruct((m, n), F32),
        compiler_params=_cp("parallel", "parallel"),
        name="mm_resid",
    )(a, w, x, ada)


def _mm_kernel(a_ref, w_ref, o_ref):
    o_ref[...] = _dot(a_ref[...].astype(BF16), w_ref[...]).astype(o_ref.dtype)


def mm(a, w, out_dtype=F32, tn=2048):
    m = a.shape[0]
    k, n = w.shape
    tm = min(1024, m)
    return pl.pallas_call(
        _mm_kernel,
        grid=(m // tm, n // tn),
        in_specs=[pl.BlockSpec((tm, k), lambda i, j: (i, 0)),
                  pl.BlockSpec((k, tn), lambda i, j: (0, j))],
        out_specs=pl.BlockSpec((tm, tn), lambda i, j: (i, j)),
        out_shape=jax.ShapeDtypeStruct((m, n), out_dtype),
        compiler_params=_cp("parallel", "parallel"),
        name="mm",
    )(a, w)


def _mm_nt_kernel(a_ref, wt_ref, o_ref):
    o_ref[...] = _dot_nt(a_ref[...], wt_ref[...]).astype(o_ref.dtype)


def mm_nt(a, wt, tn=1024):
    m, k = a.shape
    n = wt.shape[0]
    tm = min(1024, m)
    return pl.pallas_call(
        _mm_nt_kernel,
        grid=(m // tm, n // tn),
        in_specs=[pl.BlockSpec((tm, k), lambda i, j: (i, 0)),
                  pl.BlockSpec((tn, k), lambda i, j: (j, 0))],
        out_specs=pl.BlockSpec((tm, tn), lambda i, j: (i, j)),
        out_shape=jax.ShapeDtypeStruct((m, n), F32),
        compiler_params=_cp("parallel", "parallel"),
        name="mm_nt",
    )(a, wt)


def _rope_rows(x, cos, sin_signed):
    w = x.shape[1]
    lane = lax.broadcasted_iota(jnp.int32, x.shape, 1)
    fwd = pltpu.roll(x, w - HALF, axis=1)
    bwd = pltpu.roll(x, HALF, axis=1)
    swapped = jnp.where((lane % HEAD_DIM) < HALF, fwd, bwd)
    return x * cos + swapped * sin_signed


def _proj_rope_rows_kernel(h_ref, wt_ref, cos_ref, sin_ref, o_ref, *, periodic):
    z = _dot_nt(h_ref[...], wt_ref[...])
    if periodic:
        reps = z.shape[1] // LANES
        cos = jnp.concatenate([cos_ref[...]] * reps, axis=1)
        sin = jnp.concatenate([sin_ref[...]] * reps, axis=1)
    else:
        cos, sin = cos_ref[...], sin_ref[...]
    o_ref[...] = _rope_rows(z, cos, sin).astype(o_ref.dtype)


def proj_rope_rows(h, wt, cos, sin, rows_per_batch, out_dtype):
    m, k = h.shape
    n = wt.shape[0]
    tm = min(512, rows_per_batch)
    nb = rows_per_batch // tm
    tw = cos.shape[1]
    return pl.pallas_call(
        functools.partial(_proj_rope_rows_kernel, periodic=(tw != n)),
        grid=(m // tm,),
        in_specs=[pl.BlockSpec((tm, k), lambda i: (i, 0)),
                  pl.BlockSpec((n, k), lambda i: (0, 0)),
                  pl.BlockSpec((tm, tw), lambda i: (i % nb, 0)),
                  pl.BlockSpec((tm, tw), lambda i: (i % nb, 0))],
        out_specs=pl.BlockSpec((tm, n), lambda i: (i, 0)),
        out_shape=jax.ShapeDtypeStruct((m, n), out_dtype),
        compiler_params=_cp("parallel"),
        name="proj_rope_rows",
    )(h, wt, cos, sin)


KV_COLS_ROPE = (True,) * 8 + (False,) * 8 + (True,) + (True, True, False, False) * 3
KV_COLS = HEAD_DIM * len(KV_COLS_ROPE)


KROW_HEADS = tuple(range(DSA_HEADS)) + (16, 21, 22)
KROW_IDX, KROW_SEL = DSA_HEADS, DSA_HEADS + 1


def _proj_cols_kernel(h_ref, wt_ref, cos_ref, sin_ref, kv_ref, ki_ref, nsa_ref, win_ref,
                      kvb_ref, kib_ref, nsab_ref, winb_ref, krow_ref):
    zt = _dot_nt(wt_ref[...], h_ref[...])
    cos, sin = cos_ref[...], sin_ref[...]
    parts = []
    for r, rot in enumerate(KV_COLS_ROPE):
        x1 = zt[r * HEAD_DIM:r * HEAD_DIM + HALF]
        x2 = zt[r * HEAD_DIM + HALF:(r + 1) * HEAD_DIM]
        if rot:
            parts += [x1 * cos - x2 * sin, x1 * sin + x2 * cos]
        else:
            parts += [x1, x2]
    out = jnp.concatenate(parts, axis=0)
    bounds = (0, 1024, 1088, 1600, 1856)
    for lo, hi, f_ref, b_ref in zip(bounds[:-1], bounds[1:], (kv_ref, ki_ref, nsa_ref, win_ref),
                                    (kvb_ref, kib_ref, nsab_ref, winb_ref)):
        f_ref[...] = out[lo:hi]
        b_ref[...] = out[lo:hi].astype(BF16)
    for j, r in enumerate(KROW_HEADS):
        krow_ref[j] = out[r * HEAD_DIM:(r + 1) * HEAD_DIM].T.astype(BF16)


def proj_cols(h, wt, cos_t, sin_t, batch, t):
    k = h.shape[1]
    tm = min(512, t)
    nt = t // tm
    widths = (1024, 64, 512, 256)
    out_shape = ([jax.ShapeDtypeStruct((batch, w, t), F32) for w in widths]
                 + [jax.ShapeDtypeStruct((batch, w, t), BF16) for w in widths]
                 + [jax.ShapeDtypeStruct((batch, len(KROW_HEADS), t, HEAD_DIM), BF16)])
    out_specs = ([pl.BlockSpec((None, w, tm), lambda b, i: (b, 0, i)) for w in widths] * 2
                 + [pl.BlockSpec((None, len(KROW_HEADS), tm, HEAD_DIM), lambda b, i: (b, 0, i, 0))])
    return pl.pallas_call(
        _proj_cols_kernel,
        grid=(batch, nt),
        in_specs=[pl.BlockSpec((tm, k), lambda b, i: (b * nt + i, 0)),
                  pl.BlockSpec((KV_COLS, k), lambda b, i: (0, 0)),
                  pl.BlockSpec((HALF, tm), lambda b, i: (0, i)),
                  pl.BlockSpec((HALF, tm), lambda b, i: (0, i))],
        out_specs=out_specs,
        out_shape=out_shape,
        compiler_params=_cp("parallel", "parallel"),
        name="proj_cols",
    )(h, wt, cos_t, sin_t)


S5_ROWS = S5_N // LANES


def _s5_scan_kernel(x_ref, a_ref, h0_ref, s_ref, fin_ref, carry_ref, *, tc):
    j = pl.program_id(1)

    @pl.when(j == 0)
    def _():
        carry_ref[...] = h0_ref[...]

    ar, ai = a_ref[0], a_ref[1]

    def step(t, carry):
        hr, hi = carry
        nr = ar * hr - ai * hi + x_ref[t, 0]
        ni = ar * hi + ai * hr + x_ref[t, 1]
        s_ref[t, 0] = nr
        s_ref[t, 1] = ni
        return nr, ni

    hr, hi = lax.fori_loop(0, tc, step, (carry_ref[0], carry_ref[1]), unroll=8)
    carry_ref[0] = hr
    carry_ref[1] = hi

    @pl.when(j == pl.num_programs(1) - 1)
    def _():
        fin_ref[0] = hr
        fin_ref[1] = hi


def s5_scan(x, a, h0):
    b, t = x.shape[:2]
    tc = min(256, t)
    blk = (None, tc, 2, S5_ROWS, LANES)
    st = (None, 2, S5_ROWS, LANES)
    return pl.pallas_call(
        functools.partial(_s5_scan_kernel, tc=tc),
        grid=(b, t // tc),
        in_specs=[pl.BlockSpec(blk, lambda i, j: (i, j, 0, 0, 0)),
                  pl.BlockSpec((2, S5_ROWS, LANES), lambda i, j: (0, 0, 0)),
                  pl.BlockSpec(st, lambda i, j: (i, 0, 0, 0))],
        out_specs=[pl.BlockSpec(blk, lambda i, j: (i, j, 0, 0, 0)),
                   pl.BlockSpec(st, lambda i, j: (i, 0, 0, 0))],
        out_shape=[jax.ShapeDtypeStruct(x.shape, F32), jax.ShapeDtypeStruct(h0.shape, F32)],
        scratch_shapes=[pltpu.VMEM((2, S5_ROWS, LANES), F32)],
        compiler_params=_cp("parallel", "arbitrary"),
        name="s5_scan",
    )(x, a, h0)


def _gelu_tanh(x):
    return 0.5 * x * (1.0 + jnp.tanh(math.sqrt(2.0 / math.pi) * (x + 0.044715 * (x * x * x))))


def _s5_out_kernel(s_ref, c_ref, u_ref, d_ref, w_ref, o_ref):
    y = _dot(s_ref[...].astype(BF16), c_ref[...]) + d_ref[...] * u_ref[...]
    y = _gelu_tanh(y)
    o_ref[...] = (y * _sigmoid(_dot(y.astype(BF16), w_ref[...]))).astype(o_ref.dtype)


def s5_out(s, cmat, zp, d, wglu):
    m = s.shape[0]
    tm = min(512, m)
    return pl.pallas_call(
        _s5_out_kernel,
        grid=(m // tm,),
        in_specs=[pl.BlockSpec((tm, 2 * S5_N), lambda i: (i, 0)),
                  pl.BlockSpec((2 * S5_N, MIX_WIDTH), lambda i: (0, 0)),
                  pl.BlockSpec((tm, MIX_WIDTH), lambda i: (i, 0)),
                  pl.BlockSpec((1, MIX_WIDTH), lambda i: (0, 0)),
                  pl.BlockSpec((MIX_WIDTH, MIX_WIDTH), lambda i: (0, 0))],
        out_specs=pl.BlockSpec((tm, MIX_WIDTH), lambda i: (i, 0)),
        out_shape=jax.ShapeDtypeStruct((m, MIX_WIDTH), BF16),
        compiler_params=_cp("parallel"),
        name="s5_out",
    )(s, cmat, zp, d, wglu)


WI_LANE = 0
GN_LANE = WI_LANE + IDX_HEADS
IG_LANE = GN_LANE + 3 * NSA_HEADS
FG_LANE = IG_LANE + MLSTM_HEADS


def _mlstm_kernel(q_ref, k_ref, v_ref, gates_ref, om_ref, bias_ref, g_ref, c0_ref, n0_ref, m0_ref,
                  o_ref, c_out, n_out, m_out, c_s, n_s, m_s, *, chunk):
    j = pl.program_id(1)

    @pl.when(j == 0)
    def _():
        c_s[...] = c0_ref[...]
        n_s[...] = n0_ref[...]
        m_s[...] = m0_ref[...]

    gates = gates_ref[...] + bias_ref[...]
    lane = lax.broadcasted_iota(jnp.int32, gates.shape, 1)
    is_f = (lane >= FG_LANE) & (lane < FG_LANE + MLSTM_HEADS)
    logsig = jnp.minimum(gates, 0.0) - jnp.log(1.0 + jnp.exp(-jnp.abs(gates)))
    gl = jnp.where(is_f, logsig, gates)
    row = lax.broadcasted_iota(jnp.int32, (chunk, chunk), 0)
    col = lax.broadcasted_iota(jnp.int32, (chunk, chunk), 1)
    causal = col <= row
    cum = jnp.dot(causal.astype(F32), gl, preferred_element_type=F32, precision=HIGHEST)
    gl_t = gl.T
    cum_t = cum.T
    scale = 1.0 / math.sqrt(MLSTM_DK)
    for h in range(MLSTM_HEADS):
        sl = slice(h * MLSTM_DK, (h + 1) * MLSTM_DK)
        q = q_ref[:, sl]
        k = k_ref[:, sl] * scale
        v = v_ref[:, sl]
        qb, kb, vb = q.astype(BF16), k.astype(BF16), v.astype(BF16)
        c = c_s[h]
        n = n_s[h:h + 1, :]
        m = m_s[h:h + 1, 0:1]
        cum_c = cum[:, FG_LANE + h:FG_LANE + h + 1]
        cum_r = cum_t[FG_LANE + h:FG_LANE + h + 1, :]
        ig_c = gl[:, IG_LANE + h:IG_LANE + h + 1]
        ig_r = gl_t[IG_LANE + h:IG_LANE + h + 1, :]
        logd = jnp.where(causal, cum_c - cum_r + ig_r, NEG_INF)
        log_state = cum_c + m
        m_t = jnp.maximum(log_state, jnp.max(logd, axis=1, keepdims=True))
        w_in = jnp.exp(logd - m_t)
        w_st = jnp.exp(log_state - m_t)
        s = _dot_nt(qb, kb) * w_in
        num = w_st * _dot(qb, c.astype(BF16)) + _dot(s.astype(BF16), vb)
        den = w_st * jnp.sum(q * n, axis=1, keepdims=True) + jnp.sum(s, axis=1, keepdims=True)
        hh = num / jnp.maximum(jnp.abs(den), jnp.exp(-m_t))
        total = cum_r[:, chunk - 1:chunk]
        m_new = jnp.maximum(total + m, jnp.max(total - cum_r + ig_r, axis=1, keepdims=True))
        a = jnp.exp(total + m - m_new)
        ws_c = jnp.exp(total - cum_c + ig_c - m_new)
        kw = k * ws_c
        c_s[h] = a * c + lax.dot_general(kw.astype(BF16), vb, (((0,), (0,)), ((), ())),
                                         preferred_element_type=F32)
        n_s[h:h + 1, :] = a * n + jnp.sum(kw, axis=0, keepdims=True)
        m_s[h:h + 1, :] = jnp.broadcast_to(m_new, (1, LANES))
        hn = hh * lax.rsqrt(jnp.mean(hh * hh, axis=1, keepdims=True) + RMS_EPS)
        o_ref[:, sl] = (hn * g_ref[:, sl] * _sigmoid(om_ref[:, sl])).astype(o_ref.dtype)

    @pl.when(j == pl.num_programs(1) - 1)
    def _():
        c_out[...] = c_s[...]
        n_out[...] = n_s[...]
        m_out[...] = m_s[...]


def mlstm(zp, bias, norm_g, c0, n0, m0, batch, t, chunk, cols):
    nc = t // chunk
    q_col, k_col, v_col, small_col, om_col = cols
    wide = lambda col: pl.BlockSpec((chunk, MIX_WIDTH), lambda b, j: (b * nc + j, col // MIX_WIDTH))
    st = lambda shape: pl.BlockSpec((None,) + shape, lambda b, j: (b,) + (0,) * len(shape))
    return pl.pallas_call(
        functools.partial(_mlstm_kernel, chunk=chunk),
        grid=(batch, nc),
        in_specs=[wide(q_col), wide(k_col), wide(v_col),
                  pl.BlockSpec((chunk, LANES), lambda b, j: (b * nc + j, small_col // LANES)),
                  wide(om_col),
                  pl.BlockSpec((1, LANES), lambda b, j: (0, 0)),
                  pl.BlockSpec((1, MIX_WIDTH), lambda b, j: (0, 0)),
                  st((MLSTM_HEADS, MLSTM_DK, MLSTM_DK)), st((8, LANES)), st((8, LANES))],
        out_specs=[pl.BlockSpec((chunk, MIX_WIDTH), lambda b, j: (b * nc + j, 0)),
                   st((MLSTM_HEADS, MLSTM_DK, MLSTM_DK)), st((8, LANES)), st((8, LANES))],
        out_shape=[jax.ShapeDtypeStruct((batch * t, MIX_WIDTH), BF16),
                   jax.ShapeDtypeStruct((batch, MLSTM_HEADS, MLSTM_DK, MLSTM_DK), F32),
                   jax.ShapeDtypeStruct((batch, 8, LANES), F32),
                   jax.ShapeDtypeStruct((batch, 8, LANES), F32)],
        scratch_shapes=[pltpu.VMEM((MLSTM_HEADS, MLSTM_DK, MLSTM_DK), F32),
                        pltpu.VMEM((8, LANES), F32), pltpu.VMEM((8, LANES), F32)],
        compiler_params=_cp("parallel", "arbitrary"),
        name="mlstm",
    )(zp, zp, zp, zp, zp, bias, norm_g, c0, n0, m0)


INT_MIN = -2 ** 31


def _count(mask):
    return jnp.sum(mask.astype(F32), axis=1, keepdims=True)


def _sort_key(score):
    bits = pltpu.bitcast(score, jnp.int32)
    key = bits ^ ((bits >> 31) & 0x7FFFFFFF)
    return jnp.where(score == 0.0, 0, key)


def _count_paged(mask):
    x = mask.astype(F32)
    pages = x.shape[0]
    if pages % 8 == 0:
        x = jnp.sum(x.reshape((pages // 8, 8) + x.shape[1:]), axis=0)
    return jnp.sum(jnp.sum(x, axis=0, keepdims=True), axis=2, keepdims=True)


def _topk_mask(score, k, paged=False):
    count = _count_paged if paged else _count
    if paged:
        idx = (lax.broadcasted_iota(jnp.int32, score.shape, 0) * LANES
               + lax.broadcasted_iota(jnp.int32, score.shape, 2))
        s = score.shape[0] * LANES
    else:
        idx = lax.broadcasted_iota(jnp.int32, score.shape, 1)
        s = score.shape[1]
    key = _sort_key(score)
    kf = float(k)
    prefix = jnp.where(count(key >= 0) >= kf, 0, INT_MIN).astype(jnp.int32)

    def value_bit(b, prefix):
        cand = prefix | lax.shift_left(jnp.int32(1), 30 - b)
        return jnp.where(count(key >= cand) >= kf, cand, prefix)

    thr = lax.fori_loop(0, 31, value_bit, prefix)
    above = key > thr
    tie = key == thr
    need = kf - count(above)
    nbits = max(1, (s - 1).bit_length())

    def index_bit(b, p):
        cand = p | lax.shift_left(jnp.int32(1), nbits - 1 - b)
        return jnp.where(count(tie & (idx < cand)) < need, cand, p)

    last = lax.fori_loop(0, nbits, index_bit, jnp.zeros_like(prefix))
    return above | (tie & (idx <= last))


ATT_SCALE = 1.0 / math.sqrt(HEAD_DIM)


KEY_CHUNK = 512


def _chunk(c, ck):
    return pl.ds(pl.multiple_of(c * ck, ck), ck)


def _fold_rows(x, op):
    r, q = x.shape
    if r % 64 == 0:
        x = op(x.reshape(8, r // 64, 8, q), axis=1)
    else:
        x = x.reshape(r // 8, 8, q)
    return op(x, axis=0)


def _col_reduce(x, op):
    return op(_fold_rows(x, op), axis=0, keepdims=True)


def _flash_work_t(n_heads, qb, ck):
    return [pltpu.VMEM((n_heads, ck, qb), F32), pltpu.VMEM((n_heads, ck, qb), BF16)]


def _flash_chunks_t(q_ref, heads, k_of, v_of, bias_s, m_s, l_s, acc_s, work, n_act, ck):
    s_s, p_s = work
    for h in heads:
        m_s[h:h + 1, :] = jnp.full((1, m_s.shape[1]), NEG_INF, F32)
        l_s[h:h + 1, :] = jnp.zeros((1, l_s.shape[1]), F32)
        acc_s[h] = jnp.zeros(acc_s.shape[1:], F32)

    def chunk_step(c, carry):
        rows = pl.ds(pl.multiple_of(c * ck, ck), ck)
        bias = bias_s[c]
        for j, h in enumerate(heads):
            s_s[j] = _dot_nt(k_of(h, rows), q_ref[h] * ATT_SCALE) + bias
        m_new = [jnp.maximum(m_s[h:h + 1, :], _col_reduce(s_s[j], jnp.max)) for j, h in enumerate(heads)]
        col_sum = []
        for j, h in enumerate(heads):
            p = jnp.exp(s_s[j] - m_new[j])
            p_s[j] = p.astype(BF16)
            col_sum.append(_col_reduce(p, jnp.sum))
        alpha = []
        for j, h in enumerate(heads):
            alpha.append(jnp.exp(m_s[h:h + 1, :] - m_new[j]))
            l_s[h:h + 1, :] = alpha[j] * l_s[h:h + 1, :] + col_sum[j]
            m_s[h:h + 1, :] = m_new[j]
        for j, h in enumerate(heads):
            acc_s[h] = alpha[j] * acc_s[h] + _dot(v_of(h, c), p_s[j])
        return carry

    lax.fori_loop(0, n_act, chunk_step, 0)
    for h in heads:
        acc_s[h] = acc_s[h] / l_s[h:h + 1, :]


def _dsa_prompt_kernel(q_ref, qi_ref, small_ref, ki_ref, k_ref, v_ref, o_ref, key_s, bias_s, acc_s, m_s, l_s,
                       s_s, p_s, *, topk, ck):
    i = pl.program_id(1)
    nch, _, qb = key_s.shape
    t = nch * ck
    n_act = (i * qb + qb - 1) // ck + 1
    tq = i * qb + lax.broadcasted_iota(jnp.int32, (1, qb), 1)
    krow = lax.broadcasted_iota(jnp.int32, (ck, qb), 0)
    kf = float(topk)
    small_t = small_ref[...].T
    wi = [small_t[WI_LANE + h:WI_LANE + h + 1, :] for h in range(IDX_HEADS)]

    def score_chunk(c, carry):
        ki = ki_ref[pl.ds(pl.multiple_of(c * ck, ck), ck), :]
        score = None
        for h in range(IDX_HEADS):
            term = wi[h] * jnp.maximum(_dot_nt(ki, qi_ref[h]), 0.0)
            score = term if score is None else score + term
        key_s[c] = _sort_key(jnp.where(c * ck + krow <= tq, score, NEG_INF))
        return carry

    lax.fori_loop(0, n_act, score_chunk, 0)

    def count_keys(pred):
        def body(c, acc):
            return acc + _fold_rows(pred(key_s[c], c).astype(F32), jnp.sum)
        return jnp.sum(lax.fori_loop(0, n_act, body, jnp.zeros((8, qb), F32)), axis=0, keepdims=True)

    prefix = jnp.where(count_keys(lambda k, c: k >= 0) >= kf, 0, INT_MIN).astype(jnp.int32)

    def value_bit(b, prefix):
        cand = prefix | lax.shift_left(jnp.int32(1), 30 - b)
        return jnp.where(count_keys(lambda k, c: k >= cand) >= kf, cand, prefix)

    thr = lax.fori_loop(0, 31, value_bit, prefix)
    need = kf - count_keys(lambda k, c: k > thr)
    n_tie = count_keys(lambda k, c: k == thr)
    nbits = max(1, (t - 1).bit_length())

    def resolve_ties():
        def index_bit(b, p):
            cand = p | lax.shift_left(jnp.int32(1), nbits - 1 - b)
            below = count_keys(lambda k, c: (k == thr) & (c * ck + krow < cand))
            return jnp.where(below < need, cand, p)
        return lax.fori_loop(0, nbits, index_bit, jnp.zeros((1, qb), jnp.int32))

    last = lax.cond(jnp.max(n_tie - need) > 0.0, resolve_ties, lambda: jnp.full((1, qb), t, jnp.int32))

    def bias_chunk(c, carry):
        k = key_s[c]
        kpos = c * ck + krow
        keep = ((k > thr) | ((k == thr) & (kpos <= last))) & (kpos <= tq)
        bias_s[c] = jnp.where(keep, 0.0, NEG_INF)
        return carry

    lax.fori_loop(0, n_act, bias_chunk, 0)

    _flash_chunks_t(q_ref, range(DSA_HEADS), lambda h, rows: k_ref[h, rows, :],
                    lambda h, c: v_ref[h * HEAD_DIM:(h + 1) * HEAD_DIM, _chunk(c, ck)],
                    bias_s, m_s, l_s, acc_s, (s_s, p_s), n_act, ck)
    o_ref[...] = acc_s[...].reshape(DSA_HEADS * HEAD_DIM, qb).T.astype(o_ref.dtype)


def dsa_prompt(qh, zp, small_col, krow, kv_t, batch, t):
    qb = QUERY_BLOCK
    nq = t // qb
    topk = min(DSA_TOPK, t // 4)
    ck = min(KEY_CHUNK, t)
    nch = t // ck
    assert t % ck == 0 and ck >= topk and ck % qb == 0
    return pl.pallas_call(
        functools.partial(_dsa_prompt_kernel, topk=topk, ck=ck),
        grid=(batch, nq),
        in_specs=[pl.BlockSpec((DSA_HEADS, qb, HEAD_DIM), lambda b, i: (0, b * nq + i, 0)),
                  pl.BlockSpec((IDX_HEADS, qb, HEAD_DIM), lambda b, i: (4, b * nq + i, 0)),
                  pl.BlockSpec((qb, LANES), lambda b, i: (b * nq + i, small_col // LANES)),
                  pl.BlockSpec((None, None, t, HEAD_DIM), lambda b, i: (b, KROW_IDX, 0, 0)),
                  pl.BlockSpec((None, DSA_HEADS, t, HEAD_DIM), lambda b, i: (b, 0, 0, 0)),
                  pl.BlockSpec((None, MIX_WIDTH, t), lambda b, i: (b, 1, 0))],
        out_specs=pl.BlockSpec((qb, MIX_WIDTH), lambda b, i: (b * nq + i, 0)),
        out_shape=jax.ShapeDtypeStruct((batch * t, MIX_WIDTH), BF16),
        scratch_shapes=[pltpu.VMEM((nch, ck, qb), jnp.int32), pltpu.VMEM((nch, ck, qb), F32),
                        pltpu.VMEM((DSA_HEADS, HEAD_DIM, qb), F32),
                        pltpu.VMEM((DSA_HEADS, qb), F32), pltpu.VMEM((DSA_HEADS, qb), F32)]
        + _flash_work_t(DSA_HEADS, qb, ck),
        compiler_params=_cp("parallel", "arbitrary"),
        name="dsa_prompt",
    )(qh, qh, zp, krow, krow, kv_t)


NSA_GROUP_ROWS = NSA_KV_HEADS * HEAD_DIM


def _nsa_pool_kernel(x_ref, o_ref):
    t = x_ref.shape[1]
    nsub = t // CMP_STRIDE
    tok = lax.broadcasted_iota(jnp.int32, (t, nsub), 0)
    c = lax.broadcasted_iota(jnp.int32, (t, nsub), 1)
    inside = (tok >= c * CMP_STRIDE) & (tok < c * CMP_STRIDE + CMP_LEN) & (c < nsub - 1)
    pool = jnp.where(inside, 1.0 / CMP_LEN, 0.0).astype(F32)
    o_ref[...] = jnp.dot(x_ref[...], pool, preferred_element_type=F32, precision=HIGHEST).astype(o_ref.dtype)


def nsa_pool(nsa_t, batch, t):
    rows = 2 * NSA_GROUP_ROWS
    return pl.pallas_call(
        _nsa_pool_kernel,
        grid=(batch,),
        in_specs=[pl.BlockSpec((None, rows, t), lambda b: (b, 0, 0))],
        out_specs=pl.BlockSpec((None, rows, t // CMP_STRIDE), lambda b: (b, 0, 0)),
        out_shape=jax.ShapeDtypeStruct((batch, rows, t // CMP_STRIDE), BF16),
        compiler_params=_cp("parallel"),
        name="nsa_pool",
    )(nsa_t)


def _nsa_select_blocks(pcsum, tq, ns, n_sel, lanes=None):
    rows, nc = pcsum.shape
    lanes = ns if lanes is None else lanes
    c = lax.broadcasted_iota(jnp.int32, (nc, lanes), 0)
    j = lax.broadcasted_iota(jnp.int32, (nc, lanes), 1)
    pool = ((c >= j * CMP_PER_SEL) & (c < (j + 1) * CMP_PER_SEL)).astype(F32)
    imp = jnp.dot(pcsum, pool, preferred_element_type=F32, precision=HIGHEST)
    blk = lax.broadcasted_iota(jnp.int32, (rows, lanes), 1)
    forced = (blk == tq // SEL_BLOCK) | (blk == 0)
    imp = jnp.where(forced, FORCE_SCORE, imp)
    imp = jnp.where((blk * SEL_BLOCK <= tq) & (blk < ns), imp, NEG_INF)
    return _topk_mask(imp, n_sel) & (blk < ns)


def _top_rows(v, n):
    ns = v.shape[0]
    row = lax.broadcasted_iota(jnp.int32, v.shape, 0)
    rank = jnp.zeros(v.shape, F32)
    for i in range(ns):
        vi = v[i:i + 1, :]
        rank = rank + ((vi > v) | ((vi == v) & (row > i))).astype(F32)
    return rank < float(n)


def _nsa_prompt_kernel(q_ref, small_ref, cm_ref, ksel0_ref, ksel1_ref, vsel_ref, w0, w1, w2, w3, w4, o_ref,
                       bias_s, ocmp_s, osel_s, m_s, l_s, s_s, p_s, *, ns, n_sel, ck):
    i = pl.program_id(1)
    qb = bias_s.shape[2]
    n_act = (i * qb + qb - 1) // ck + 1
    tq_row = i * qb + lax.broadcasted_iota(jnp.int32, (ns, qb), 1)
    tq1 = tq_row[0:1, :]
    blk = lax.broadcasted_iota(jnp.int32, (ns, qb), 0)
    ncp = cm_ref.shape[1]
    cidx = lax.broadcasted_iota(jnp.int32, (ncp, qb), 0)
    cvalid = (cidx * CMP_STRIDE + CMP_LEN - 1 <= tq1) & (cidx < ncp - 1)
    for g in range(NSA_KV_HEADS):
        grows = slice(g * HEAD_DIM, (g + 1) * HEAD_DIM)
        vrows = slice(NSA_GROUP_ROWS + g * HEAD_DIM, NSA_GROUP_ROWS + (g + 1) * HEAD_DIM)
        kcm = cm_ref[grows, :].astype(F32).T.astype(BF16)
        vcm = cm_ref[vrows, :]
        pcsum = jnp.zeros((ncp, qb), F32)
        for r in range(NSA_REP):
            h = g * NSA_REP + r
            lc = jnp.where(cvalid, _dot_nt(kcm, q_ref[h] * ATT_SCALE), NEG_INF)
            p = jnp.exp(lc - _col_reduce(lc, jnp.max))
            pc = jnp.where(cvalid, p / _col_reduce(p, jnp.sum), 0.0)
            ocmp_s[h] = _dot(vcm, pc.astype(BF16))
            pcsum = pcsum + pc
        pj = lax.broadcasted_iota(jnp.int32, (ns, ncp), 0)
        pc_ = lax.broadcasted_iota(jnp.int32, (ns, ncp), 1)
        pool = ((pc_ >= pj * CMP_PER_SEL) & (pc_ < (pj + 1) * CMP_PER_SEL)).astype(F32)
        imp = jnp.dot(pool, pcsum, preferred_element_type=F32, precision=HIGHEST)
        imp = jnp.where((blk == tq_row // SEL_BLOCK) | (blk == 0), FORCE_SCORE, imp)
        imp = jnp.where(blk * SEL_BLOCK <= tq_row, imp, NEG_INF)
        sel = _top_rows(imp, n_sel).astype(BF16)

        def bias_chunk(c, carry, sel=sel):
            tk = c * ck + lax.broadcasted_iota(jnp.int32, (ck, ns), 0)
            bj = lax.broadcasted_iota(jnp.int32, (ck, ns), 1)
            expand = (tk // SEL_BLOCK == bj).astype(BF16)
            kpos = c * ck + lax.broadcasted_iota(jnp.int32, (ck, qb), 0)
            keep = (_dot(expand, sel) > 0.5) & (kpos <= tq1)
            bias_s[c] = jnp.where(keep, 0.0, NEG_INF)
            return carry

        lax.fori_loop(0, n_act, bias_chunk, 0)

        ksel_ref = (ksel0_ref, ksel1_ref)[g]
        _flash_chunks_t(q_ref, range(g * NSA_REP, (g + 1) * NSA_REP), lambda h, rows, k=ksel_ref: k[rows, :],
                        lambda h, c, grows=grows: vsel_ref[grows, _chunk(c, ck)],
                        bias_s, m_s, l_s, osel_s, (s_s, p_s), n_act, ck)

    wins = (w0, w1, w2, w3, w4)
    nwin = len(wins)
    kw = jnp.concatenate([w[...] for w in wins], axis=1)
    wpos = (i - (nwin - 1)) * LANES + lax.broadcasted_iota(jnp.int32, (nwin * LANES, qb), 0)
    wok = (wpos >= 0) & (wpos <= tq1) & (wpos > tq1 - WINDOW)
    gates = _sigmoid(small_ref[...]).T
    for g in range(NSA_KV_HEADS):
        kwin = kw[g * HEAD_DIM:(g + 1) * HEAD_DIM].astype(F32).T.astype(BF16)
        vwin = kw[NSA_GROUP_ROWS + g * HEAD_DIM:NSA_GROUP_ROWS + (g + 1) * HEAD_DIM]
        for r in range(NSA_REP):
            h = g * NSA_REP + r
            lw_ = jnp.where(wok, _dot_nt(kwin, q_ref[h] * ATT_SCALE), NEG_INF)
            p = jnp.exp(lw_ - _col_reduce(lw_, jnp.max))
            o_w = _dot(vwin, p.astype(BF16)) / _col_reduce(p, jnp.sum)
            gl = GN_LANE + 3 * h
            ocmp_s[h] = (gates[gl:gl + 1, :] * ocmp_s[h] + gates[gl + 1:gl + 2, :] * osel_s[h]
                         + gates[gl + 2:gl + 3, :] * o_w)
    o_ref[...] = ocmp_s[...].reshape(NSA_HEADS * HEAD_DIM, qb).T.astype(o_ref.dtype)


def nsa_prompt(qh, zp, small_col, cm_t, krow, nsa_t, win_t, batch, t):
    qb = QUERY_BLOCK
    nq = t // qb
    ns = -(-t // SEL_BLOCK)
    nwin = WINDOW // qb + 1
    win_specs = [pl.BlockSpec((None, 2 * NSA_GROUP_ROWS, qb),
                              lambda b, i, j=j: (b, 0, jnp.maximum(i - (nwin - 1) + j, 0)))
                 for j in range(nwin)]
    ck = min(KEY_CHUNK, t)
    assert t % ck == 0 and ck % qb == 0 and t % SEL_BLOCK == 0
    return pl.pallas_call(
        functools.partial(_nsa_prompt_kernel, ns=ns, n_sel=min(SEL_TOPN, ns), ck=ck),
        grid=(batch, nq),
        in_specs=[pl.BlockSpec((NSA_HEADS, qb, HEAD_DIM), lambda b, i: (1, b * nq + i, 0)),
                  pl.BlockSpec((qb, LANES), lambda b, i: (b * nq + i, small_col // LANES)),
                  pl.BlockSpec((None, 2 * NSA_GROUP_ROWS, cm_t.shape[2]), lambda b, i: (b, 0, 0)),
                  pl.BlockSpec((None, None, t, HEAD_DIM), lambda b, i: (b, KROW_SEL, 0, 0)),
                  pl.BlockSpec((None, None, t, HEAD_DIM), lambda b, i: (b, KROW_SEL + 1, 0, 0)),
                  pl.BlockSpec((None, NSA_GROUP_ROWS, t), lambda b, i: (b, 3, 0))] + win_specs,
        out_specs=pl.BlockSpec((qb, MIX_WIDTH), lambda b, i: (b * nq + i, 0)),
        out_shape=jax.ShapeDtypeStruct((batch * t, MIX_WIDTH), BF16),
        scratch_shapes=[pltpu.VMEM((t // ck, ck, qb), F32), pltpu.VMEM((NSA_HEADS, HEAD_DIM, qb), F32),
                        pltpu.VMEM((NSA_HEADS, HEAD_DIM, qb), F32),
                        pltpu.VMEM((NSA_HEADS, qb), F32), pltpu.VMEM((NSA_HEADS, qb), F32)]
        + _flash_work_t(NSA_REP, qb, ck),
        compiler_params=_cp("parallel", "arbitrary"),
        name="nsa_prompt",
    )(qh, zp, cm_t, krow, krow, nsa_t, *([win_t] * nwin))


PAGES_PER_STEP = 8


def _page_specs(block, layer, n_pages, slot=None, per_step=PAGES_PER_STEP):
    def spec(j):
        def index(b, s, pt):
            page = pt[b, jnp.minimum(s * per_step + j, n_pages - 1)]
            lead = (layer, page) if slot is None else (layer, page, slot)
            return lead + (0,) * (len(block) - len(lead))
        return pl.BlockSpec(block, index)
    return [spec(j) for j in range(per_step)]


SCORE_PAGES_PER_STEP = 32


def _dsa_sample_scores_kernel(pt_ref, qi_ref, small_ref, kinew_ref, *rest, n_steps):
    pages, o_ref = rest[:SCORE_PAGES_PER_STEP], rest[SCORE_PAGES_PER_STEP]
    s = pl.program_id(1)
    tnew = qi_ref.shape[1]

    def score(keys_t=None, keys=None):
        acc = None
        for h in range(IDX_HEADS):
            d = _dot(qi_ref[h], keys_t) if keys is None else _dot_nt(qi_ref[h], keys)
            term = small_ref[:, WI_LANE + h:WI_LANE + h + 1] * jnp.maximum(d, 0.0)
            acc = term if acc is None else acc + term
        return acc

    @pl.when(s < n_steps - 1)
    def _():
        for j, page in enumerate(pages):
            o_ref[j] = score(keys_t=page[...].astype(BF16))

    @pl.when(s == n_steps - 1)
    def _():
        sc = score(keys=kinew_ref[...])
        q = lax.broadcasted_iota(jnp.int32, sc.shape, 0)
        k = lax.broadcasted_iota(jnp.int32, sc.shape, 1)
        o_ref[0] = jnp.where((k <= q) & (k < tnew), sc, NEG_INF)
        for j in range(1, SCORE_PAGES_PER_STEP):
            o_ref[j] = jnp.full(sc.shape, NEG_INF, F32)


def dsa_sample_scores(page_table, qh, zp, ki_new, kidx_view, layer):
    batch, n_pages = page_table.shape
    tnew = qh.shape[1] // batch
    assert n_pages % SCORE_PAGES_PER_STEP == 0
    n_steps = n_pages // SCORE_PAGES_PER_STEP + 1
    return pl.pallas_call(
        functools.partial(_dsa_sample_scores_kernel, n_steps=n_steps),
        grid_spec=pltpu.PrefetchScalarGridSpec(
            num_scalar_prefetch=1,
            grid=(batch, n_steps),
            in_specs=[pl.BlockSpec((IDX_HEADS, tnew, HEAD_DIM), lambda b, s, pt: (4, b, 0)),
                      pl.BlockSpec((tnew, LANES), lambda b, s, pt: (b, ZP_SMALL // LANES)),
                      pl.BlockSpec((None, LANES, IDX_DIM), lambda b, s, pt: (b, 0, 0))]
            + _page_specs((None, None, IDX_DIM, LANES), layer, n_pages, per_step=SCORE_PAGES_PER_STEP),
            out_specs=pl.BlockSpec((None, SCORE_PAGES_PER_STEP, tnew, LANES), lambda b, s, pt: (b, s, 0, 0))),
        out_shape=jax.ShapeDtypeStruct((batch, SCORE_PAGES_PER_STEP * n_steps, tnew, LANES), F32),
        compiler_params=_cp("parallel", "arbitrary"),
        name="dsa_sample_scores",
    )(page_table, qh, zp, ki_new, *([kidx_view] * SCORE_PAGES_PER_STEP))


def _online_softmax_step(logits, keep, v_t, m_ref, l_ref, acc_ref, v_rows=None):
    lm = jnp.where(keep, logits, NEG_INF)
    m_old = m_ref[...]
    m_new = jnp.maximum(m_old, jnp.max(lm, axis=1, keepdims=True))
    alpha = jnp.exp(m_old - m_new)
    p = jnp.where(keep, jnp.exp(lm - m_new), 0.0)
    pv = _dot_nt(p.astype(BF16), v_t) if v_rows is None else _dot(p.astype(BF16), v_rows)
    l_ref[...] = alpha * l_ref[...] + jnp.sum(p, axis=1, keepdims=True)
    acc_ref[...] = alpha * acc_ref[...] + pv
    m_ref[...] = m_new


def _dsa_sample_attn_kernel(pt_ref, score_ref, qbd_ref, kvnew_ref, *rest, n_steps, topk):
    pages, o_ref = rest[:PAGES_PER_STEP], rest[PAGES_PER_STEP]
    keep_s, m_s, l_s, acc_s = rest[PAGES_PER_STEP + 1:]
    s = pl.program_id(1)
    tnew = score_ref.shape[1]

    @pl.when(s == 0)
    def _():
        keep_s[...] = _topk_mask(score_ref[...], topk, paged=True).astype(F32)
        m_s[...] = jnp.full(m_s.shape, NEG_INF, F32)
        l_s[...] = jnp.zeros(l_s.shape, F32)
        acc_s[...] = jnp.zeros(acc_s.shape, F32)

    qbd = qbd_ref[...] * ATT_SCALE

    def keep_rows(page):
        return jnp.concatenate([keep_s[page]] * DSA_HEADS, axis=0) > 0.5

    @pl.when(s < n_steps - 1)
    def _():
        k_t = jnp.concatenate([p[0].reshape(MIX_WIDTH, LANES).astype(BF16) for p in pages], axis=1)
        v_t = jnp.concatenate([p[1].reshape(MIX_WIDTH, LANES).astype(BF16) for p in pages], axis=1)
        keep = jnp.concatenate([keep_rows(s * PAGES_PER_STEP + j) for j in range(PAGES_PER_STEP)], axis=1)
        _online_softmax_step(_dot(qbd, k_t), keep, v_t, m_s, l_s, acc_s)

    @pl.when(s == n_steps - 1)
    def _():
        kv = kvnew_ref[...]
        logits = _dot_nt(qbd, kv[:, :MIX_WIDTH])
        q = lax.broadcasted_iota(jnp.int32, logits.shape, 0) % tnew
        k = lax.broadcasted_iota(jnp.int32, logits.shape, 1)
        keep = keep_rows((n_steps - 1) * PAGES_PER_STEP) & (k <= q) & (k < tnew)
        _online_softmax_step(logits, keep, None, m_s, l_s, acc_s, v_rows=kv[:, MIX_WIDTH:])
        out = acc_s[...] / l_s[...]
        for h in range(DSA_HEADS):
            o_ref[:, h * HEAD_DIM:(h + 1) * HEAD_DIM] = (
                out[h * tnew:(h + 1) * tnew, h * HEAD_DIM:(h + 1) * HEAD_DIM].astype(o_ref.dtype))


def dsa_sample_attn(page_table, scores, qbd, kv_new, kv_view, layer):
    batch, n_pages = page_table.shape
    tnew = scores.shape[2]
    n_steps = n_pages // PAGES_PER_STEP + 1
    topk = min(DSA_TOPK, (n_pages * LANES + tnew) // 4)
    rows = DSA_HEADS * tnew
    return pl.pallas_call(
        functools.partial(_dsa_sample_attn_kernel, n_steps=n_steps, topk=topk),
        grid_spec=pltpu.PrefetchScalarGridSpec(
            num_scalar_prefetch=1,
            grid=(batch, n_steps),
            in_specs=[pl.BlockSpec((None,) + scores.shape[1:], lambda b, s, pt: (b, 0, 0, 0)),
                      pl.BlockSpec((None, rows, MIX_WIDTH), lambda b, s, pt: (b, 0, 0)),
                      pl.BlockSpec((None, LANES, 2 * MIX_WIDTH), lambda b, s, pt: (b, 0, 0))]
            + _page_specs((None, None, 2, DSA_HEADS, HEAD_DIM, LANES), layer, n_pages),
            out_specs=pl.BlockSpec((tnew, MIX_WIDTH), lambda b, s, pt: (b, 0)),
            scratch_shapes=[pltpu.VMEM(scores.shape[1:], F32), pltpu.VMEM((rows, 1), F32),
                            pltpu.VMEM((rows, 1), F32), pltpu.VMEM((rows, MIX_WIDTH), F32)]),
        out_shape=jax.ShapeDtypeStruct((batch * tnew, MIX_WIDTH), BF16),
        compiler_params=_cp("parallel", "arbitrary"),
        name="dsa_sample_attn",
    )(page_table, scores, qbd, kv_new, *([kv_view] * PAGES_PER_STEP))


CMP_PAGES_PER_STEP = 16


def _nsa_sample_cmp_kernel(pt_ref, q_ref, *rest, n_steps, past, ns, n_sel):
    pages = rest[:CMP_PAGES_PER_STEP]
    ocmp_ref, sel_ref, sub_s = rest[CMP_PAGES_PER_STEP:]
    s = pl.program_id(1)
    tnew = q_ref.shape[1]
    rows = 2 * NSA_GROUP_ROWS
    n_tok = CMP_PAGES_PER_STEP * LANES
    tok = lax.broadcasted_iota(jnp.int32, (n_tok, LANES), 0)
    col = lax.broadcasted_iota(jnp.int32, (n_tok, LANES), 1)
    pool = jnp.where(col == tok // CMP_STRIDE, 1.0 / CMP_STRIDE, 0.0).astype(BF16)
    x = jnp.concatenate([page[...].reshape(rows, LANES) for page in pages], axis=1)
    x_hi = x.astype(BF16)
    r1 = x - x_hi.astype(F32)
    x_mid = r1.astype(BF16)
    x_lo = (r1 - x_mid.astype(F32)).astype(BF16)
    sub_s[s] = _dot(x_hi, pool) + _dot(x_mid, pool) + _dot(x_lo, pool)

    @pl.when(s == n_steps - 1)
    def _():
        sub_all = jnp.concatenate([sub_s[i] for i in range(n_steps)], axis=1)
        ncp = sub_all.shape[1]
        cm = (0.5 * (sub_all + pltpu.roll(sub_all, ncp - 1, axis=1))).astype(BF16)
        nc = (past + tnew) // CMP_STRIDE - 1
        tq = past + lax.broadcasted_iota(jnp.int32, (tnew, 1), 0)
        cidx = lax.broadcasted_iota(jnp.int32, (tnew, ncp), 1)
        cvalid = (cidx * CMP_STRIDE + CMP_LEN - 1 <= tq) & (cidx < nc)
        for g in range(NSA_KV_HEADS):
            kcm = cm[g * HEAD_DIM:(g + 1) * HEAD_DIM]
            vcm = cm[NSA_GROUP_ROWS + g * HEAD_DIM:NSA_GROUP_ROWS + (g + 1) * HEAD_DIM]
            pcsum = jnp.zeros((tnew, ncp), F32)
            for r in range(NSA_REP):
                h = g * NSA_REP + r
                lc = jnp.where(cvalid, _dot(q_ref[h], kcm) * ATT_SCALE, NEG_INF)
                p = jnp.exp(lc - jnp.max(lc, axis=1, keepdims=True))
                pc = jnp.where(cvalid, p / jnp.sum(p, axis=1, keepdims=True), 0.0)
                ocmp_ref[h] = _dot_nt(pc.astype(BF16), vcm)
                pcsum = pcsum + pc
            sel_ref[g] = _nsa_select_blocks(pcsum, tq, ns, n_sel, lanes=sel_ref.shape[-1]).astype(F32)


def nsa_sample_cmp(page_table, qh, nsa_view, layer):
    batch, n_pages = page_table.shape
    tnew = qh.shape[1] // batch
    past = n_pages * LANES
    assert tnew < CMP_STRIDE and n_pages % CMP_PAGES_PER_STEP == 0
    n_steps = n_pages // CMP_PAGES_PER_STEP
    ns = -(-(past + tnew) // SEL_BLOCK)
    ns_lanes = -(-ns // LANES) * LANES
    return pl.pallas_call(
        functools.partial(_nsa_sample_cmp_kernel, n_steps=n_steps, past=past, ns=ns, n_sel=min(SEL_TOPN, ns)),
        grid_spec=pltpu.PrefetchScalarGridSpec(
            num_scalar_prefetch=1,
            grid=(batch, n_steps),
            in_specs=[pl.BlockSpec((NSA_HEADS, tnew, HEAD_DIM), lambda b, s, pt: (1, b, 0))]
            + _page_specs((None, None, 2, NSA_KV_HEADS, HEAD_DIM, LANES), layer, n_pages, slot=0,
                          per_step=CMP_PAGES_PER_STEP),
            out_specs=[pl.BlockSpec((None, NSA_HEADS, tnew, HEAD_DIM), lambda b, s, pt: (b, 0, 0, 0)),
                       pl.BlockSpec((None, NSA_KV_HEADS, tnew, ns_lanes), lambda b, s, pt: (b, 0, 0, 0))],
            scratch_shapes=[pltpu.VMEM((n_steps, 2 * NSA_GROUP_ROWS, LANES), F32)]),
        out_shape=[jax.ShapeDtypeStruct((batch, NSA_HEADS, tnew, HEAD_DIM), F32),
                   jax.ShapeDtypeStruct((batch, NSA_KV_HEADS, tnew, ns_lanes), F32)],
        compiler_params=_cp("parallel", "arbitrary"),
        name="nsa_sample_cmp",
    )(page_table, qh, *([nsa_view] * CMP_PAGES_PER_STEP))


def _nsa_sample_sel_kernel(pt_ref, q_ref, sel_ref, ocmp_ref, small_ref, new_ref, wbuf_ref, wnew_ref, *rest,
                           n_steps, past):
    pages, o_ref = rest[:PAGES_PER_STEP], rest[PAGES_PER_STEP]
    m_s, l_s, acc_s = rest[PAGES_PER_STEP + 1:]
    s = pl.program_id(1)
    tnew = q_ref.shape[1]
    grp_rows = NSA_REP * tnew

    @pl.when(s == 0)
    def _():
        m_s[...] = jnp.full(m_s.shape, NEG_INF, F32)
        l_s[...] = jnp.zeros(l_s.shape, F32)
        acc_s[...] = jnp.zeros(acc_s.shape, F32)

    q_all = q_ref[...].reshape(NSA_HEADS * tnew, HEAD_DIM) * ATT_SCALE

    @pl.when(s < n_steps - 1)
    def _():
        step_tokens = PAGES_PER_STEP * LANES
        nsl = sel_ref.shape[2]
        bj = lax.broadcasted_iota(jnp.int32, (nsl, step_tokens), 0)
        tk = lax.broadcasted_iota(jnp.int32, (nsl, step_tokens), 1)
        expand = (bj == s * (step_tokens // SEL_BLOCK) + tk // SEL_BLOCK).astype(BF16)
        for g in range(NSA_KV_HEADS):
            keep_g = _dot(sel_ref[g].astype(BF16), expand) > 0.5
            keep = jnp.concatenate([keep_g] * NSA_REP, axis=0)
            qg = q_all[g * grp_rows:(g + 1) * grp_rows]
            k_t = jnp.concatenate([p[0, g].astype(BF16) for p in pages], axis=1)
            v_t = jnp.concatenate([p[1, g].astype(BF16) for p in pages], axis=1)
            _online_softmax_step(_dot(qg, k_t), keep, v_t, m_s.at[g], l_s.at[g], acc_s.at[g])

    @pl.when(s == n_steps - 1)
    def _():
        new = new_ref[...]
        new_block = past // SEL_BLOCK
        osel = []
        for g in range(NSA_KV_HEADS):
            qg = q_all[g * grp_rows:(g + 1) * grp_rows]
            k_new = new[:, (2 * NSA_KV_HEADS + g) * HEAD_DIM:(2 * NSA_KV_HEADS + g + 1) * HEAD_DIM]
            v_new = new[:, (3 * NSA_KV_HEADS + g) * HEAD_DIM:(3 * NSA_KV_HEADS + g + 1) * HEAD_DIM]
            logits = _dot_nt(qg, k_new)
            q = lax.broadcasted_iota(jnp.int32, logits.shape, 0) % tnew
            k = lax.broadcasted_iota(jnp.int32, logits.shape, 1)
            chosen = jnp.concatenate([sel_ref[g][:, new_block:new_block + 1]] * NSA_REP, axis=0) > 0.5
            _online_softmax_step(logits, chosen & (k <= q) & (k < tnew), None, m_s.at[g], l_s.at[g],
                                 acc_s.at[g], v_rows=v_new)
            osel.append(acc_s[g] / l_s[g])
        wbuf = wbuf_ref[...].reshape(2 * NSA_GROUP_ROWS, wbuf_ref.shape[-1]).astype(BF16)
        wnew = wnew_ref[...]
        wb = wbuf.shape[1]
        tq = past + lax.broadcasted_iota(jnp.int32, (tnew, 1), 0)
        pos_buf = past - wb + lax.broadcasted_iota(jnp.int32, (tnew, wb), 1)
        kn = lax.broadcasted_iota(jnp.int32, (tnew, LANES), 1)
        ok = jnp.concatenate([(pos_buf <= tq) & (pos_buf > tq - WINDOW),
                              (kn < tnew) & (past + kn <= tq) & (past + kn > tq - WINDOW)], axis=1)
        gates = _sigmoid(small_ref[...])
        for h in range(NSA_HEADS):
            g, r = divmod(h, NSA_REP)
            qh_ = q_ref[h]
            k_buf = wbuf[g * HEAD_DIM:(g + 1) * HEAD_DIM]
            v_buf = wbuf[NSA_GROUP_ROWS + g * HEAD_DIM:NSA_GROUP_ROWS + (g + 1) * HEAD_DIM]
            k_new = wnew[:, g * HEAD_DIM:(g + 1) * HEAD_DIM]
            v_new = wnew[:, NSA_GROUP_ROWS + g * HEAD_DIM:NSA_GROUP_ROWS + (g + 1) * HEAD_DIM]
            logits = jnp.concatenate([_dot(qh_, k_buf), _dot_nt(qh_, k_new)], axis=1) * ATT_SCALE
            logits = jnp.where(ok, logits, NEG_INF)
            p = jnp.exp(logits - jnp.max(logits, axis=1, keepdims=True))
            pb = p.astype(BF16)
            o_w = (_dot_nt(pb[:, :wb], v_buf) + _dot(pb[:, wb:], v_new)) / jnp.sum(p, axis=1, keepdims=True)
            gl = GN_LANE + 3 * h
            o = (gates[:, gl:gl + 1] * ocmp_ref[h] + gates[:, gl + 1:gl + 2] * osel[g][r * tnew:(r + 1) * tnew]
                 + gates[:, gl + 2:gl + 3] * o_w)
            o_ref[:, h * HEAD_DIM:(h + 1) * HEAD_DIM] = o.astype(o_ref.dtype)


def nsa_sample_sel(page_table, qh, sel, ocmp, zp, nsa_new, win_view, win_new, nsa_view, layer):
    batch, n_pages = page_table.shape
    tnew = qh.shape[1] // batch
    past = n_pages * LANES
    assert past % SEL_BLOCK == 0 and tnew <= SEL_BLOCK
    n_steps = n_pages // PAGES_PER_STEP + 1
    grp_rows = NSA_REP * tnew
    full = lambda a: pl.BlockSpec((None,) + a.shape[1:], lambda b, s, pt: (b,) + (0,) * (a.ndim - 1))
    return pl.pallas_call(
        functools.partial(_nsa_sample_sel_kernel, n_steps=n_steps, past=past),
        grid_spec=pltpu.PrefetchScalarGridSpec(
            num_scalar_prefetch=1,
            grid=(batch, n_steps),
            in_specs=[pl.BlockSpec((NSA_HEADS, tnew, HEAD_DIM), lambda b, s, pt: (1, b, 0)),
                      full(sel), full(ocmp),
                      pl.BlockSpec((tnew, LANES), lambda b, s, pt: (b, ZP_SMALL // LANES)),
                      full(nsa_new),
                      pl.BlockSpec((None, None) + win_view.shape[2:], lambda b, s, pt: (layer, b, 0, 0, 0, 0)),
                      full(win_new)]
            + _page_specs((None, None, 2, NSA_KV_HEADS, HEAD_DIM, LANES), layer, n_pages, slot=1),
            out_specs=pl.BlockSpec((tnew, MIX_WIDTH), lambda b, s, pt: (b, 0)),
            scratch_shapes=[pltpu.VMEM((NSA_KV_HEADS, grp_rows, 1), F32), pltpu.VMEM((NSA_KV_HEADS, grp_rows, 1), F32),
                            pltpu.VMEM((NSA_KV_HEADS, grp_rows, HEAD_DIM), F32)]),
        out_shape=jax.ShapeDtypeStruct((batch * tnew, MIX_WIDTH), BF16),
        compiler_params=_cp("parallel", "arbitrary"),
        name="nsa_sample_sel",
    )(page_table, qh, sel, ocmp, zp, nsa_new, win_view, win_new, *([nsa_view] * PAGES_PER_STEP))


def _merge_kernel(b0, b1, b2, b3, w_ref, g0, g1, g2, g3, o_ref):
    acc = None
    for k, (b_ref, g_ref) in enumerate(zip((b0, b1, b2, b3), (g0, g1, g2, g3))):
        term = _sigmoid(g_ref[...]) * _dot(b_ref[...], w_ref[k])
        acc = term if acc is None else acc + term
    o_ref[...] = acc.astype(o_ref.dtype)


def merge_branches(branches, w, zp, gate_col):
    m = branches[0].shape[0]
    tm = min(1024, m)
    tn = 512
    nj = D_MODEL // tn
    gate_specs = [pl.BlockSpec((tm, tn), lambda i, j, k=k: (i, gate_col // tn + k * nj + j))
                  for k in range(N_BRANCH)]
    return pl.pallas_call(
        _merge_kernel,
        grid=(m // tm, nj),
        in_specs=[pl.BlockSpec((tm, MIX_WIDTH), lambda i, j: (i, 0))] * N_BRANCH
        + [pl.BlockSpec((N_BRANCH, MIX_WIDTH, tn), lambda i, j: (0, 0, j))] + gate_specs,
        out_specs=pl.BlockSpec((tm, tn), lambda i, j: (i, j)),
        out_shape=jax.ShapeDtypeStruct((m, D_MODEL), BF16),
        compiler_params=_cp("parallel", "parallel"),
        name="merge_branches",
    )(*branches, w, *([zp] * N_BRANCH))


IN_SIZES = (MIX_WIDTH, 3 * MIX_WIDTH, IDX_HEADS * IDX_DIM, IDX_DIM, IDX_HEADS, MIX_WIDTH,
            6 * NSA_KV_HEADS * HEAD_DIM, 3 * NSA_HEADS, 3 * MIX_WIDTH, 2 * MLSTM_HEADS, MIX_WIDTH,
            N_BRANCH * D_MODEL)
(OFF_U, OFF_QKVB, OFF_QI, OFF_KI, OFF_WI, OFF_QN, OFF_KVN, OFF_GN, OFF_QKVM, OFF_GIF, OFF_OM,
 OFF_GBR) = np.concatenate([[0], np.cumsum(IN_SIZES)[:-1]]).tolist()

ZP_U, ZP_OM, ZP_Q, ZP_K, ZP_V, ZP_SMALL, ZP_GBR = 0, 512, 1024, 1536, 2048, 2560, 3072
ZP_WIDTH = ZP_GBR + N_BRANCH * D_MODEL
Q_WIDTH = 2 * MIX_WIDTH + IDX_HEADS * IDX_DIM
Q_HEADS = Q_WIDTH // HEAD_DIM


def _block_diag(blocks):
    g, r, c = blocks.shape
    eye = jnp.eye(g, dtype=blocks.dtype)
    return (blocks[:, :, None, :] * eye[:, None, :, None]).reshape(g * r, g * c)


def _s5_discretize(lam_re, lam_im, log_dt, b_re, b_im):
    dt = jnp.exp(log_dt)[:, None]
    mag = jnp.exp(lam_re * dt)
    a_re, a_im = mag * jnp.cos(lam_im * dt), mag * jnp.sin(lam_im * dt)
    den = lam_re * lam_re + lam_im * lam_im
    nr = a_re - 1.0
    coef_re = (nr * lam_re + a_im * lam_im) / den
    coef_im = (a_im * lam_re - nr * lam_im) / den
    bb_re = coef_re[..., None] * b_re - coef_im[..., None] * b_im
    bb_im = coef_re[..., None] * b_im + coef_im[..., None] * b_re
    return a_re, a_im, bb_re, bb_im


def _prep_layer(l, p):
    wt = jnp.transpose(p['w_in'], (2, 0, 1))[:, l, :]
    seg = lambda off, n: wt[off:off + n]
    wt_q = jnp.concatenate([seg(OFF_QKVB, MIX_WIDTH), seg(OFF_QN, MIX_WIDTH),
                            seg(OFF_QI, IDX_HEADS * IDX_DIM)]).astype(BF16)
    wt_kv = jnp.concatenate([seg(OFF_QKVB + MIX_WIDTH, 2 * MIX_WIDTH), seg(OFF_KI, IDX_DIM),
                             seg(OFF_KVN, 6 * NSA_KV_HEADS * HEAD_DIM)]).astype(BF16)
    small = jnp.concatenate([seg(OFF_WI, IDX_HEADS), seg(OFF_GN, 3 * NSA_HEADS), seg(OFF_GIF, 2 * MLSTM_HEADS)])
    pad = jnp.zeros((ZP_GBR - ZP_SMALL - small.shape[0], D_MODEL), F32)
    wt_plain = jnp.concatenate([seg(OFF_U, MIX_WIDTH), seg(OFF_OM, MIX_WIDTH), seg(OFF_QKVM, 3 * MIX_WIDTH),
                                small, pad, seg(OFF_GBR, N_BRANCH * D_MODEL)]).astype(BF16)
    a_re, a_im, bb_re, bb_im = _s5_discretize(p['s5_lam_re'][l], p['s5_lam_im'][l], p['s5_log_dt'][l],
                                              p['s5_b_re'][l], p['s5_b_im'][l])
    s5_b = jnp.concatenate([_block_diag(bb_re.transpose(0, 2, 1)), _block_diag(bb_im.transpose(0, 2, 1))],
                           axis=1).astype(BF16)
    s5_c = jnp.concatenate([_block_diag(p['s5_c_re'][l].transpose(0, 2, 1)),
                            -_block_diag(p['s5_c_im'][l].transpose(0, 2, 1))], axis=0).astype(BF16)
    gate_bias = jnp.zeros((1, LANES), F32)
    gate_bias = gate_bias.at[0, IG_LANE:IG_LANE + MLSTM_HEADS].set(p['mlstm_b_i'][l])
    gate_bias = gate_bias.at[0, FG_LANE:FG_LANE + MLSTM_HEADS].set(p['mlstm_b_f'][l])
    return dict(
        norm_g=p['norm_g'][l][:, None, :],
        w_ffn1_in=p['w_ffn1_in'][l].astype(BF16), w_ffn1_out=p['w_ffn1_out'][l].astype(BF16),
        w_ffn2_in=p['w_ffn2_in'][l].astype(BF16), w_ffn2_out=p['w_ffn2_out'][l].astype(BF16),
        wt_q=wt_q, wt_kv=wt_kv, wt_plain=wt_plain,
        s5_a=jnp.stack([a_re, a_im]).reshape(2, S5_ROWS, LANES), s5_b=s5_b, s5_c=s5_c,
        s5_d=p['s5_d'][l][None, :], w_s5_glu=p['w_s5_glu'][l].astype(BF16),
        gate_bias=gate_bias, mlstm_norm_g=p['mlstm_norm_g'][l][None, :],
        w_branch=p['w_branch'][l].astype(BF16), w_out=p['w_out'][l].astype(BF16))


def _rope_tables(pos):
    inv = ROPE_THETA ** (-jnp.arange(HALF, dtype=F32) / HALF)
    ang = pos.astype(F32)[:, None] * inv[None, :]
    return jnp.cos(ang), jnp.sin(ang)


def _row_tables(cos, sin, rotated):
    one, zero = jnp.ones_like(cos), jnp.zeros_like(sin)
    c = jnp.concatenate([x for r in rotated for x in ((cos, cos) if r else (one, one))], axis=1)
    s = jnp.concatenate([x for r in rotated for x in ((-sin, sin) if r else (zero, zero))], axis=1)
    return c, s


def _ffn(x, ada, sub, g, w_in, w_out, t):
    h = modnorm(x, g, ada, sub, t, BF16)
    return mm_resid(swiglu_in(h, w_in), w_out, x, ada, 3 * sub + 2, 0.5, t)


def _s5_mixer(zp, lw, h0, batch, t):
    xs = mm(zp, lw['s5_b'])
    s, fin = s5_scan(xs.reshape(batch, t, 2, S5_ROWS, LANES), lw['s5_a'], h0)
    o = s5_out(s.reshape(batch * t, 2 * S5_N), lw['s5_c'], zp, lw['s5_d'], lw['w_s5_glu'])
    return o, fin[:, 0].reshape(batch, S5_GROUPS, S5_STATE), fin[:, 1].reshape(batch, S5_GROUPS, S5_STATE)


def _finish_layer(x, ada, lw, zp, branches, t):
    merged = merge_branches(branches, lw['w_branch'], zp, ZP_GBR)
    x = mm_resid(merged, lw['w_out'], x, ada, 5, 1.0, t)
    return _ffn(x, ada, 2, lw['norm_g'][2], lw['w_ffn2_in'], lw['w_ffn2_out'], t)


def _layer_prompt(x, ada, lw, batch, t):
    m = batch * t
    x = _ffn(x, ada, 0, lw['norm_g'][0], lw['w_ffn1_in'], lw['w_ffn1_out'], t)
    h = modnorm(x, lw['norm_g'][1], ada, 1, t, BF16)
    cos, sin = _rope_tables(jnp.arange(t, dtype=jnp.int32))
    cq, sq = _row_tables(cos, sin, (True, True))
    qr = proj_rope_rows(h, lw['wt_q'], cq, sq, t, BF16)
    qh = qr.reshape(m, Q_HEADS, HEAD_DIM).transpose(1, 0, 2)
    kv_t, ki_t, nsa_t, win_t, kv_tb, ki_tb, nsa_tb, win_tb, krow = proj_cols(h, lw['wt_kv'], cos.T, sin.T, batch, t)
    zp = mm_nt(h, lw['wt_plain'])
    o_s5, s5_re, s5_im = _s5_mixer(zp, lw, jnp.zeros((batch, 2, S5_ROWS, LANES), F32), batch, t)
    o_dsa = dsa_prompt(qh, zp, ZP_SMALL, krow, kv_tb, batch, t)
    o_nsa = nsa_prompt(qh, zp, ZP_SMALL, nsa_pool(nsa_t, batch, t), krow, nsa_tb, win_tb, batch, t)
    chunk = math.gcd(t, 256)
    o_ml, mc, mn, mm_ = mlstm(zp, lw['gate_bias'], lw['mlstm_norm_g'],
                              jnp.zeros((batch, MLSTM_HEADS, MLSTM_DK, MLSTM_DK), F32),
                              jnp.zeros((batch, 8, LANES), F32), jnp.zeros((batch, 8, LANES), F32),
                              batch, t, chunk, (ZP_Q, ZP_K, ZP_V, ZP_SMALL, ZP_OM))
    x = _finish_layer(x, ada, lw, zp, [o_s5, o_dsa, o_nsa, o_ml], t)
    tokens_last = lambda a, shape: jnp.moveaxis(a.reshape((batch,) + shape + (a.shape[-1],)), -1, 1)
    wb = min(WINDOW, t)
    state = (tokens_last(kv_t, (2, DSA_HEADS, HEAD_DIM)), jnp.swapaxes(ki_t, 1, 2),
             tokens_last(nsa_t, (4, NSA_KV_HEADS, HEAD_DIM)),
             tokens_last(win_t[:, :, t - wb:], (2, NSA_KV_HEADS, HEAD_DIM)),
             mc, mn[:, :MLSTM_HEADS], mm_[:, :MLSTM_HEADS, 0], s5_re, s5_im)
    return x, state


def _layer_sample(x, ada, lw, layer, batch, t, page_table, views, past):
    m = batch * t
    kidx_view, kv_view, nsa_view, win_view = views
    past_len = page_table.shape[1] * LANES
    x = _ffn(x, ada, 0, lw['norm_g'][0], lw['w_ffn1_in'], lw['w_ffn1_out'], t)
    h = modnorm(x, lw['norm_g'][1], ada, 1, t, BF16)
    cos, sin = _rope_tables(past_len + jnp.arange(t, dtype=jnp.int32))
    cq, sq = _row_tables(cos, sin, (True, True))
    qr = proj_rope_rows(h, lw['wt_q'], cq, sq, t, BF16)
    qh = qr.reshape(m, Q_HEADS, HEAD_DIM).transpose(1, 0, 2)
    ckv, skv = _row_tables(cos, sin, KV_COLS_ROPE + (False,))
    wt_kv = jnp.concatenate([lw['wt_kv'], jnp.zeros((HEAD_DIM, D_MODEL), BF16)])
    kvr = proj_rope_rows(h, wt_kv, ckv, skv, t, F32)
    kv_rows, ki_rows = kvr[:, :2 * MIX_WIDTH], kvr[:, 2 * MIX_WIDTH:2 * MIX_WIDTH + IDX_DIM]
    nsa_rows = kvr[:, 2 * MIX_WIDTH + IDX_DIM:3 * MIX_WIDTH + IDX_DIM]
    win_rows = kvr[:, 3 * MIX_WIDTH + IDX_DIM:KV_COLS]
    zp = mm_nt(h, lw['wt_plain'])
    h0 = jnp.stack([past['s5_re'][layer], past['s5_im'][layer]], axis=1).reshape(batch, 2, S5_ROWS, LANES)
    o_s5, s5_re, s5_im = _s5_mixer(zp, lw, h0, batch, t)

    def new_rows(a):
        a = a.reshape(batch, t, a.shape[-1])
        return jnp.pad(a, ((0, 0), (0, LANES - t), (0, 0))).astype(BF16)

    scores = dsa_sample_scores(page_table, qh, zp, new_rows(ki_rows), kidx_view, layer)
    q_dsa = qr[:, :MIX_WIDTH].reshape(batch, t, DSA_HEADS, HEAD_DIM)
    qbd = jnp.einsum('bqhd,hg->bhqgd', q_dsa, jnp.eye(DSA_HEADS, dtype=BF16)).reshape(batch, DSA_HEADS * t, MIX_WIDTH)
    o_dsa = dsa_sample_attn(page_table, scores, qbd, new_rows(kv_rows), kv_view, layer)
    ocmp, sel = nsa_sample_cmp(page_table, qh, nsa_view, layer)
    o_nsa = nsa_sample_sel(page_table, qh, sel, ocmp, zp, new_rows(nsa_rows), win_view, new_rows(win_rows),
                           nsa_view, layer)
    n0 = jnp.pad(past['mlstm_n'][layer], ((0, 0), (0, 8 - MLSTM_HEADS), (0, 0)))
    m0 = jnp.pad(jnp.broadcast_to(past['mlstm_m'][layer][:, :, None], (batch, MLSTM_HEADS, LANES)),
                 ((0, 0), (0, 8 - MLSTM_HEADS), (0, 0)))
    chunk = 64 if t % 64 == 0 else t
    o_ml, mc, mn, mm_ = mlstm(zp, lw['gate_bias'], lw['mlstm_norm_g'], past['mlstm_c'][layer], n0, m0,
                              batch, t, chunk, (ZP_Q, ZP_K, ZP_V, ZP_SMALL, ZP_OM))
    x = _finish_layer(x, ada, lw, zp, [o_s5, o_dsa, o_nsa, o_ml], t)
    win_buf = past['nsa_win'][layer]
    wb = win_buf.shape[1]
    win_all = jnp.concatenate([win_buf, win_rows.reshape(batch, t, 2, NSA_KV_HEADS, HEAD_DIM)], axis=1)
    state = (kv_rows.reshape(batch, t, 2, DSA_HEADS, HEAD_DIM), ki_rows.reshape(batch, t, IDX_DIM),
             nsa_rows.reshape(batch, t, 4, NSA_KV_HEADS, HEAD_DIM), win_all[:, win_all.shape[1] - wb:],
             mc, mn[:, :MLSTM_HEADS], mm_[:, :MLSTM_HEADS, 0], s5_re, s5_im)
    return x, state


def kernel(x_prompt, x_sample, cache_dsa_kv, cache_dsa_kidx, cache_nsa_kv, cache_nsa_win, state_mlstm_c,
           state_mlstm_n, state_mlstm_m, state_s5_re, state_s5_im, page_table, c_prompt, c_sample, w_ada, b_ada,
           norm_g, w_ffn1_in, w_ffn1_out, w_ffn2_in, w_ffn2_out, w_in, s5_lam_re, s5_lam_im, s5_log_dt, s5_b_re,
           s5_b_im, s5_c_re, s5_c_im, s5_d, w_s5_glu, mlstm_b_i, mlstm_b_f, mlstm_norm_g, w_branch, w_out,
           final_norm_g):
    params = dict(norm_g=norm_g, w_ffn1_in=w_ffn1_in, w_ffn1_out=w_ffn1_out, w_ffn2_in=w_ffn2_in,
                  w_ffn2_out=w_ffn2_out, w_in=w_in, s5_lam_re=s5_lam_re, s5_lam_im=s5_lam_im, s5_log_dt=s5_log_dt,
                  s5_b_re=s5_b_re, s5_b_im=s5_b_im, s5_c_re=s5_c_re, s5_c_im=s5_c_im, s5_d=s5_d,
                  w_s5_glu=w_s5_glu, mlstm_b_i=mlstm_b_i, mlstm_b_f=mlstm_b_f, mlstm_norm_g=mlstm_norm_g,
                  w_branch=w_branch, w_out=w_out)
    bp, tp, d = x_prompt.shape
    bs, ts, _ = x_sample.shape
    depth = w_ada.shape[0]
    views = (jnp.transpose(cache_dsa_kidx, (0, 1, 3, 2)), jnp.transpose(cache_dsa_kv, (0, 1, 3, 4, 5, 2)),
             jnp.transpose(cache_nsa_kv, (0, 1, 3, 4, 5, 2)), jnp.transpose(cache_nsa_win, (0, 1, 3, 4, 5, 2)))
    past = dict(nsa_win=cache_nsa_win, mlstm_c=state_mlstm_c, mlstm_n=state_mlstm_n, mlstm_m=state_mlstm_m,
                s5_re=state_s5_re, s5_im=state_s5_im)
    ada_rows = -(-(bp + bs) // 8) * 8
    c_all = jnp.pad(jnp.concatenate([c_prompt, c_sample]), ((0, ada_rows - bp - bs), (0, 0)))
    xp = x_prompt.reshape(bp * tp, d)
    xs = x_sample.reshape(bs * ts, d)
    st_p, st_s = [], []
    for l in range(depth):
        lw = _prep_layer(l, params)
        ada = ada_project(c_all, w_ada[l].astype(BF16), b_ada[l][None]).reshape(ada_rows, 9, d)
        xp, sp = _layer_prompt(xp, ada[:bp], lw, bp, tp)
        xs, ss = _layer_sample(xs, ada[bp:bp + bs], lw, l, bs, ts, page_table, views, past)
        st_p.append(sp)
        st_s.append(ss)
    g = final_norm_g[None, :]
    y_p = modnorm(xp, g, ada[:bp], None, tp, F32).reshape(bp, tp, d)
    y_s = modnorm(xs, g, ada[bp:bp + bs], None, ts, F32).reshape(bs, ts, d)
    outs = [y_p, y_s]
    for i in range(9):
        outs.append(jnp.stack([s[i] for s in st_p]))
        outs.append(jnp.stack([s[i] for s in st_s]))
    return tuple(outs)
```

```python
import functools
import math

import jax
import jax.numpy as jnp
import numpy as np
from jax import lax
from jax.experimental import pallas as pl
from jax.experimental.pallas import tpu as pltpu

F32 = jnp.float32
BF16 = jnp.bfloat16

D_MODEL = 2048
MIX_WIDTH = D_MODEL // 4
HEAD_DIM = 64
HALF = HEAD_DIM // 2
S5_GROUP = 16
S5_GROUPS = MIX_WIDTH // S5_GROUP
S5_STATE = 64
S5_N = S5_GROUPS * S5_STATE
DSA_HEADS = MIX_WIDTH // HEAD_DIM
IDX_HEADS = 4
IDX_DIM = 64
DSA_TOPK = 256
NSA_HEADS = MIX_WIDTH // HEAD_DIM
NSA_KV_HEADS = 2
NSA_REP = NSA_HEADS // NSA_KV_HEADS
CMP_STRIDE = 16
CMP_LEN = 2 * CMP_STRIDE
SEL_BLOCK = 64
SEL_TOPN = 16
CMP_PER_SEL = SEL_BLOCK // CMP_STRIDE
WINDOW = 512
MLSTM_HEADS = 4
MLSTM_DK = MIX_WIDTH // MLSTM_HEADS
D_FF = 2 * D_MODEL
ROPE_THETA = 10000.0
QUERY_BLOCK = 128
RMS_EPS = 1e-6
NEG_INF = -1e30
FORCE_SCORE = 1e4
N_BRANCH = 4

LANES = 128
VMEM_LIMIT = 56 * 1024 * 1024
HIGHEST = lax.Precision.HIGHEST


def _cp(*sem):
    return pltpu.CompilerParams(dimension_semantics=sem, vmem_limit_bytes=VMEM_LIMIT)


def _dot(a, b):
    return jnp.dot(a, b, preferred_element_type=F32)


def _dot_nt(a, b, precision=None):
    return lax.dot_general(a, b, (((1,), (1,)), ((), ())), preferred_element_type=F32,
                           precision=precision)


def _sigmoid(x):
    return 1.0 / (1.0 + jnp.exp(-x))


def _silu(x):
    return x * _sigmoid(x)


def _ada_kernel(c_ref, w_ref, b_ref, o_ref):
    c = c_ref[...]
    o_ref[...] = _dot(_silu(c).astype(BF16), w_ref[...]) + b_ref[...]


def ada_project(c, w, b):
    r, d = c.shape
    n = w.shape[1]
    tn = 2048
    return pl.pallas_call(
        _ada_kernel,
        grid=(n // tn,),
        in_specs=[pl.BlockSpec((r, d), lambda j: (0, 0)),
                  pl.BlockSpec((d, tn), lambda j: (0, j)),
                  pl.BlockSpec((1, tn), lambda j: (0, j))],
        out_specs=pl.BlockSpec((r, tn), lambda j: (0, j)),
        out_shape=jax.ShapeDtypeStruct((r, n), F32),
        compiler_params=_cp("parallel"),
        name="ada_project",
    )(c, w, b)


def _modnorm_kernel(x_ref, g_ref, ada_ref, o_ref, *, sub):
    x = x_ref[...]
    y = x * lax.rsqrt(jnp.mean(x * x, axis=-1, keepdims=True) + RMS_EPS)
    y = y * g_ref[...]
    if sub is not None:
        shift = ada_ref[3 * sub:3 * sub + 1, :]
        scale = ada_ref[3 * sub + 1:3 * sub + 2, :]
        y = y * (1.0 + scale) + shift
    o_ref[...] = y.astype(o_ref.dtype)


def modnorm(x, g, ada, sub, rows_per_batch, out_dtype):
    m, d = x.shape
    tm = min(512, rows_per_batch)
    nb = rows_per_batch // tm
    return pl.pallas_call(
        functools.partial(_modnorm_kernel, sub=sub),
        grid=(m // tm,),
        in_specs=[pl.BlockSpec((tm, d), lambda i: (i, 0)),
                  pl.BlockSpec((1, d), lambda i: (0, 0)),
                  pl.BlockSpec((None, 9, d), lambda i: (i // nb, 0, 0))],
        out_specs=pl.BlockSpec((tm, d), lambda i: (i, 0)),
        out_shape=jax.ShapeDtypeStruct((m, d), out_dtype),
        compiler_params=_cp("parallel"),
        name="modnorm",
    )(x, g, ada)


def _swiglu_in_kernel(h_ref, wa_ref, wg_ref, o_ref):
    h = h_ref[...]
    a = _dot(h, wa_ref[...])
    g = _dot(h, wg_ref[...])
    o_ref[...] = (_silu(a) * g).astype(o_ref.dtype)


def swiglu_in(h, w):
    m, d = h.shape
    f = w.shape[1] // 2
    tm = min(1024, m)
    tn = 512
    nj = f // tn
    return pl.pallas_call(
        _swiglu_in_kernel,
        grid=(m // tm, nj),
        in_specs=[pl.BlockSpec((tm, d), lambda i, j: (i, 0)),
                  pl.BlockSpec((d, tn), lambda i, j: (0, j)),
                  pl.BlockSpec((d, tn), lambda i, j: (0, j + nj))],
        out_specs=pl.BlockSpec((tm, tn), lambda i, j: (i, j)),
        out_shape=jax.ShapeDtypeStruct((m, f), BF16),
        compiler_params=_cp("parallel", "parallel"),
        name="swiglu_in",
    )(h, w, w)


def _mm_resid_kernel(a_ref, w_ref, x_ref, ada_ref, o_ref, *, gate_row, coef):
    y = _dot(a_ref[...], w_ref[...])
    gate = ada_ref[gate_row:gate_row + 1, :]
    o_ref[...] = x_ref[...] + (coef * gate) * y


def mm_resid(a, w, x, ada, gate_row, coef, rows_per_batch):
    m, k = a.shape
    n = w.shape[1]
    tm = min(1024, rows_per_batch)
    tn = 512
    nb = rows_per_batch // tm
    return pl.pallas_call(
        functools.partial(_mm_resid_kernel, gate_row=gate_row, coef=coef),
        grid=(m // tm, n // tn),
        in_specs=[pl.BlockSpec((tm, k), lambda i, j: (i, 0)),
                  pl.BlockSpec((k, tn), lambda i, j: (0, j)),
                  pl.BlockSpec((tm, tn), lambda i, j: (i, j)),
                  pl.BlockSpec((None, 9, tn), lambda i, j: (i // nb, 0, j))],
        out_specs=pl.BlockSpec((tm, tn), lambda i, j: (i, j)),
        out_shape=jax.ShapeDtypeStruct((m, n), F32),
        compiler_params=_cp("parallel", "parallel"),
        name="mm_resid",
    )(a, w, x, ada)


def _mm_kernel(a_ref, w_ref, o_ref):
    o_ref[...] = _dot(a_ref[...].astype(BF16), w_ref[...]).astype(o_ref.dtype)


def mm(a, w, out_dtype=F32, tn=2048):
    m = a.shape[0]
    k, n = w.shape
    tm = min(1024, m)
    return pl.pallas_call(
        _mm_kernel,
        grid=(m // tm, n // tn),
        in_specs=[pl.BlockSpec((tm, k), lambda i, j: (i, 0)),
                  pl.BlockSpec((k, tn), lambda i, j: (0, j))],
        out_specs=pl.BlockSpec((tm, tn), lambda i, j: (i, j)),
        out_shape=jax.ShapeDtypeStruct((m, n), out_dtype),
        compiler_params=_cp("parallel", "parallel"),
        name="mm",
    )(a, w)


def _mm_nt_kernel(a_ref, wt_ref, o_ref):
    o_ref[...] = _dot_nt(a_ref[...], wt_ref[...]).astype(o_ref.dtype)


def mm_nt(a, wt, tn=1024):
    m, k = a.shape
    n = wt.shape[0]
    tm = min(1024, m)
    return pl.pallas_call(
        _mm_nt_kernel,
        grid=(m // tm, n // tn),
        in_specs=[pl.BlockSpec((tm, k), lambda i, j: (i, 0)),
                  pl.BlockSpec((tn, k), lambda i, j: (j, 0))],
        out_specs=pl.BlockSpec((tm, tn), lambda i, j: (i, j)),
        out_shape=jax.ShapeDtypeStruct((m, n), F32),
        compiler_params=_cp("parallel", "parallel"),
        name="mm_nt",
    )(a, wt)


def _rope_rows(x, cos, sin_signed):
    w = x.shape[1]
    lane = lax.broadcasted_iota(jnp.int32, x.shape, 1)
    fwd = pltpu.roll(x, w - HALF, axis=1)
    bwd = pltpu.roll(x, HALF, axis=1)
    swapped = jnp.where((lane % HEAD_DIM) < HALF, fwd, bwd)
    return x * cos + swapped * sin_signed


def _proj_rope_rows_kernel(h_ref, wt_ref, cos_ref, sin_ref, o_ref, *, periodic):
    z = _dot_nt(h_ref[...], wt_ref[...])
    if periodic:
        reps = z.shape[1] // LANES
        cos = jnp.concatenate([cos_ref[...]] * reps, axis=1)
        sin = jnp.concatenate([sin_ref[...]] * reps, axis=1)
    else:
        cos, sin = cos_ref[...], sin_ref[...]
    o_ref[...] = _rope_rows(z, cos, sin).astype(o_ref.dtype)


def proj_rope_rows(h, wt, cos, sin, rows_per_batch, out_dtype):
    m, k = h.shape
    n = wt.shape[0]
    tm = min(512, rows_per_batch)
    nb = rows_per_batch // tm
    tw = cos.shape[1]
    return pl.pallas_call(
        functools.partial(_proj_rope_rows_kernel, periodic=(tw != n)),
        grid=(m // tm,),
        in_specs=[pl.BlockSpec((tm, k), lambda i: (i, 0)),
                  pl.BlockSpec((n, k), lambda i: (0, 0)),
                  pl.BlockSpec((tm, tw), lambda i: (i % nb, 0)),
                  pl.BlockSpec((tm, tw), lambda i: (i % nb, 0))],
        out_specs=pl.BlockSpec((tm, n), lambda i: (i, 0)),
        out_shape=jax.ShapeDtypeStruct((m, n), out_dtype),
        compiler_params=_cp("parallel"),
        name="proj_rope_rows",
    )(h, wt, cos, sin)


KV_COLS_ROPE = (True,) * 8 + (False,) * 8 + (True,) + (True, True, False, False) * 3
KV_COLS = HEAD_DIM * len(KV_COLS_ROPE)


KROW_HEADS = tuple(range(DSA_HEADS)) + (16, 21, 22)
KROW_IDX, KROW_SEL = DSA_HEADS, DSA_HEADS + 1


def _proj_cols_kernel(h_ref, wt_ref, cos_ref, sin_ref, kv_ref, ki_ref, nsa_ref, win_ref,
                      kvb_ref, kib_ref, nsab_ref, winb_ref, krow_ref):
    zt = _dot_nt(wt_ref[...], h_ref[...])
    cos, sin = cos_ref[...], sin_ref[...]
    parts = []
    for r, rot in enumerate(KV_COLS_ROPE):
        x1 = zt[r * HEAD_DIM:r * HEAD_DIM + HALF]
        x2 = zt[r * HEAD_DIM + HALF:(r + 1) * HEAD_DIM]
        if rot:
            parts += [x1 * cos - x2 * sin, x1 * sin + x2 * cos]
        else:
            parts += [x1, x2]
    out = jnp.concatenate(parts, axis=0)
    bounds = (0, 1024, 1088, 1600, 1856)
    for lo, hi, f_ref, b_ref in zip(bounds[:-1], bounds[1:], (kv_ref, ki_ref, nsa_ref, win_ref),
                                    (kvb_ref, kib_ref, nsab_ref, winb_ref)):
        f_ref[...] = out[lo:hi]
        b_ref[...] = out[lo:hi].astype(BF16)
    for j, r in enumerate(KROW_HEADS):
        krow_ref[j] = out[r * HEAD_DIM:(r + 1) * HEAD_DIM].T.astype(BF16)


def proj_cols(h, wt, cos_t, sin_t, batch, t):
    k = h.shape[1]
    tm = min(512, t)
    nt = t // tm
    widths = (1024, 64, 512, 256)
    out_shape = ([jax.ShapeDtypeStruct((batch, w, t), F32) for w in widths]
                 + [jax.ShapeDtypeStruct((batch, w, t), BF16) for w in widths]
                 + [jax.ShapeDtypeStruct((batch, len(KROW_HEADS), t, HEAD_DIM), BF16)])
    out_specs = ([pl.BlockSpec((None, w, tm), lambda b, i: (b, 0, i)) for w in widths] * 2
                 + [pl.BlockSpec((None, len(KROW_HEADS), tm, HEAD_DIM), lambda b, i: (b, 0, i, 0))])
    return pl.pallas_call(
        _proj_cols_kernel,
        grid=(batch, nt),
        in_specs=[pl.BlockSpec((tm, k), lambda b, i: (b * nt + i, 0)),
                  pl.BlockSpec((KV_COLS, k), lambda b, i: (0, 0)),
                  pl.BlockSpec((HALF, tm), lambda b, i: (0, i)),
                  pl.BlockSpec((HALF, tm), lambda b, i: (0, i))],
        out_specs=out_specs,
        out_shape=out_shape,
        compiler_params=_cp("parallel", "parallel"),
        name="proj_cols",
    )(h, wt, cos_t, sin_t)


S5_ROWS = S5_N // LANES


def _s5_scan_kernel(x_ref, a_ref, h0_ref, s_ref, fin_ref, carry_ref, *, tc):
    j = pl.program_id(1)

    @pl.when(j == 0)
    def _():
        carry_ref[...] = h0_ref[...]

    ar, ai = a_ref[0], a_ref[1]

    def step(t, carry):
        hr, hi = carry
        nr = ar * hr - ai * hi + x_ref[t, 0]
        ni = ar * hi + ai * hr + x_ref[t, 1]
        s_ref[t, 0] = nr
        s_ref[t, 1] = ni
        return nr, ni

    hr, hi = lax.fori_loop(0, tc, step, (carry_ref[0], carry_ref[1]), unroll=8)
    carry_ref[0] = hr
    carry_ref[1] = hi

    @pl.when(j == pl.num_programs(1) - 1)
    def _():
        fin_ref[0] = hr
        fin_ref[1] = hi


def s5_scan(x, a, h0):
    b, t = x.shape[:2]
    tc = min(256, t)
    blk = (None, tc, 2, S5_ROWS, LANES)
    st = (None, 2, S5_ROWS, LANES)
    return pl.pallas_call(
        functools.partial(_s5_scan_kernel, tc=tc),
        grid=(b, t // tc),
        in_specs=[pl.BlockSpec(blk, lambda i, j: (i, j, 0, 0, 0)),
                  pl.BlockSpec((2, S5_ROWS, LANES), lambda i, j: (0, 0, 0)),
                  pl.BlockSpec(st, lambda i, j: (i, 0, 0, 0))],
        out_specs=[pl.BlockSpec(blk, lambda i, j: (i, j, 0, 0, 0)),
                   pl.BlockSpec(st, lambda i, j: (i, 0, 0, 0))],
        out_shape=[jax.ShapeDtypeStruct(x.shape, F32), jax.ShapeDtypeStruct(h0.shape, F32)],
        scratch_shapes=[pltpu.VMEM((2, S5_ROWS, LANES), F32)],
        compiler_params=_cp("parallel", "arbitrary"),
        name="s5_scan",
    )(x, a, h0)


def _gelu_tanh(x):
    return 0.5 * x * (1.0 + jnp.tanh(math.sqrt(2.0 / math.pi) * (x + 0.044715 * (x * x * x))))


def _s5_out_kernel(s_ref, c_ref, u_ref, d_ref, w_ref, o_ref):
    y = _dot(s_ref[...].astype(BF16), c_ref[...]) + d_ref[...] * u_ref[...]
    y = _gelu_tanh(y)
    o_ref[...] = (y * _sigmoid(_dot(y.astype(BF16), w_ref[...]))).astype(o_ref.dtype)


def s5_out(s, cmat, zp, d, wglu):
    m = s.shape[0]
    tm = min(512, m)
    return pl.pallas_call(
        _s5_out_kernel,
        grid=(m // tm,),
        in_specs=[pl.BlockSpec((tm, 2 * S5_N), lambda i: (i, 0)),
                  pl.BlockSpec((2 * S5_N, MIX_WIDTH), lambda i: (0, 0)),
                  pl.BlockSpec((tm, MIX_WIDTH), lambda i: (i, 0)),
                  pl.BlockSpec((1, MIX_WIDTH), lambda i: (0, 0)),
                  pl.BlockSpec((MIX_WIDTH, MIX_WIDTH), lambda i: (0, 0))],
        out_specs=pl.BlockSpec((tm, MIX_WIDTH), lambda i: (i, 0)),
        out_shape=jax.ShapeDtypeStruct((m, MIX_WIDTH), BF16),
        compiler_params=_cp("parallel"),
        name="s5_out",
    )(s, cmat, zp, d, wglu)


WI_LANE = 0
GN_LANE = WI_LANE + IDX_HEADS
IG_LANE = GN_LANE + 3 * NSA_HEADS
FG_LANE = IG_LANE + MLSTM_HEADS


def _mlstm_kernel(q_ref, k_ref, v_ref, gates_ref, om_ref, bias_ref, g_ref, c0_ref, n0_ref, m0_ref,
                  o_ref, c_out, n_out, m_out, c_s, n_s, m_s, *, chunk):
    j = pl.program_id(1)

    @pl.when(j == 0)
    def _():
        c_s[...] = c0_ref[...]
        n_s[...] = n0_ref[...]
        m_s[...] = m0_ref[...]

    gates = gates_ref[...] + bias_ref[...]
    lane = lax.broadcasted_iota(jnp.int32, gates.shape, 1)
    is_f = (lane >= FG_LANE) & (lane < FG_LANE + MLSTM_HEADS)
    logsig = jnp.minimum(gates, 0.0) - jnp.log(1.0 + jnp.exp(-jnp.abs(gates)))
    gl = jnp.where(is_f, logsig, gates)
    row = lax.broadcasted_iota(jnp.int32, (chunk, chunk), 0)
    col = lax.broadcasted_iota(jnp.int32, (chunk, chunk), 1)
    causal = col <= row
    cum = jnp.dot(causal.astype(F32), gl, preferred_element_type=F32, precision=HIGHEST)
    gl_t = gl.T
    cum_t = cum.T
    scale = 1.0 / math.sqrt(MLSTM_DK)
    for h in range(MLSTM_HEADS):
        sl = slice(h * MLSTM_DK, (h + 1) * MLSTM_DK)
        q = q_ref[:, sl]
        k = k_ref[:, sl] * scale
        v = v_ref[:, sl]
        qb, kb, vb = q.astype(BF16), k.astype(BF16), v.astype(BF16)
        c = c_s[h]
        n = n_s[h:h + 1, :]
        m = m_s[h:h + 1, 0:1]
        cum_c = cum[:, FG_LANE + h:FG_LANE + h + 1]
        cum_r = cum_t[FG_LANE + h:FG_LANE + h + 1, :]
        ig_c = gl[:, IG_LANE + h:IG_LANE + h + 1]
        ig_r = gl_t[IG_LANE + h:IG_LANE + h + 1, :]
        logd = jnp.where(causal, cum_c - cum_r + ig_r, NEG_INF)
        log_state = cum_c + m
        m_t = jnp.maximum(log_state, jnp.max(logd, axis=1, keepdims=True))
        w_in = jnp.exp(logd - m_t)
        w_st = jnp.exp(log_state - m_t)
        s = _dot_nt(qb, kb) * w_in
        num = w_st * _dot(qb, c.astype(BF16)) + _dot(s.astype(BF16), vb)
        den = w_st * jnp.sum(q * n, axis=1, keepdims=True) + jnp.sum(s, axis=1, keepdims=True)
        hh = num / jnp.maximum(jnp.abs(den), jnp.exp(-m_t))
        total = cum_r[:, chunk - 1:chunk]
        m_new = jnp.maximum(total + m, jnp.max(total - cum_r + ig_r, axis=1, keepdims=True))
        a = jnp.exp(total + m - m_new)
        ws_c = jnp.exp(total - cum_c + ig_c - m_new)
        kw = k * ws_c
        c_s[h] = a * c + lax.dot_general(kw.astype(BF16), vb, (((0,), (0,)), ((), ())),
                                         preferred_element_type=F32)
        n_s[h:h + 1, :] = a * n + jnp.sum(kw, axis=0, keepdims=True)
        m_s[h:h + 1, :] = jnp.broadcast_to(m_new, (1, LANES))
        hn = hh * lax.rsqrt(jnp.mean(hh * hh, axis=1, keepdims=True) + RMS_EPS)
        o_ref[:, sl] = (hn * g_ref[:, sl] * _sigmoid(om_ref[:, sl])).astype(o_ref.dtype)

    @pl.when(j == pl.num_programs(1) - 1)
    def _():
        c_out[...] = c_s[...]
        n_out[...] = n_s[...]
        m_out[...] = m_s[...]


def mlstm(zp, bias, norm_g, c0, n0, m0, batch, t, chunk, cols):
    nc = t // chunk
    q_col, k_col, v_col, small_col, om_col = cols
    wide = lambda col: pl.BlockSpec((chunk, MIX_WIDTH), lambda b, j: (b * nc + j, col // MIX_WIDTH))
    st = lambda shape: pl.BlockSpec((None,) + shape, lambda b, j: (b,) + (0,) * len(shape))
    return pl.pallas_call(
        functools.partial(_mlstm_kernel, chunk=chunk),
        grid=(batch, nc),
        in_specs=[wide(q_col), wide(k_col), wide(v_col),
                  pl.BlockSpec((chunk, LANES), lambda b, j: (b * nc + j, small_col // LANES)),
                  wide(om_col),
                  pl.BlockSpec((1, LANES), lambda b, j: (0, 0)),
                  pl.BlockSpec((1, MIX_WIDTH), lambda b, j: (0, 0)),
                  st((MLSTM_HEADS, MLSTM_DK, MLSTM_DK)), st((8, LANES)), st((8, LANES))],
        out_specs=[pl.BlockSpec((chunk, MIX_WIDTH), lambda b, j: (b * nc + j, 0)),
                   st((MLSTM_HEADS, MLSTM_DK, MLSTM_DK)), st((8, LANES)), st((8, LANES))],
        out_shape=[jax.ShapeDtypeStruct((batch * t, MIX_WIDTH), BF16),
                   jax.ShapeDtypeStruct((batch, MLSTM_HEADS, MLSTM_DK, MLSTM_DK), F32),
                   jax.ShapeDtypeStruct((batch, 8, LANES), F32),
                   jax.ShapeDtypeStruct((batch, 8, LANES), F32)],
        scratch_shapes=[pltpu.VMEM((MLSTM_HEADS, MLSTM_DK, MLSTM_DK), F32),
                        pltpu.VMEM((8, LANES), F32), pltpu.VMEM((8, LANES), F32)],
        compiler_params=_cp("parallel", "arbitrary"),
        name="mlstm",
    )(zp, zp, zp, zp, zp, bias, norm_g, c0, n0, m0)


INT_MIN = -2 ** 31


def _count(mask):
    return jnp.sum(mask.astype(F32), axis=1, keepdims=True)


def _sort_key(score):
    bits = pltpu.bitcast(score, jnp.int32)
    key = bits ^ ((bits >> 31) & 0x7FFFFFFF)
    return jnp.where(score == 0.0, 0, key)


def _count_paged(mask):
    x = mask.astype(F32)
    pages = x.shape[0]
    if pages % 8 == 0:
        x = jnp.sum(x.reshape((pages // 8, 8) + x.shape[1:]), axis=0)
    return jnp.sum(jnp.sum(x, axis=0, keepdims=True), axis=2, keepdims=True)


def _topk_mask(score, k, paged=False):
    count = _count_paged if paged else _count
    if paged:
        idx = (lax.broadcasted_iota(jnp.int32, score.shape, 0) * LANES
               + lax.broadcasted_iota(jnp.int32, score.shape, 2))
        s = score.shape[0] * LANES
    else:
        idx = lax.broadcasted_iota(jnp.int32, score.shape, 1)
        s = score.shape[1]
    key = _sort_key(score)
    kf = float(k)
    prefix = jnp.where(count(key >= 0) >= kf, 0, INT_MIN).astype(jnp.int32)

    def value_bit(b, prefix):
        cand = prefix | lax.shift_left(jnp.int32(1), 30 - b)
        return jnp.where(count(key >= cand) >= kf, cand, prefix)

    thr = lax.fori_loop(0, 31, value_bit, prefix)
    above = key > thr
    tie = key == thr
    need = kf - count(above)
    nbits = max(1, (s - 1).bit_length())

    def index_bit(b, p):
        cand = p | lax.shift_left(jnp.int32(1), nbits - 1 - b)
        return jnp.where(count(tie & (idx < cand)) < need, cand, p)

    last = lax.fori_loop(0, nbits, index_bit, jnp.zeros_like(prefix))
    return above | (tie & (idx <= last))


ATT_SCALE = 1.0 / math.sqrt(HEAD_DIM)


KEY_CHUNK = 512
DSA_QUERY_BLOCK = 256
NSA_QUERY_BLOCK = 256


def _chunk(c, ck):
    return pl.ds(pl.multiple_of(c * ck, ck), ck)


def _fold_rows(x, op):
    r, q = x.shape
    if r % 64 == 0:
        x = op(x.reshape(8, r // 64, 8, q), axis=1)
    else:
        x = x.reshape(r // 8, 8, q)
    return op(x, axis=0)


def _col_reduce(x, op):
    return op(_fold_rows(x, op), axis=0, keepdims=True)


def _flash_work_t(n_heads, qb, ck):
    return [pltpu.VMEM((n_heads, ck, qb), F32), pltpu.VMEM((n_heads, ck, qb), BF16)]


def _flash_chunks_t(q_ref, heads, k_of, v_of, bias_s, m_s, l_s, acc_s, work, n_act, ck):
    s_s, p_s = work
    for h in heads:
        m_s[h:h + 1, :] = jnp.full((1, m_s.shape[1]), NEG_INF, F32)
        l_s[h:h + 1, :] = jnp.zeros((1, l_s.shape[1]), F32)
        acc_s[h] = jnp.zeros(acc_s.shape[1:], F32)

    def chunk_step(c, carry):
        rows = pl.ds(pl.multiple_of(c * ck, ck), ck)
        bias = bias_s[c]
        for j, h in enumerate(heads):
            s_s[j] = _dot_nt(k_of(h, rows), q_ref[h] * ATT_SCALE) + bias
        m_new = [jnp.maximum(m_s[h:h + 1, :], _col_reduce(s_s[j], jnp.max)) for j, h in enumerate(heads)]
        col_sum = []
        for j, h in enumerate(heads):
            p = jnp.exp(s_s[j] - m_new[j])
            p_s[j] = p.astype(BF16)
            col_sum.append(_col_reduce(p, jnp.sum))
        alpha = []
        for j, h in enumerate(heads):
            alpha.append(jnp.exp(m_s[h:h + 1, :] - m_new[j]))
            l_s[h:h + 1, :] = alpha[j] * l_s[h:h + 1, :] + col_sum[j]
            m_s[h:h + 1, :] = m_new[j]
        for j, h in enumerate(heads):
            acc_s[h] = alpha[j] * acc_s[h] + _dot(v_of(h, c), p_s[j])
        return carry

    lax.fori_loop(0, n_act, chunk_step, 0)
    for h in heads:
        acc_s[h] = acc_s[h] / l_s[h:h + 1, :]


def _dsa_prompt_kernel(q_ref, qi_ref, small_ref, ki_ref, k_ref, v_ref, o_ref, key_s, bias_s, acc_s, m_s, l_s,
                       s_s, p_s, *, topk, ck):
    i = pl.program_id(1)
    nch, _, qb = key_s.shape
    t = nch * ck
    n_act = (i * qb + qb - 1) // ck + 1
    tq = i * qb + lax.broadcasted_iota(jnp.int32, (1, qb), 1)
    krow = lax.broadcasted_iota(jnp.int32, (ck, qb), 0)
    kf = float(topk)
    small_t = small_ref[...].T
    wi = [small_t[WI_LANE + h:WI_LANE + h + 1, :] for h in range(IDX_HEADS)]

    def score_chunk(c, carry):
        ki = ki_ref[pl.ds(pl.multiple_of(c * ck, ck), ck), :]
        score = None
        for h in range(IDX_HEADS):
            term = wi[h] * jnp.maximum(_dot_nt(ki, qi_ref[h]), 0.0)
            score = term if score is None else score + term
        key_s[c] = _sort_key(jnp.where(c * ck + krow <= tq, score, NEG_INF))
        return carry

    lax.fori_loop(0, n_act, score_chunk, 0)

    def count_keys(pred):
        def body(c, acc):
            return acc + _fold_rows(pred(key_s[c], c).astype(F32), jnp.sum)
        return jnp.sum(lax.fori_loop(0, n_act, body, jnp.zeros((8, qb), F32)), axis=0, keepdims=True)

    prefix = jnp.where(count_keys(lambda k, c: k >= 0) >= kf, 0, INT_MIN).astype(jnp.int32)

    def value_bit(b, prefix):
        cand = prefix | lax.shift_left(jnp.int32(1), 30 - b)
        return jnp.where(count_keys(lambda k, c: k >= cand) >= kf, cand, prefix)

    thr = lax.fori_loop(0, 31, value_bit, prefix)
    need = kf - count_keys(lambda k, c: k > thr)
    n_tie = count_keys(lambda k, c: k == thr)
    nbits = max(1, (t - 1).bit_length())

    def resolve_ties():
        def index_bit(b, p):
            cand = p | lax.shift_left(jnp.int32(1), nbits - 1 - b)
            below = count_keys(lambda k, c: (k == thr) & (c * ck + krow < cand))
            return jnp.where(below < need, cand, p)
        return lax.fori_loop(0, nbits, index_bit, jnp.zeros((1, qb), jnp.int32))

    last = lax.cond(jnp.max(n_tie - need) > 0.0, resolve_ties, lambda: jnp.full((1, qb), t, jnp.int32))

    def bias_chunk(c, carry):
        k = key_s[c]
        kpos = c * ck + krow
        keep = ((k > thr) | ((k == thr) & (kpos <= last))) & (kpos <= tq)
        bias_s[c] = jnp.where(keep, 0.0, NEG_INF)
        return carry

    lax.fori_loop(0, n_act, bias_chunk, 0)

    _flash_chunks_t(q_ref, range(DSA_HEADS), lambda h, rows: k_ref[h, rows, :],
                    lambda h, c: v_ref[h * HEAD_DIM:(h + 1) * HEAD_DIM, _chunk(c, ck)],
                    bias_s, m_s, l_s, acc_s, (s_s, p_s), n_act, ck)
    o_ref[...] = acc_s[...].reshape(DSA_HEADS * HEAD_DIM, qb).T.astype(o_ref.dtype)


def dsa_prompt(qh, zp, small_col, krow, kv_t, batch, t):
    qb = min(DSA_QUERY_BLOCK, t)
    nq = t // qb
    topk = min(DSA_TOPK, t // 4)
    ck = min(KEY_CHUNK, t)
    nch = t // ck
    assert t % ck == 0 and ck >= topk and ck % qb == 0
    return pl.pallas_call(
        functools.partial(_dsa_prompt_kernel, topk=topk, ck=ck),
        grid=(batch, nq),
        in_specs=[pl.BlockSpec((DSA_HEADS, qb, HEAD_DIM), lambda b, i: (0, b * nq + i, 0)),
                  pl.BlockSpec((IDX_HEADS, qb, HEAD_DIM), lambda b, i: (4, b * nq + i, 0)),
                  pl.BlockSpec((qb, LANES), lambda b, i: (b * nq + i, small_col // LANES)),
                  pl.BlockSpec((None, None, t, HEAD_DIM), lambda b, i: (b, KROW_IDX, 0, 0)),
                  pl.BlockSpec((None, DSA_HEADS, t, HEAD_DIM), lambda b, i: (b, 0, 0, 0)),
                  pl.BlockSpec((None, MIX_WIDTH, t), lambda b, i: (b, 1, 0))],
        out_specs=pl.BlockSpec((qb, MIX_WIDTH), lambda b, i: (b * nq + i, 0)),
        out_shape=jax.ShapeDtypeStruct((batch * t, MIX_WIDTH), BF16),
        scratch_shapes=[pltpu.VMEM((nch, ck, qb), jnp.int32), pltpu.VMEM((nch, ck, qb), F32),
                        pltpu.VMEM((DSA_HEADS, HEAD_DIM, qb), F32),
                        pltpu.VMEM((DSA_HEADS, qb), F32), pltpu.VMEM((DSA_HEADS, qb), F32)]
        + _flash_work_t(DSA_HEADS, qb, ck),
        compiler_params=_cp("parallel", "arbitrary"),
        name="dsa_prompt",
    )(qh, qh, zp, krow, krow, kv_t)


NSA_GROUP_ROWS = NSA_KV_HEADS * HEAD_DIM


def _nsa_pool_kernel(x_ref, o_ref):
    t = x_ref.shape[1]
    nsub = t // CMP_STRIDE
    tok = lax.broadcasted_iota(jnp.int32, (t, nsub), 0)
    c = lax.broadcasted_iota(jnp.int32, (t, nsub), 1)
    inside = (tok >= c * CMP_STRIDE) & (tok < c * CMP_STRIDE + CMP_LEN) & (c < nsub - 1)
    pool = jnp.where(inside, 1.0 / CMP_LEN, 0.0).astype(F32)
    o_ref[...] = jnp.dot(x_ref[...], pool, preferred_element_type=F32, precision=HIGHEST).astype(o_ref.dtype)


def nsa_pool(nsa_t, batch, t):
    rows = 2 * NSA_GROUP_ROWS
    return pl.pallas_call(
        _nsa_pool_kernel,
        grid=(batch,),
        in_specs=[pl.BlockSpec((None, rows, t), lambda b: (b, 0, 0))],
        out_specs=pl.BlockSpec((None, rows, t // CMP_STRIDE), lambda b: (b, 0, 0)),
        out_shape=jax.ShapeDtypeStruct((batch, rows, t // CMP_STRIDE), BF16),
        compiler_params=_cp("parallel"),
        name="nsa_pool",
    )(nsa_t)


def _nsa_select_blocks(pcsum, tq, ns, n_sel, lanes=None):
    rows, nc = pcsum.shape
    lanes = ns if lanes is None else lanes
    c = lax.broadcasted_iota(jnp.int32, (nc, lanes), 0)
    j = lax.broadcasted_iota(jnp.int32, (nc, lanes), 1)
    pool = ((c >= j * CMP_PER_SEL) & (c < (j + 1) * CMP_PER_SEL)).astype(F32)
    imp = jnp.dot(pcsum, pool, preferred_element_type=F32, precision=HIGHEST)
    blk = lax.broadcasted_iota(jnp.int32, (rows, lanes), 1)
    forced = (blk == tq // SEL_BLOCK) | (blk == 0)
    imp = jnp.where(forced, FORCE_SCORE, imp)
    imp = jnp.where((blk * SEL_BLOCK <= tq) & (blk < ns), imp, NEG_INF)
    return _topk_mask(imp, n_sel) & (blk < ns)


def _top_rows(v, n):
    ns = v.shape[0]
    row = lax.broadcasted_iota(jnp.int32, v.shape, 0)
    rank = jnp.zeros(v.shape, F32)
    for i in range(ns):
        vi = v[i:i + 1, :]
        rank = rank + ((vi > v) | ((vi == v) & (row > i))).astype(F32)
    return rank < float(n)


def _nsa_prompt_kernel(q_ref, small_ref, cm_ref, ksel0_ref, ksel1_ref, vsel_ref, *rest, ns, n_sel, ck, nwin):
    wins, o_ref = rest[:nwin], rest[nwin]
    bias_s, ocmp_s, osel_s, m_s, l_s, s_s, p_s = rest[nwin + 1:]
    i = pl.program_id(1)
    qb = bias_s.shape[2]
    n_act = (i * qb + qb - 1) // ck + 1
    tq_row = i * qb + lax.broadcasted_iota(jnp.int32, (ns, qb), 1)
    tq1 = tq_row[0:1, :]
    blk = lax.broadcasted_iota(jnp.int32, (ns, qb), 0)
    ncp = cm_ref.shape[1]
    cidx = lax.broadcasted_iota(jnp.int32, (ncp, qb), 0)
    cvalid = (cidx * CMP_STRIDE + CMP_LEN - 1 <= tq1) & (cidx < ncp - 1)
    for g in range(NSA_KV_HEADS):
        grows = slice(g * HEAD_DIM, (g + 1) * HEAD_DIM)
        vrows = slice(NSA_GROUP_ROWS + g * HEAD_DIM, NSA_GROUP_ROWS + (g + 1) * HEAD_DIM)
        kcm = cm_ref[grows, :].astype(F32).T.astype(BF16)
        vcm = cm_ref[vrows, :]
        pcsum = jnp.zeros((ncp, qb), F32)
        for r in range(NSA_REP):
            h = g * NSA_REP + r
            lc = jnp.where(cvalid, _dot_nt(kcm, q_ref[h] * ATT_SCALE), NEG_INF)
            p = jnp.exp(lc - _col_reduce(lc, jnp.max))
            pc = jnp.where(cvalid, p / _col_reduce(p, jnp.sum), 0.0)
            ocmp_s[h] = _dot(vcm, pc.astype(BF16))
            pcsum = pcsum + pc
        pj = lax.broadcasted_iota(jnp.int32, (ns, ncp), 0)
        pc_ = lax.broadcasted_iota(jnp.int32, (ns, ncp), 1)
        pool = ((pc_ >= pj * CMP_PER_SEL) & (pc_ < (pj + 1) * CMP_PER_SEL)).astype(F32)
        imp = jnp.dot(pool, pcsum, preferred_element_type=F32, precision=HIGHEST)
        imp = jnp.where((blk == tq_row // SEL_BLOCK) | (blk == 0), FORCE_SCORE, imp)
        imp = jnp.where(blk * SEL_BLOCK <= tq_row, imp, NEG_INF)
        sel = _top_rows(imp, n_sel).astype(BF16)

        def bias_chunk(c, carry, sel=sel):
            tk = c * ck + lax.broadcasted_iota(jnp.int32, (ck, ns), 0)
            bj = lax.broadcasted_iota(jnp.int32, (ck, ns), 1)
            expand = (tk // SEL_BLOCK == bj).astype(BF16)
            kpos = c * ck + lax.broadcasted_iota(jnp.int32, (ck, qb), 0)
            keep = (_dot(expand, sel) > 0.5) & (kpos <= tq1)
            bias_s[c] = jnp.where(keep, 0.0, NEG_INF)
            return carry

        lax.fori_loop(0, n_act, bias_chunk, 0)

        ksel_ref = (ksel0_ref, ksel1_ref)[g]
        _flash_chunks_t(q_ref, range(g * NSA_REP, (g + 1) * NSA_REP), lambda h, rows, k=ksel_ref: k[rows, :],
                        lambda h, c, grows=grows: vsel_ref[grows, _chunk(c, ck)],
                        bias_s, m_s, l_s, osel_s, (s_s, p_s), n_act, ck)

    kw = jnp.concatenate([w[...] for w in wins], axis=1)
    wpos = (i - (nwin - 1)) * qb + lax.broadcasted_iota(jnp.int32, (nwin * qb, qb), 0)
    wok = (wpos >= 0) & (wpos <= tq1) & (wpos > tq1 - WINDOW)
    gates = _sigmoid(small_ref[...]).T
    for g in range(NSA_KV_HEADS):
        kwin = kw[g * HEAD_DIM:(g + 1) * HEAD_DIM].astype(F32).T.astype(BF16)
        vwin = kw[NSA_GROUP_ROWS + g * HEAD_DIM:NSA_GROUP_ROWS + (g + 1) * HEAD_DIM]
        for r in range(NSA_REP):
            h = g * NSA_REP + r
            lw_ = jnp.where(wok, _dot_nt(kwin, q_ref[h] * ATT_SCALE), NEG_INF)
            p = jnp.exp(lw_ - _col_reduce(lw_, jnp.max))
            o_w = _dot(vwin, p.astype(BF16)) / _col_reduce(p, jnp.sum)
            gl = GN_LANE + 3 * h
            ocmp_s[h] = (gates[gl:gl + 1, :] * ocmp_s[h] + gates[gl + 1:gl + 2, :] * osel_s[h]
                         + gates[gl + 2:gl + 3, :] * o_w)
    o_ref[...] = ocmp_s[...].reshape(NSA_HEADS * HEAD_DIM, qb).T.astype(o_ref.dtype)


def nsa_prompt(qh, zp, small_col, cm_t, krow, nsa_t, win_t, batch, t):
    qb = min(NSA_QUERY_BLOCK, t)
    nq = t // qb
    ns = -(-t // SEL_BLOCK)
    nwin = -(-WINDOW // qb) + 1
    win_specs = [pl.BlockSpec((None, 2 * NSA_GROUP_ROWS, qb),
                              lambda b, i, j=j: (b, 0, jnp.maximum(i - (nwin - 1) + j, 0)))
                 for j in range(nwin)]
    ck = min(KEY_CHUNK, t)
    assert t % ck == 0 and ck % qb == 0 and t % SEL_BLOCK == 0
    return pl.pallas_call(
        functools.partial(_nsa_prompt_kernel, ns=ns, n_sel=min(SEL_TOPN, ns), ck=ck, nwin=nwin),
        grid=(batch, nq),
        in_specs=[pl.BlockSpec((NSA_HEADS, qb, HEAD_DIM), lambda b, i: (1, b * nq + i, 0)),
                  pl.BlockSpec((qb, LANES), lambda b, i: (b * nq + i, small_col // LANES)),
                  pl.BlockSpec((None, 2 * NSA_GROUP_ROWS, cm_t.shape[2]), lambda b, i: (b, 0, 0)),
                  pl.BlockSpec((None, None, t, HEAD_DIM), lambda b, i: (b, KROW_SEL, 0, 0)),
                  pl.BlockSpec((None, None, t, HEAD_DIM), lambda b, i: (b, KROW_SEL + 1, 0, 0)),
                  pl.BlockSpec((None, NSA_GROUP_ROWS, t), lambda b, i: (b, 3, 0))] + win_specs,
        out_specs=pl.BlockSpec((qb, MIX_WIDTH), lambda b, i: (b * nq + i, 0)),
        out_shape=jax.ShapeDtypeStruct((batch * t, MIX_WIDTH), BF16),
        scratch_shapes=[pltpu.VMEM((t // ck, ck, qb), F32), pltpu.VMEM((NSA_HEADS, HEAD_DIM, qb), F32),
                        pltpu.VMEM((NSA_HEADS, HEAD_DIM, qb), F32),
                        pltpu.VMEM((NSA_HEADS, qb), F32), pltpu.VMEM((NSA_HEADS, qb), F32)]
        + _flash_work_t(NSA_REP, qb, ck),
        compiler_params=_cp("parallel", "arbitrary"),
        name="nsa_prompt",
    )(qh, zp, cm_t, krow, krow, nsa_t, *([win_t] * nwin))


PAGES_PER_STEP = 8


def _page_specs(block, layer, n_pages, slot=None, per_step=PAGES_PER_STEP):
    def spec(j):
        def index(b, s, pt):
            page = pt[b, jnp.minimum(s * per_step + j, n_pages - 1)]
            lead = (layer, page) if slot is None else (layer, page, slot)
            return lead + (0,) * (len(block) - len(lead))
        return pl.BlockSpec(block, index)
    return [spec(j) for j in range(per_step)]


SCORE_PAGES_PER_STEP = 32


def _dsa_sample_scores_kernel(pt_ref, qi_ref, small_ref, kinew_ref, *rest, n_steps):
    pages, o_ref = rest[:SCORE_PAGES_PER_STEP], rest[SCORE_PAGES_PER_STEP]
    s = pl.program_id(1)
    tnew = qi_ref.shape[1]

    def score(keys_t=None, keys=None):
        acc = None
        for h in range(IDX_HEADS):
            d = _dot(qi_ref[h], keys_t) if keys is None else _dot_nt(qi_ref[h], keys)
            term = small_ref[:, WI_LANE + h:WI_LANE + h + 1] * jnp.maximum(d, 0.0)
            acc = term if acc is None else acc + term
        return acc

    @pl.when(s < n_steps - 1)
    def _():
        for j, page in enumerate(pages):
            o_ref[j] = score(keys_t=page[...].astype(BF16))

    @pl.when(s == n_steps - 1)
    def _():
        sc = score(keys=kinew_ref[...])
        q = lax.broadcasted_iota(jnp.int32, sc.shape, 0)
        k = lax.broadcasted_iota(jnp.int32, sc.shape, 1)
        o_ref[0] = jnp.where((k <= q) & (k < tnew), sc, NEG_INF)
        for j in range(1, SCORE_PAGES_PER_STEP):
            o_ref[j] = jnp.full(sc.shape, NEG_INF, F32)


def dsa_sample_scores(page_table, qh, zp, ki_new, kidx_view, layer):
    batch, n_pages = page_table.shape
    tnew = qh.shape[1] // batch
    assert n_pages % SCORE_PAGES_PER_STEP == 0
    n_steps = n_pages // SCORE_PAGES_PER_STEP + 1
    return pl.pallas_call(
        functools.partial(_dsa_sample_scores_kernel, n_steps=n_steps),
        grid_spec=pltpu.PrefetchScalarGridSpec(
            num_scalar_prefetch=1,
            grid=(batch, n_steps),
            in_specs=[pl.BlockSpec((IDX_HEADS, tnew, HEAD_DIM), lambda b, s, pt: (4, b, 0)),
                      pl.BlockSpec((tnew, LANES), lambda b, s, pt: (b, ZP_SMALL // LANES)),
                      pl.BlockSpec((None, LANES, IDX_DIM), lambda b, s, pt: (b, 0, 0))]
            + _page_specs((None, None, IDX_DIM, LANES), layer, n_pages, per_step=SCORE_PAGES_PER_STEP),
            out_specs=pl.BlockSpec((None, SCORE_PAGES_PER_STEP, tnew, LANES), lambda b, s, pt: (b, s, 0, 0))),
        out_shape=jax.ShapeDtypeStruct((batch, SCORE_PAGES_PER_STEP * n_steps, tnew, LANES), F32),
        compiler_params=_cp("parallel", "arbitrary"),
        name="dsa_sample_scores",
    )(page_table, qh, zp, ki_new, *([kidx_view] * SCORE_PAGES_PER_STEP))


def _online_softmax_step(logits, keep, v_t, m_ref, l_ref, acc_ref, v_rows=None):
    lm = jnp.where(keep, logits, NEG_INF)
    m_old = m_ref[...]
    m_new = jnp.maximum(m_old, jnp.max(lm, axis=1, keepdims=True))
    alpha = jnp.exp(m_old - m_new)
    p = jnp.where(keep, jnp.exp(lm - m_new), 0.0)
    pv = _dot_nt(p.astype(BF16), v_t) if v_rows is None else _dot(p.astype(BF16), v_rows)
    l_ref[...] = alpha * l_ref[...] + jnp.sum(p, axis=1, keepdims=True)
    acc_ref[...] = alpha * acc_ref[...] + pv
    m_ref[...] = m_new


def _dsa_sample_attn_kernel(pt_ref, score_ref, qbd_ref, kvnew_ref, *rest, n_steps, topk):
    pages, o_ref = rest[:PAGES_PER_STEP], rest[PAGES_PER_STEP]
    keep_s, m_s, l_s, acc_s = rest[PAGES_PER_STEP + 1:]
    s = pl.program_id(1)
    tnew = score_ref.shape[1]

    @pl.when(s == 0)
    def _():
        keep_s[...] = _topk_mask(score_ref[...], topk, paged=True).astype(F32)
        m_s[...] = jnp.full(m_s.shape, NEG_INF, F32)
        l_s[...] = jnp.zeros(l_s.shape, F32)
        acc_s[...] = jnp.zeros(acc_s.shape, F32)

    qbd = qbd_ref[...] * ATT_SCALE

    def keep_rows(page):
        return jnp.concatenate([keep_s[page]] * DSA_HEADS, axis=0) > 0.5

    @pl.when(s < n_steps - 1)
    def _():
        k_t = jnp.concatenate([p[0].reshape(MIX_WIDTH, LANES).astype(BF16) for p in pages], axis=1)
        v_t = jnp.concatenate([p[1].reshape(MIX_WIDTH, LANES).astype(BF16) for p in pages], axis=1)
        keep = jnp.concatenate([keep_rows(s * PAGES_PER_STEP + j) for j in range(PAGES_PER_STEP)], axis=1)
        _online_softmax_step(_dot(qbd, k_t), keep, v_t, m_s, l_s, acc_s)

    @pl.when(s == n_steps - 1)
    def _():
        kv = kvnew_ref[...]
        logits = _dot_nt(qbd, kv[:, :MIX_WIDTH])
        q = lax.broadcasted_iota(jnp.int32, logits.shape, 0) % tnew
        k = lax.broadcasted_iota(jnp.int32, logits.shape, 1)
        keep = keep_rows((n_steps - 1) * PAGES_PER_STEP) & (k <= q) & (k < tnew)
        _online_softmax_step(logits, keep, None, m_s, l_s, acc_s, v_rows=kv[:, MIX_WIDTH:])
        out = acc_s[...] / l_s[...]
        for h in range(DSA_HEADS):
            o_ref[:, h * HEAD_DIM:(h + 1) * HEAD_DIM] = (
                out[h * tnew:(h + 1) * tnew, h * HEAD_DIM:(h + 1) * HEAD_DIM].astype(o_ref.dtype))


def dsa_sample_attn(page_table, scores, qbd, kv_new, kv_view, layer):
    batch, n_pages = page_table.shape
    tnew = scores.shape[2]
    n_steps = n_pages // PAGES_PER_STEP + 1
    topk = min(DSA_TOPK, (n_pages * LANES + tnew) // 4)
    rows = DSA_HEADS * tnew
    return pl.pallas_call(
        functools.partial(_dsa_sample_attn_kernel, n_steps=n_steps, topk=topk),
        grid_spec=pltpu.PrefetchScalarGridSpec(
            num_scalar_prefetch=1,
            grid=(batch, n_steps),
            in_specs=[pl.BlockSpec((None,) + scores.shape[1:], lambda b, s, pt: (b, 0, 0, 0)),
                      pl.BlockSpec((None, rows, MIX_WIDTH), lambda b, s, pt: (b, 0, 0)),
                      pl.BlockSpec((None, LANES, 2 * MIX_WIDTH), lambda b, s, pt: (b, 0, 0))]
            + _page_specs((None, None, 2, DSA_HEADS, HEAD_DIM, LANES), layer, n_pages),
            out_specs=pl.BlockSpec((tnew, MIX_WIDTH), lambda b, s, pt: (b, 0)),
            scratch_shapes=[pltpu.VMEM(scores.shape[1:], F32), pltpu.VMEM((rows, 1), F32),
                            pltpu.VMEM((rows, 1), F32), pltpu.VMEM((rows, MIX_WIDTH), F32)]),
        out_shape=jax.ShapeDtypeStruct((batch * tnew, MIX_WIDTH), BF16),
        compiler_params=_cp("parallel", "arbitrary"),
        name="dsa_sample_attn",
    )(page_table, scores, qbd, kv_new, *([kv_view] * PAGES_PER_STEP))


CMP_PAGES_PER_STEP = 16


def _nsa_sample_cmp_kernel(pt_ref, q_ref, *rest, n_steps, past, ns, n_sel):
    pages = rest[:CMP_PAGES_PER_STEP]
    ocmp_ref, sel_ref, sub_s = rest[CMP_PAGES_PER_STEP:]
    s = pl.program_id(1)
    tnew = q_ref.shape[1]
    rows = 2 * NSA_GROUP_ROWS
    n_tok = CMP_PAGES_PER_STEP * LANES
    tok = lax.broadcasted_iota(jnp.int32, (n_tok, LANES), 0)
    col = lax.broadcasted_iota(jnp.int32, (n_tok, LANES), 1)
    pool = jnp.where(col == tok // CMP_STRIDE, 1.0 / CMP_STRIDE, 0.0).astype(BF16)
    x = jnp.concatenate([page[...].reshape(rows, LANES) for page in pages], axis=1)
    x_hi = x.astype(BF16)
    r1 = x - x_hi.astype(F32)
    x_mid = r1.astype(BF16)
    x_lo = (r1 - x_mid.astype(F32)).astype(BF16)
    sub_s[s] = _dot(x_hi, pool) + _dot(x_mid, pool) + _dot(x_lo, pool)

    @pl.when(s == n_steps - 1)
    def _():
        sub_all = jnp.concatenate([sub_s[i] for i in range(n_steps)], axis=1)
        ncp = sub_all.shape[1]
        cm = (0.5 * (sub_all + pltpu.roll(sub_all, ncp - 1, axis=1))).astype(BF16)
        nc = (past + tnew) // CMP_STRIDE - 1
        tq = past + lax.broadcasted_iota(jnp.int32, (tnew, 1), 0)
        cidx = lax.broadcasted_iota(jnp.int32, (tnew, ncp), 1)
        cvalid = (cidx * CMP_STRIDE + CMP_LEN - 1 <= tq) & (cidx < nc)
        for g in range(NSA_KV_HEADS):
            kcm = cm[g * HEAD_DIM:(g + 1) * HEAD_DIM]
            vcm = cm[NSA_GROUP_ROWS + g * HEAD_DIM:NSA_GROUP_ROWS + (g + 1) * HEAD_DIM]
            pcsum = jnp.zeros((tnew, ncp), F32)
            for r in range(NSA_REP):
                h = g * NSA_REP + r
                lc = jnp.where(cvalid, _dot(q_ref[h], kcm) * ATT_SCALE, NEG_INF)
                p = jnp.exp(lc - jnp.max(lc, axis=1, keepdims=True))
                pc = jnp.where(cvalid, p / jnp.sum(p, axis=1, keepdims=True), 0.0)
                ocmp_ref[h] = _dot_nt(pc.astype(BF16), vcm)
                pcsum = pcsum + pc
            sel_ref[g] = _nsa_select_blocks(pcsum, tq, ns, n_sel, lanes=sel_ref.shape[-1]).astype(F32)


def nsa_sample_cmp(page_table, qh, nsa_view, layer):
    batch, n_pages = page_table.shape
    tnew = qh.shape[1] // batch
    past = n_pages * LANES
    assert tnew < CMP_STRIDE and n_pages % CMP_PAGES_PER_STEP == 0
    n_steps = n_pages // CMP_PAGES_PER_STEP
    ns = -(-(past + tnew) // SEL_BLOCK)
    ns_lanes = -(-ns // LANES) * LANES
    return pl.pallas_call(
        functools.partial(_nsa_sample_cmp_kernel, n_steps=n_steps, past=past, ns=ns, n_sel=min(SEL_TOPN, ns)),
        grid_spec=pltpu.PrefetchScalarGridSpec(
            num_scalar_prefetch=1,
            grid=(batch, n_steps),
            in_specs=[pl.BlockSpec((NSA_HEADS, tnew, HEAD_DIM), lambda b, s, pt: (1, b, 0))]
            + _page_specs((None, None, 2, NSA_KV_HEADS, HEAD_DIM, LANES), layer, n_pages, slot=0,
                          per_step=CMP_PAGES_PER_STEP),
            out_specs=[pl.BlockSpec((None, NSA_HEADS, tnew, HEAD_DIM), lambda b, s, pt: (b, 0, 0, 0)),
                       pl.BlockSpec((None, NSA_KV_HEADS, tnew, ns_lanes), lambda b, s, pt: (b, 0, 0, 0))],
            scratch_shapes=[pltpu.VMEM((n_steps, 2 * NSA_GROUP_ROWS, LANES), F32)]),
        out_shape=[jax.ShapeDtypeStruct((batch, NSA_HEADS, tnew, HEAD_DIM), F32),
                   jax.ShapeDtypeStruct((batch, NSA_KV_HEADS, tnew, ns_lanes), F32)],
        compiler_params=_cp("parallel", "arbitrary"),
        name="nsa_sample_cmp",
    )(page_table, qh, *([nsa_view] * CMP_PAGES_PER_STEP))


def _nsa_sample_sel_kernel(pt_ref, q_ref, sel_ref, ocmp_ref, small_ref, new_ref, wbuf_ref, wnew_ref, *rest,
                           n_steps, past):
    pages, o_ref = rest[:PAGES_PER_STEP], rest[PAGES_PER_STEP]
    m_s, l_s, acc_s = rest[PAGES_PER_STEP + 1:]
    s = pl.program_id(1)
    tnew = q_ref.shape[1]
    grp_rows = NSA_REP * tnew

    @pl.when(s == 0)
    def _():
        m_s[...] = jnp.full(m_s.shape, NEG_INF, F32)
        l_s[...] = jnp.zeros(l_s.shape, F32)
        acc_s[...] = jnp.zeros(acc_s.shape, F32)

    q_all = q_ref[...].reshape(NSA_HEADS * tnew, HEAD_DIM) * ATT_SCALE

    @pl.when(s < n_steps - 1)
    def _():
        step_tokens = PAGES_PER_STEP * LANES
        nsl = sel_ref.shape[2]
        bj = lax.broadcasted_iota(jnp.int32, (nsl, step_tokens), 0)
        tk = lax.broadcasted_iota(jnp.int32, (nsl, step_tokens), 1)
        expand = (bj == s * (step_tokens // SEL_BLOCK) + tk // SEL_BLOCK).astype(BF16)
        for g in range(NSA_KV_HEADS):
            keep_g = _dot(sel_ref[g].astype(BF16), expand) > 0.5
            keep = jnp.concatenate([keep_g] * NSA_REP, axis=0)
            qg = q_all[g * grp_rows:(g + 1) * grp_rows]
            k_t = jnp.concatenate([p[0, g].astype(BF16) for p in pages], axis=1)
            v_t = jnp.concatenate([p[1, g].astype(BF16) for p in pages], axis=1)
            _online_softmax_step(_dot(qg, k_t), keep, v_t, m_s.at[g], l_s.at[g], acc_s.at[g])

    @pl.when(s == n_steps - 1)
    def _():
        new = new_ref[...]
        new_block = past // SEL_BLOCK
        osel = []
        for g in range(NSA_KV_HEADS):
            qg = q_all[g * grp_rows:(g + 1) * grp_rows]
            k_new = new[:, (2 * NSA_KV_HEADS + g) * HEAD_DIM:(2 * NSA_KV_HEADS + g + 1) * HEAD_DIM]
            v_new = new[:, (3 * NSA_KV_HEADS + g) * HEAD_DIM:(3 * NSA_KV_HEADS + g + 1) * HEAD_DIM]
            logits = _dot_nt(qg, k_new)
            q = lax.broadcasted_iota(jnp.int32, logits.shape, 0) % tnew
            k = lax.broadcasted_iota(jnp.int32, logits.shape, 1)
            chosen = jnp.concatenate([sel_ref[g][:, new_block:new_block + 1]] * NSA_REP, axis=0) > 0.5
            _online_softmax_step(logits, chosen & (k <= q) & (k < tnew), None, m_s.at[g], l_s.at[g],
                                 acc_s.at[g], v_rows=v_new)
            osel.append(acc_s[g] / l_s[g])
        wbuf = wbuf_ref[...].reshape(2 * NSA_GROUP_ROWS, wbuf_ref.shape[-1]).astype(BF16)
        wnew = wnew_ref[...]
        wb = wbuf.shape[1]
        tq = past + lax.broadcasted_iota(jnp.int32, (tnew, 1), 0)
        pos_buf = past - wb + lax.broadcasted_iota(jnp.int32, (tnew, wb), 1)
        kn = lax.broadcasted_iota(jnp.int32, (tnew, LANES), 1)
        ok = jnp.concatenate([(pos_buf <= tq) & (pos_buf > tq - WINDOW),
                              (kn < tnew) & (past + kn <= tq) & (past + kn > tq - WINDOW)], axis=1)
        gates = _sigmoid(small_ref[...])
        for h in range(NSA_HEADS):
            g, r = divmod(h, NSA_REP)
            qh_ = q_ref[h]
            k_buf = wbuf[g * HEAD_DIM:(g + 1) * HEAD_DIM]
            v_buf = wbuf[NSA_GROUP_ROWS + g * HEAD_DIM:NSA_GROUP_ROWS + (g + 1) * HEAD_DIM]
            k_new = wnew[:, g * HEAD_DIM:(g + 1) * HEAD_DIM]
            v_new = wnew[:, NSA_GROUP_ROWS + g * HEAD_DIM:NSA_GROUP_ROWS + (g + 1) * HEAD_DIM]
            logits = jnp.concatenate([_dot(qh_, k_buf), _dot_nt(qh_, k_new)], axis=1) * ATT_SCALE
            logits = jnp.where(ok, logits, NEG_INF)
            p = jnp.exp(logits - jnp.max(logits, axis=1, keepdims=True))
            pb = p.astype(BF16)
            o_w = (_dot_nt(pb[:, :wb], v_buf) + _dot(pb[:, wb:], v_new)) / jnp.sum(p, axis=1, keepdims=True)
            gl = GN_LANE + 3 * h
            o = (gates[:, gl:gl + 1] * ocmp_ref[h] + gates[:, gl + 1:gl + 2] * osel[g][r * tnew:(r + 1) * tnew]
                 + gates[:, gl + 2:gl + 3] * o_w)
            o_ref[:, h * HEAD_DIM:(h + 1) * HEAD_DIM] = o.astype(o_ref.dtype)


def nsa_sample_sel(page_table, qh, sel, ocmp, zp, nsa_new, win_view, win_new, nsa_view, layer):
    batch, n_pages = page_table.shape
    tnew = qh.shape[1] // batch
    past = n_pages * LANES
    assert past % SEL_BLOCK == 0 and tnew <= SEL_BLOCK
    n_steps = n_pages // PAGES_PER_STEP + 1
    grp_rows = NSA_REP * tnew
    full = lambda a: pl.BlockSpec((None,) + a.shape[1:], lambda b, s, pt: (b,) + (0,) * (a.ndim - 1))
    return pl.pallas_call(
        functools.partial(_nsa_sample_sel_kernel, n_steps=n_steps, past=past),
        grid_spec=pltpu.PrefetchScalarGridSpec(
            num_scalar_prefetch=1,
            grid=(batch, n_steps),
            in_specs=[pl.BlockSpec((NSA_HEADS, tnew, HEAD_DIM), lambda b, s, pt: (1, b, 0)),
                      full(sel), full(ocmp),
                      pl.BlockSpec((tnew, LANES), lambda b, s, pt: (b, ZP_SMALL // LANES)),
                      full(nsa_new),
                      pl.BlockSpec((None, None) + win_view.shape[2:], lambda b, s, pt: (layer, b, 0, 0, 0, 0)),
                      full(win_new)]
            + _page_specs((None, None, 2, NSA_KV_HEADS, HEAD_DIM, LANES), layer, n_pages, slot=1),
            out_specs=pl.BlockSpec((tnew, MIX_WIDTH), lambda b, s, pt: (b, 0)),
            scratch_shapes=[pltpu.VMEM((NSA_KV_HEADS, grp_rows, 1), F32), pltpu.VMEM((NSA_KV_HEADS, grp_rows, 1), F32),
                            pltpu.VMEM((NSA_KV_HEADS, grp_rows, HEAD_DIM), F32)]),
        out_shape=jax.ShapeDtypeStruct((batch * tnew, MIX_WIDTH), BF16),
        compiler_params=_cp("parallel", "arbitrary"),
        name="nsa_sample_sel",
    )(page_table, qh, sel, ocmp, zp, nsa_new, win_view, win_new, *([nsa_view] * PAGES_PER_STEP))


def _merge_kernel(h_ref, b0, b1, b2, b3, w_ref, g0, g1, g2, g3, o_ref):
    h = h_ref[...]
    acc = None
    for k, (b_ref, g_ref) in enumerate(zip((b0, b1, b2, b3), (g0, g1, g2, g3))):
        term = _sigmoid(_dot_nt(h, g_ref[...])) * _dot(b_ref[...], w_ref[k])
        acc = term if acc is None else acc + term
    o_ref[...] = acc.astype(o_ref.dtype)


def merge_branches(h, branches, w, wt_gate):
    m, d = h.shape
    tm = min(1024, m)
    tn = 512
    nj = D_MODEL // tn
    gate_specs = [pl.BlockSpec((tn, d), lambda i, j, k=k: (k * nj + j, 0)) for k in range(N_BRANCH)]
    return pl.pallas_call(
        _merge_kernel,
        grid=(m // tm, nj),
        in_specs=[pl.BlockSpec((tm, d), lambda i, j: (i, 0))]
        + [pl.BlockSpec((tm, MIX_WIDTH), lambda i, j: (i, 0))] * N_BRANCH
        + [pl.BlockSpec((N_BRANCH, MIX_WIDTH, tn), lambda i, j: (0, 0, j))] + gate_specs,
        out_specs=pl.BlockSpec((tm, tn), lambda i, j: (i, j)),
        out_shape=jax.ShapeDtypeStruct((m, D_MODEL), BF16),
        compiler_params=_cp("parallel", "parallel"),
        name="merge_branches",
    )(h, *branches, w, *([wt_gate] * N_BRANCH))


IN_SIZES = (MIX_WIDTH, 3 * MIX_WIDTH, IDX_HEADS * IDX_DIM, IDX_DIM, IDX_HEADS, MIX_WIDTH,
            6 * NSA_KV_HEADS * HEAD_DIM, 3 * NSA_HEADS, 3 * MIX_WIDTH, 2 * MLSTM_HEADS, MIX_WIDTH,
            N_BRANCH * D_MODEL)
(OFF_U, OFF_QKVB, OFF_QI, OFF_KI, OFF_WI, OFF_QN, OFF_KVN, OFF_GN, OFF_QKVM, OFF_GIF, OFF_OM,
 OFF_GBR) = np.concatenate([[0], np.cumsum(IN_SIZES)[:-1]]).tolist()

ZP_U, ZP_OM, ZP_Q, ZP_K, ZP_V, ZP_SMALL, ZP_WIDTH = 0, 512, 1024, 1536, 2048, 2560, 3072
Q_WIDTH = 2 * MIX_WIDTH + IDX_HEADS * IDX_DIM
Q_HEADS = Q_WIDTH // HEAD_DIM


def _block_diag(blocks):
    g, r, c = blocks.shape
    eye = jnp.eye(g, dtype=blocks.dtype)
    return (blocks[:, :, None, :] * eye[:, None, :, None]).reshape(g * r, g * c)


def _s5_discretize(lam_re, lam_im, log_dt, b_re, b_im):
    dt = jnp.exp(log_dt)[:, None]
    mag = jnp.exp(lam_re * dt)
    a_re, a_im = mag * jnp.cos(lam_im * dt), mag * jnp.sin(lam_im * dt)
    den = lam_re * lam_re + lam_im * lam_im
    nr = a_re - 1.0
    coef_re = (nr * lam_re + a_im * lam_im) / den
    coef_im = (a_im * lam_re - nr * lam_im) / den
    bb_re = coef_re[..., None] * b_re - coef_im[..., None] * b_im
    bb_im = coef_re[..., None] * b_im + coef_im[..., None] * b_re
    return a_re, a_im, bb_re, bb_im


def _prep_layer(l, p):
    wt = jnp.transpose(p['w_in'], (2, 0, 1))[:, l, :]
    seg = lambda off, n: wt[off:off + n]
    wt_q = jnp.concatenate([seg(OFF_QKVB, MIX_WIDTH), seg(OFF_QN, MIX_WIDTH),
                            seg(OFF_QI, IDX_HEADS * IDX_DIM)]).astype(BF16)
    wt_kv = jnp.concatenate([seg(OFF_QKVB + MIX_WIDTH, 2 * MIX_WIDTH), seg(OFF_KI, IDX_DIM),
                             seg(OFF_KVN, 6 * NSA_KV_HEADS * HEAD_DIM)]).astype(BF16)
    small = jnp.concatenate([seg(OFF_WI, IDX_HEADS), seg(OFF_GN, 3 * NSA_HEADS), seg(OFF_GIF, 2 * MLSTM_HEADS)])
    pad = jnp.zeros((ZP_WIDTH - ZP_SMALL - small.shape[0], D_MODEL), F32)
    wt_plain = jnp.concatenate([seg(OFF_U, MIX_WIDTH), seg(OFF_OM, MIX_WIDTH), seg(OFF_QKVM, 3 * MIX_WIDTH),
                                small, pad]).astype(BF16)
    wt_gate = seg(OFF_GBR, N_BRANCH * D_MODEL).astype(BF16)
    a_re, a_im, bb_re, bb_im = _s5_discretize(p['s5_lam_re'][l], p['s5_lam_im'][l], p['s5_log_dt'][l],
                                              p['s5_b_re'][l], p['s5_b_im'][l])
    s5_b = jnp.concatenate([_block_diag(bb_re.transpose(0, 2, 1)), _block_diag(bb_im.transpose(0, 2, 1))],
                           axis=1).astype(BF16)
    s5_c = jnp.concatenate([_block_diag(p['s5_c_re'][l].transpose(0, 2, 1)),
                            -_block_diag(p['s5_c_im'][l].transpose(0, 2, 1))], axis=0).astype(BF16)
    gate_bias = jnp.zeros((1, LANES), F32)
    gate_bias = gate_bias.at[0, IG_LANE:IG_LANE + MLSTM_HEADS].set(p['mlstm_b_i'][l])
    gate_bias = gate_bias.at[0, FG_LANE:FG_LANE + MLSTM_HEADS].set(p['mlstm_b_f'][l])
    return dict(
        norm_g=p['norm_g'][l][:, None, :],
        w_ffn1_in=p['w_ffn1_in'][l].astype(BF16), w_ffn1_out=p['w_ffn1_out'][l].astype(BF16),
        w_ffn2_in=p['w_ffn2_in'][l].astype(BF16), w_ffn2_out=p['w_ffn2_out'][l].astype(BF16),
        wt_q=wt_q, wt_kv=wt_kv, wt_plain=wt_plain, wt_gate=wt_gate,
        s5_a=jnp.stack([a_re, a_im]).reshape(2, S5_ROWS, LANES), s5_b=s5_b, s5_c=s5_c,
        s5_d=p['s5_d'][l][None, :], w_s5_glu=p['w_s5_glu'][l].astype(BF16),
        gate_bias=gate_bias, mlstm_norm_g=p['mlstm_norm_g'][l][None, :],
        w_branch=p['w_branch'][l].astype(BF16), w_out=p['w_out'][l].astype(BF16))


def _rope_tables(pos):
    inv = ROPE_THETA ** (-jnp.arange(HALF, dtype=F32) / HALF)
    ang = pos.astype(F32)[:, None] * inv[None, :]
    return jnp.cos(ang), jnp.sin(ang)


def _row_tables(cos, sin, rotated):
    one, zero = jnp.ones_like(cos), jnp.zeros_like(sin)
    c = jnp.concatenate([x for r in rotated for x in ((cos, cos) if r else (one, one))], axis=1)
    s = jnp.concatenate([x for r in rotated for x in ((-sin, sin) if r else (zero, zero))], axis=1)
    return c, s


def _ffn(x, ada, sub, g, w_in, w_out, t):
    h = modnorm(x, g, ada, sub, t, BF16)
    return mm_resid(swiglu_in(h, w_in), w_out, x, ada, 3 * sub + 2, 0.5, t)


def _s5_mixer(zp, lw, h0, batch, t):
    xs = mm(zp, lw['s5_b'])
    s, fin = s5_scan(xs.reshape(batch, t, 2, S5_ROWS, LANES), lw['s5_a'], h0)
    o = s5_out(s.reshape(batch * t, 2 * S5_N), lw['s5_c'], zp, lw['s5_d'], lw['w_s5_glu'])
    return o, fin[:, 0].reshape(batch, S5_GROUPS, S5_STATE), fin[:, 1].reshape(batch, S5_GROUPS, S5_STATE)


def _finish_layer(x, ada, lw, h, branches, t):
    merged = merge_branches(h, branches, lw['w_branch'], lw['wt_gate'])
    x = mm_resid(merged, lw['w_out'], x, ada, 5, 1.0, t)
    return _ffn(x, ada, 2, lw['norm_g'][2], lw['w_ffn2_in'], lw['w_ffn2_out'], t)


def _layer_prompt(x, ada, lw, batch, t):
    m = batch * t
    x = _ffn(x, ada, 0, lw['norm_g'][0], lw['w_ffn1_in'], lw['w_ffn1_out'], t)
    h = modnorm(x, lw['norm_g'][1], ada, 1, t, BF16)
    cos, sin = _rope_tables(jnp.arange(t, dtype=jnp.int32))
    cq, sq = _row_tables(cos, sin, (True, True))
    qr = proj_rope_rows(h, lw['wt_q'], cq, sq, t, BF16)
    qh = qr.reshape(m, Q_HEADS, HEAD_DIM).transpose(1, 0, 2)
    kv_t, ki_t, nsa_t, win_t, kv_tb, ki_tb, nsa_tb, win_tb, krow = proj_cols(h, lw['wt_kv'], cos.T, sin.T, batch, t)
    zp = mm_nt(h, lw['wt_plain'])
    o_s5, s5_re, s5_im = _s5_mixer(zp, lw, jnp.zeros((batch, 2, S5_ROWS, LANES), F32), batch, t)
    o_dsa = dsa_prompt(qh, zp, ZP_SMALL, krow, kv_tb, batch, t)
    o_nsa = nsa_prompt(qh, zp, ZP_SMALL, nsa_pool(nsa_t, batch, t), krow, nsa_tb, win_tb, batch, t)
    chunk = math.gcd(t, 256)
    o_ml, mc, mn, mm_ = mlstm(zp, lw['gate_bias'], lw['mlstm_norm_g'],
                              jnp.zeros((batch, MLSTM_HEADS, MLSTM_DK, MLSTM_DK), F32),
                              jnp.zeros((batch, 8, LANES), F32), jnp.zeros((batch, 8, LANES), F32),
                              batch, t, chunk, (ZP_Q, ZP_K, ZP_V, ZP_SMALL, ZP_OM))
    x = _finish_layer(x, ada, lw, h, [o_s5, o_dsa, o_nsa, o_ml], t)
    tokens_last = lambda a, shape: jnp.moveaxis(a.reshape((batch,) + shape + (a.shape[-1],)), -1, 1)
    wb = min(WINDOW, t)
    state = (tokens_last(kv_t, (2, DSA_HEADS, HEAD_DIM)), jnp.swapaxes(ki_t, 1, 2),
             tokens_last(nsa_t, (4, NSA_KV_HEADS, HEAD_DIM)),
             tokens_last(win_t[:, :, t - wb:], (2, NSA_KV_HEADS, HEAD_DIM)),
             mc, mn[:, :MLSTM_HEADS], mm_[:, :MLSTM_HEADS, 0], s5_re, s5_im)
    return x, state


def _layer_sample(x, ada, lw, layer, batch, t, page_table, views, past):
    m = batch * t
    kidx_view, kv_view, nsa_view, win_view = views
    past_len = page_table.shape[1] * LANES
    x = _ffn(x, ada, 0, lw['norm_g'][0], lw['w_ffn1_in'], lw['w_ffn1_out'], t)
    h = modnorm(x, lw['norm_g'][1], ada, 1, t, BF16)
    cos, sin = _rope_tables(past_len + jnp.arange(t, dtype=jnp.int32))
    cq, sq = _row_tables(cos, sin, (True, True))
    qr = proj_rope_rows(h, lw['wt_q'], cq, sq, t, BF16)
    qh = qr.reshape(m, Q_HEADS, HEAD_DIM).transpose(1, 0, 2)
    ckv, skv = _row_tables(cos, sin, KV_COLS_ROPE + (False,))
    wt_kv = jnp.concatenate([lw['wt_kv'], jnp.zeros((HEAD_DIM, D_MODEL), BF16)])
    kvr = proj_rope_rows(h, wt_kv, ckv, skv, t, F32)
    kv_rows, ki_rows = kvr[:, :2 * MIX_WIDTH], kvr[:, 2 * MIX_WIDTH:2 * MIX_WIDTH + IDX_DIM]
    nsa_rows = kvr[:, 2 * MIX_WIDTH + IDX_DIM:3 * MIX_WIDTH + IDX_DIM]
    win_rows = kvr[:, 3 * MIX_WIDTH + IDX_DIM:KV_COLS]
    zp = mm_nt(h, lw['wt_plain'])
    h0 = jnp.stack([past['s5_re'][layer], past['s5_im'][layer]], axis=1).reshape(batch, 2, S5_ROWS, LANES)
    o_s5, s5_re, s5_im = _s5_mixer(zp, lw, h0, batch, t)

    def new_rows(a):
        a = a.reshape(batch, t, a.shape[-1])
        return jnp.pad(a, ((0, 0), (0, LANES - t), (0, 0))).astype(BF16)

    scores = dsa_sample_scores(page_table, qh, zp, new_rows(ki_rows), kidx_view, layer)
    q_dsa = qr[:, :MIX_WIDTH].reshape(batch, t, DSA_HEADS, HEAD_DIM)
    qbd = jnp.einsum('bqhd,hg->bhqgd', q_dsa, jnp.eye(DSA_HEADS, dtype=BF16)).reshape(batch, DSA_HEADS * t, MIX_WIDTH)
    o_dsa = dsa_sample_attn(page_table, scores, qbd, new_rows(kv_rows), kv_view, layer)
    ocmp, sel = nsa_sample_cmp(page_table, qh, nsa_view, layer)
    o_nsa = nsa_sample_sel(page_table, qh, sel, ocmp, zp, new_rows(nsa_rows), win_view, new_rows(win_rows),
                           nsa_view, layer)
    n0 = jnp.pad(past['mlstm_n'][layer], ((0, 0), (0, 8 - MLSTM_HEADS), (0, 0)))
    m0 = jnp.pad(jnp.broadcast_to(past['mlstm_m'][layer][:, :, None], (batch, MLSTM_HEADS, LANES)),
                 ((0, 0), (0, 8 - MLSTM_HEADS), (0, 0)))
    chunk = 64 if t % 64 == 0 else t
    o_ml, mc, mn, mm_ = mlstm(zp, lw['gate_bias'], lw['mlstm_norm_g'], past['mlstm_c'][layer], n0, m0,
                              batch, t, chunk, (ZP_Q, ZP_K, ZP_V, ZP_SMALL, ZP_OM))
    x = _finish_layer(x, ada, lw, h, [o_s5, o_dsa, o_nsa, o_ml], t)
    win_buf = past['nsa_win'][layer]
    wb = win_buf.shape[1]
    win_all = jnp.concatenate([win_buf, win_rows.reshape(batch, t, 2, NSA_KV_HEADS, HEAD_DIM)], axis=1)
    state = (kv_rows.reshape(batch, t, 2, DSA_HEADS, HEAD_DIM), ki_rows.reshape(batch, t, IDX_DIM),
             nsa_rows.reshape(batch, t, 4, NSA_KV_HEADS, HEAD_DIM), win_all[:, win_all.shape[1] - wb:],
             mc, mn[:, :MLSTM_HEADS], mm_[:, :MLSTM_HEADS, 0], s5_re, s5_im)
    return x, state


def kernel(x_prompt, x_sample, cache_dsa_kv, cache_dsa_kidx, cache_nsa_kv, cache_nsa_win, state_mlstm_c,
           state_mlstm_n, state_mlstm_m, state_s5_re, state_s5_im, page_table, c_prompt, c_sample, w_ada, b_ada,
           norm_g, w_ffn1_in, w_ffn1_out, w_ffn2_in, w_ffn2_out, w_in, s5_lam_re, s5_lam_im, s5_log_dt, s5_b_re,
           s5_b_im, s5_c_re, s5_c_im, s5_d, w_s5_glu, mlstm_b_i, mlstm_b_f, mlstm_norm_g, w_branch, w_out,
           final_norm_g):
    params = dict(norm_g=norm_g, w_ffn1_in=w_ffn1_in, w_ffn1_out=w_ffn1_out, w_ffn2_in=w_ffn2_in,
                  w_ffn2_out=w_ffn2_out, w_in=w_in, s5_lam_re=s5_lam_re, s5_lam_im=s5_lam_im, s5_log_dt=s5_log_dt,
                  s5_b_re=s5_b_re, s5_b_im=s5_b_im, s5_c_re=s5_c_re, s5_c_im=s5_c_im, s5_d=s5_d,
                  w_s5_glu=w_s5_glu, mlstm_b_i=mlstm_b_i, mlstm_b_f=mlstm_b_f, mlstm_norm_g=mlstm_norm_g,
                  w_branch=w_branch, w_out=w_out)
    bp, tp, d = x_prompt.shape
    bs, ts, _ = x_sample.shape
    depth = w_ada.shape[0]
    views = (jnp.transpose(cache_dsa_kidx, (0, 1, 3, 2)), jnp.transpose(cache_dsa_kv, (0, 1, 3, 4, 5, 2)),
             jnp.transpose(cache_nsa_kv, (0, 1, 3, 4, 5, 2)), jnp.transpose(cache_nsa_win, (0, 1, 3, 4, 5, 2)))
    past = dict(nsa_win=cache_nsa_win, mlstm_c=state_mlstm_c, mlstm_n=state_mlstm_n, mlstm_m=state_mlstm_m,
                s5_re=state_s5_re, s5_im=state_s5_im)
    ada_rows = -(-(bp + bs) // 8) * 8
    c_all = jnp.pad(jnp.concatenate([c_prompt, c_sample]), ((0, ada_rows - bp - bs), (0, 0)))
    xp = x_prompt.reshape(bp * tp, d)
    xs = x_sample.reshape(bs * ts, d)
    st_p, st_s = [], []
    for l in range(depth):
        lw = _prep_layer(l, params)
        ada = ada_project(c_all, w_ada[l].astype(BF16), b_ada[l][None]).reshape(ada_rows, 9, d)
        xp, sp = _layer_prompt(xp, ada[:bp], lw, bp, tp)
        xs, ss = _layer_sample(xs, ada[bp:bp + bs], lw, l, bs, ts, page_table, views, past)
        st_p.append(sp)
        st_s.append(ss)
    g = final_norm_g[None, :]
    y_p = modnorm(xp, g, ada[:bp], None, tp, F32).reshape(bp, tp, d)
    y_s = modnorm(xs, g, ada[bp:bp + bs], None, ts, F32).reshape(bs, ts, d)
    outs = [y_p, y_s]
    for i in range(9):
        outs.append(jnp.stack([s[i] for s in st_p]))
        outs.append(jnp.stack([s[i] for s in st_s]))
    return tuple(outs)
```

```python
import functools
import math

import jax
import jax.numpy as jnp
import numpy as np
from jax import lax
from jax.experimental import pallas as pl
from jax.experimental.pallas import tpu as pltpu

F32 = jnp.float32
BF16 = jnp.bfloat16

D_MODEL = 2048
MIX_WIDTH = D_MODEL // 4
HEAD_DIM = 64
HALF = HEAD_DIM // 2
S5_GROUP = 16
S5_GROUPS = MIX_WIDTH // S5_GROUP
S5_STATE = 64
S5_N = S5_GROUPS * S5_STATE
DSA_HEADS = MIX_WIDTH // HEAD_DIM
IDX_HEADS = 4
IDX_DIM = 64
DSA_TOPK = 256
NSA_HEADS = MIX_WIDTH // HEAD_DIM
NSA_KV_HEADS = 2
NSA_REP = NSA_HEADS // NSA_KV_HEADS
CMP_STRIDE = 16
CMP_LEN = 2 * CMP_STRIDE
SEL_BLOCK = 64
SEL_TOPN = 16
CMP_PER_SEL = SEL_BLOCK // CMP_STRIDE
WINDOW = 512
MLSTM_HEADS = 4
MLSTM_DK = MIX_WIDTH // MLSTM_HEADS
D_FF = 2 * D_MODEL
ROPE_THETA = 10000.0
QUERY_BLOCK = 128
RMS_EPS = 1e-6
NEG_INF = -1e30
FORCE_SCORE = 1e4
N_BRANCH = 4

LANES = 128
VMEM_LIMIT = 56 * 1024 * 1024
HIGHEST = lax.Precision.HIGHEST


def _cp(*sem):
    return pltpu.CompilerParams(dimension_semantics=sem, vmem_limit_bytes=VMEM_LIMIT)


def _dot(a, b):
    return jnp.dot(a, b, preferred_element_type=F32)


def _dot_nt(a, b, precision=None):
    return lax.dot_general(a, b, (((1,), (1,)), ((), ())), preferred_element_type=F32,
                           precision=precision)


def _sigmoid(x):
    return 1.0 / (1.0 + jnp.exp(-x))


def _silu(x):
    return x * _sigmoid(x)


def _ada_kernel(c_ref, w_ref, b_ref, o_ref):
    c = c_ref[...]
    o_ref[...] = _dot(_silu(c).astype(BF16), w_ref[...]) + b_ref[...]


def ada_project(c, w, b):
    r, d = c.shape
    n = w.shape[1]
    tn = 2048
    return pl.pallas_call(
        _ada_kernel,
        grid=(n // tn,),
        in_specs=[pl.BlockSpec((r, d), lambda j: (0, 0)),
                  pl.BlockSpec((d, tn), lambda j: (0, j)),
                  pl.BlockSpec((1, tn), lambda j: (0, j))],
        out_specs=pl.BlockSpec((r, tn), lambda j: (0, j)),
        out_shape=jax.ShapeDtypeStruct((r, n), F32),
        compiler_params=_cp("parallel"),
        name="ada_project",
    )(c, w, b)


def _modnorm_kernel(x_ref, g_ref, ada_ref, o_ref, *, sub):
    x = x_ref[...]
    y = x * lax.rsqrt(jnp.mean(x * x, axis=-1, keepdims=True) + RMS_EPS)
    y = y * g_ref[...]
    if sub is not None:
        shift = ada_ref[3 * sub:3 * sub + 1, :]
        scale = ada_ref[3 * sub + 1:3 * sub + 2, :]
        y = y * (1.0 + scale) + shift
    o_ref[...] = y.astype(o_ref.dtype)


def modnorm(x, g, ada, sub, rows_per_batch, out_dtype):
    m, d = x.shape
    tm = min(512, rows_per_batch)
    nb = rows_per_batch // tm
    return pl.pallas_call(
        functools.partial(_modnorm_kernel, sub=sub),
        grid=(m // tm,),
        in_specs=[pl.BlockSpec((tm, d), lambda i: (i, 0)),
                  pl.BlockSpec((1, d), lambda i: (0, 0)),
                  pl.BlockSpec((None, 9, d), lambda i: (i // nb, 0, 0))],
        out_specs=pl.BlockSpec((tm, d), lambda i: (i, 0)),
        out_shape=jax.ShapeDtypeStruct((m, d), out_dtype),
        compiler_params=_cp("parallel"),
        name="modnorm",
    )(x, g, ada)


def _swiglu_in_kernel(h_ref, wa_ref, wg_ref, o_ref):
    h = h_ref[...]
    a = _dot(h, wa_ref[...])
    g = _dot(h, wg_ref[...])
    o_ref[...] = (_silu(a) * g).astype(o_ref.dtype)


def swiglu_in(h, w):
    m, d = h.shape
    f = w.shape[1] // 2
    tm = min(1024, m)
    tn = 512
    nj = f // tn
    return pl.pallas_call(
        _swiglu_in_kernel,
        grid=(m // tm, nj),
        in_specs=[pl.BlockSpec((tm, d), lambda i, j: (i, 0)),
                  pl.BlockSpec((d, tn), lambda i, j: (0, j)),
                  pl.BlockSpec((d, tn), lambda i, j: (0, j + nj))],
        out_specs=pl.BlockSpec((tm, tn), lambda i, j: (i, j)),
        out_shape=jax.ShapeDtypeStruct((m, f), BF16),
        compiler_params=_cp("parallel", "parallel"),
        name="swiglu_in",
    )(h, w, w)


def _norm_swiglu_in_kernel(x_ref, g_ref, ada_ref, wa_ref, wg_ref, o_ref, h_s, *, sub):
    @pl.when(pl.program_id(1) == 0)
    def _():
        _modnorm_kernel(x_ref, g_ref, ada_ref, h_s, sub=sub)

    _swiglu_in_kernel(h_s, wa_ref, wg_ref, o_ref)


def norm_swiglu_in(x, g, ada, sub, w, rows_per_batch):
    m, d = x.shape
    f = w.shape[1] // 2
    tm = min(1024, rows_per_batch)
    tn = 512
    nj = f // tn
    nb = rows_per_batch // tm
    return pl.pallas_call(
        functools.partial(_norm_swiglu_in_kernel, sub=sub),
        grid=(m // tm, nj),
        in_specs=[pl.BlockSpec((tm, d), lambda i, j: (i, 0)),
                  pl.BlockSpec((1, d), lambda i, j: (0, 0)),
                  pl.BlockSpec((None, 9, d), lambda i, j: (i // nb, 0, 0)),
                  pl.BlockSpec((d, tn), lambda i, j: (0, j)),
                  pl.BlockSpec((d, tn), lambda i, j: (0, j + nj))],
        out_specs=pl.BlockSpec((tm, tn), lambda i, j: (i, j)),
        out_shape=jax.ShapeDtypeStruct((m, f), BF16),
        scratch_shapes=[pltpu.VMEM((tm, d), BF16)],
        compiler_params=_cp("parallel", "arbitrary"),
        name="norm_swiglu_in",
    )(x, g, ada, w, w)


def _mm_resid_kernel(a_ref, w_ref, x_ref, ada_ref, o_ref, *, gate_row, coef):
    y = _dot(a_ref[...], w_ref[...])
    gate = ada_ref[gate_row:gate_row + 1, :]
    o_ref[...] = x_ref[...] + (coef * gate) * y


def mm_resid(a, w, x, ada, gate_row, coef, rows_per_batch):
    m, k = a.shape
    n = w.shape[1]
    tm = min(512, rows_per_batch)
    nb = rows_per_batch // tm
    return pl.pallas_call(
        functools.partial(_mm_resid_kernel, gate_row=gate_row, coef=coef),
        grid=(m // tm,),
        in_specs=[pl.BlockSpec((tm, k), lambda i: (i, 0)),
                  pl.BlockSpec((k, n), lambda i: (0, 0), pipeline_mode=pl.Buffered(1)),
                  pl.BlockSpec((tm, n), lambda i: (i, 0)),
                  pl.BlockSpec((None, 9, n), lambda i: (i // nb, 0, 0))],
        out_specs=pl.BlockSpec((tm, n), lambda i: (i, 0)),
        out_shape=jax.ShapeDtypeStruct((m, n), F32),
        compiler_params=_cp("parallel"),
        name="mm_resid",
    )(a, w, x, ada)


def _mm_kernel(a_ref, w_ref, o_ref):
    o_ref[...] = _dot(a_ref[...].astype(BF16), w_ref[...]).astype(o_ref.dtype)


def mm(a, w, out_dtype=F32, tn=2048):
    m = a.shape[0]
    k, n = w.shape
    tm = min(1024, m)
    return pl.pallas_call(
        _mm_kernel,
        grid=(m // tm, n // tn),
        in_specs=[pl.BlockSpec((tm, k), lambda i, j: (i, 0)),
                  pl.BlockSpec((k, tn), lambda i, j: (0, j))],
        out_specs=pl.BlockSpec((tm, tn), lambda i, j: (i, j)),
        out_shape=jax.ShapeDtypeStruct((m, n), out_dtype),
        compiler_params=_cp("parallel", "parallel"),
        name="mm",
    )(a, w)


def _mm_nt_kernel(a_ref, wt_ref, o_ref):
    o_ref[...] = _dot_nt(a_ref[...], wt_ref[...]).astype(o_ref.dtype)


def mm_nt(a, wt, tn=1024):
    m, k = a.shape
    n = wt.shape[0]
    tm = min(1024, m)
    return pl.pallas_call(
        _mm_nt_kernel,
        grid=(m // tm, n // tn),
        in_specs=[pl.BlockSpec((tm, k), lambda i, j: (i, 0)),
                  pl.BlockSpec((tn, k), lambda i, j: (j, 0))],
        out_specs=pl.BlockSpec((tm, tn), lambda i, j: (i, j)),
        out_shape=jax.ShapeDtypeStruct((m, n), F32),
        compiler_params=_cp("parallel", "parallel"),
        name="mm_nt",
    )(a, wt)


def _rope_rows(x, cos, sin_signed):
    w = x.shape[1]
    lane = lax.broadcasted_iota(jnp.int32, x.shape, 1)
    fwd = pltpu.roll(x, w - HALF, axis=1)
    bwd = pltpu.roll(x, HALF, axis=1)
    swapped = jnp.where((lane % HEAD_DIM) < HALF, fwd, bwd)
    return x * cos + swapped * sin_signed


def _proj_rope_rows_kernel(h_ref, wt_ref, cos_ref, sin_ref, o_ref, *, periodic):
    z = _dot_nt(h_ref[...], wt_ref[...])
    if periodic:
        reps = z.shape[1] // LANES
        cos = jnp.concatenate([cos_ref[...]] * reps, axis=1)
        sin = jnp.concatenate([sin_ref[...]] * reps, axis=1)
    else:
        cos, sin = cos_ref[...], sin_ref[...]
    o_ref[...] = _rope_rows(z, cos, sin).astype(o_ref.dtype)


def proj_rope_rows(h, wt, cos, sin, rows_per_batch, out_dtype):
    m, k = h.shape
    n = wt.shape[0]
    tm = min(512, rows_per_batch)
    nb = rows_per_batch // tm
    tw = cos.shape[1]
    return pl.pallas_call(
        functools.partial(_proj_rope_rows_kernel, periodic=(tw != n)),
        grid=(m // tm,),
        in_specs=[pl.BlockSpec((tm, k), lambda i: (i, 0)),
                  pl.BlockSpec((n, k), lambda i: (0, 0)),
                  pl.BlockSpec((tm, tw), lambda i: (i % nb, 0)),
                  pl.BlockSpec((tm, tw), lambda i: (i % nb, 0))],
        out_specs=pl.BlockSpec((tm, n), lambda i: (i, 0)),
        out_shape=jax.ShapeDtypeStruct((m, n), out_dtype),
        compiler_params=_cp("parallel"),
        name="proj_rope_rows",
    )(h, wt, cos, sin)


KV_COLS_ROPE = (True,) * 8 + (False,) * 8 + (True,) + (True, True, False, False) * 3
KV_COLS = HEAD_DIM * len(KV_COLS_ROPE)


KROW_HEADS = tuple(range(DSA_HEADS)) + (16, 21, 22)
KROW_IDX, KROW_SEL = DSA_HEADS, DSA_HEADS + 1


def _proj_cols_kernel(h_ref, wt_ref, cos_ref, sin_ref, kv_ref, ki_ref, nsa_ref, win_ref,
                      kvb_ref, kib_ref, nsab_ref, winb_ref, krow_ref):
    zt = _dot_nt(wt_ref[...], h_ref[...])
    cos, sin = cos_ref[...], sin_ref[...]
    parts = []
    for r, rot in enumerate(KV_COLS_ROPE):
        x1 = zt[r * HEAD_DIM:r * HEAD_DIM + HALF]
        x2 = zt[r * HEAD_DIM + HALF:(r + 1) * HEAD_DIM]
        if rot:
            parts += [x1 * cos - x2 * sin, x1 * sin + x2 * cos]
        else:
            parts += [x1, x2]
    out = jnp.concatenate(parts, axis=0)
    bounds = (0, 1024, 1088, 1600, 1856)
    for lo, hi, f_ref, b_ref in zip(bounds[:-1], bounds[1:], (kv_ref, ki_ref, nsa_ref, win_ref),
                                    (kvb_ref, kib_ref, nsab_ref, winb_ref)):
        f_ref[...] = out[lo:hi]
        b_ref[...] = out[lo:hi].astype(BF16)
    for j, r in enumerate(KROW_HEADS):
        krow_ref[j] = out[r * HEAD_DIM:(r + 1) * HEAD_DIM].T.astype(BF16)


def proj_cols(h, wt, cos_t, sin_t, batch, t):
    k = h.shape[1]
    tm = min(512, t)
    nt = t // tm
    widths = (1024, 64, 512, 256)
    out_shape = ([jax.ShapeDtypeStruct((batch, w, t), F32) for w in widths]
                 + [jax.ShapeDtypeStruct((batch, w, t), BF16) for w in widths]
                 + [jax.ShapeDtypeStruct((batch, len(KROW_HEADS), t, HEAD_DIM), BF16)])
    out_specs = ([pl.BlockSpec((None, w, tm), lambda b, i: (b, 0, i)) for w in widths] * 2
                 + [pl.BlockSpec((None, len(KROW_HEADS), tm, HEAD_DIM), lambda b, i: (b, 0, i, 0))])
    return pl.pallas_call(
        _proj_cols_kernel,
        grid=(batch, nt),
        in_specs=[pl.BlockSpec((tm, k), lambda b, i: (b * nt + i, 0)),
                  pl.BlockSpec((KV_COLS, k), lambda b, i: (0, 0)),
                  pl.BlockSpec((HALF, tm), lambda b, i: (0, i)),
                  pl.BlockSpec((HALF, tm), lambda b, i: (0, i))],
        out_specs=out_specs,
        out_shape=out_shape,
        compiler_params=_cp("parallel", "parallel"),
        name="proj_cols",
    )(h, wt, cos_t, sin_t)


S5_ROWS = S5_N // LANES


def _s5_scan_kernel(x_ref, a_ref, h0_ref, s_ref, fin_ref, carry_ref, *, tc):
    j = pl.program_id(1)

    @pl.when(j == 0)
    def _():
        carry_ref[...] = h0_ref[...]

    ar, ai = a_ref[0], a_ref[1]

    def step(t, carry):
        hr, hi = carry
        nr = ar * hr - ai * hi + x_ref[t, 0]
        ni = ar * hi + ai * hr + x_ref[t, 1]
        s_ref[t, 0] = nr
        s_ref[t, 1] = ni
        return nr, ni

    hr, hi = lax.fori_loop(0, tc, step, (carry_ref[0], carry_ref[1]), unroll=8)
    carry_ref[0] = hr
    carry_ref[1] = hi

    @pl.when(j == pl.num_programs(1) - 1)
    def _():
        fin_ref[0] = hr
        fin_ref[1] = hi


def s5_scan(x, a, h0):
    b, t = x.shape[:2]
    tc = min(256, t)
    blk = (None, tc, 2, S5_ROWS, LANES)
    st = (None, 2, S5_ROWS, LANES)
    return pl.pallas_call(
        functools.partial(_s5_scan_kernel, tc=tc),
        grid=(b, t // tc),
        in_specs=[pl.BlockSpec(blk, lambda i, j: (i, j, 0, 0, 0)),
                  pl.BlockSpec((2, S5_ROWS, LANES), lambda i, j: (0, 0, 0)),
                  pl.BlockSpec(st, lambda i, j: (i, 0, 0, 0))],
        out_specs=[pl.BlockSpec(blk, lambda i, j: (i, j, 0, 0, 0)),
                   pl.BlockSpec(st, lambda i, j: (i, 0, 0, 0))],
        out_shape=[jax.ShapeDtypeStruct(x.shape, F32), jax.ShapeDtypeStruct(h0.shape, F32)],
        scratch_shapes=[pltpu.VMEM((2, S5_ROWS, LANES), F32)],
        compiler_params=_cp("parallel", "arbitrary"),
        name="s5_scan",
    )(x, a, h0)


def _gelu_tanh(x):
    return 0.5 * x * (1.0 + jnp.tanh(math.sqrt(2.0 / math.pi) * (x + 0.044715 * (x * x * x))))


def _s5_out_kernel(s_ref, c_ref, u_ref, d_ref, w_ref, o_ref):
    y = _dot(s_ref[...].astype(BF16), c_ref[...]) + d_ref[...] * u_ref[...]
    y = _gelu_tanh(y)
    o_ref[...] = (y * _sigmoid(_dot(y.astype(BF16), w_ref[...]))).astype(o_ref.dtype)


def s5_out(s, cmat, zp, d, wglu):
    m = s.shape[0]
    tm = min(512, m)
    return pl.pallas_call(
        _s5_out_kernel,
        grid=(m // tm,),
        in_specs=[pl.BlockSpec((tm, 2 * S5_N), lambda i: (i, 0)),
                  pl.BlockSpec((2 * S5_N, MIX_WIDTH), lambda i: (0, 0)),
                  pl.BlockSpec((tm, MIX_WIDTH), lambda i: (i, 0)),
                  pl.BlockSpec((1, MIX_WIDTH), lambda i: (0, 0)),
                  pl.BlockSpec((MIX_WIDTH, MIX_WIDTH), lambda i: (0, 0))],
        out_specs=pl.BlockSpec((tm, MIX_WIDTH), lambda i: (i, 0)),
        out_shape=jax.ShapeDtypeStruct((m, MIX_WIDTH), BF16),
        compiler_params=_cp("parallel"),
        name="s5_out",
    )(s, cmat, zp, d, wglu)


WI_LANE = 0
GN_LANE = WI_LANE + IDX_HEADS
IG_LANE = GN_LANE + 3 * NSA_HEADS
FG_LANE = IG_LANE + MLSTM_HEADS


def _mlstm_kernel(q_ref, k_ref, v_ref, gates_ref, om_ref, bias_ref, g_ref, c0_ref, n0_ref, m0_ref,
                  o_ref, c_out, n_out, m_out, c_s, n_s, m_s, *, chunk):
    j = pl.program_id(1)

    @pl.when(j == 0)
    def _():
        c_s[...] = c0_ref[...]
        n_s[...] = n0_ref[...]
        m_s[...] = m0_ref[...]

    gates = gates_ref[...] + bias_ref[...]
    lane = lax.broadcasted_iota(jnp.int32, gates.shape, 1)
    is_f = (lane >= FG_LANE) & (lane < FG_LANE + MLSTM_HEADS)
    logsig = jnp.minimum(gates, 0.0) - jnp.log(1.0 + jnp.exp(-jnp.abs(gates)))
    gl = jnp.where(is_f, logsig, gates)
    row = lax.broadcasted_iota(jnp.int32, (chunk, chunk), 0)
    col = lax.broadcasted_iota(jnp.int32, (chunk, chunk), 1)
    causal = col <= row
    cum = jnp.dot(causal.astype(F32), gl, preferred_element_type=F32, precision=HIGHEST)
    gl_t = gl.T
    cum_t = cum.T
    scale = 1.0 / math.sqrt(MLSTM_DK)
    for h in range(MLSTM_HEADS):
        sl = slice(h * MLSTM_DK, (h + 1) * MLSTM_DK)
        q = q_ref[:, sl]
        k = k_ref[:, sl] * scale
        v = v_ref[:, sl]
        qb, kb, vb = q.astype(BF16), k.astype(BF16), v.astype(BF16)
        c = c_s[h]
        n = n_s[h:h + 1, :]
        m = m_s[h:h + 1, 0:1]
        cum_c = cum[:, FG_LANE + h:FG_LANE + h + 1]
        cum_r = cum_t[FG_LANE + h:FG_LANE + h + 1, :]
        ig_c = gl[:, IG_LANE + h:IG_LANE + h + 1]
        ig_r = gl_t[IG_LANE + h:IG_LANE + h + 1, :]
        logd = jnp.where(causal, cum_c - cum_r + ig_r, NEG_INF)
        log_state = cum_c + m
        m_t = jnp.maximum(log_state, jnp.max(logd, axis=1, keepdims=True))
        w_in = jnp.exp(logd - m_t)
        w_st = jnp.exp(log_state - m_t)
        s = _dot_nt(qb, kb) * w_in
        num = w_st * _dot(qb, c.astype(BF16)) + _dot(s.astype(BF16), vb)
        den = w_st * jnp.sum(q * n, axis=1, keepdims=True) + jnp.sum(s, axis=1, keepdims=True)
        hh = num / jnp.maximum(jnp.abs(den), jnp.exp(-m_t))
        total = cum_r[:, chunk - 1:chunk]
        m_new = jnp.maximum(total + m, jnp.max(total - cum_r + ig_r, axis=1, keepdims=True))
        a = jnp.exp(total + m - m_new)
        ws_c = jnp.exp(total - cum_c + ig_c - m_new)
        kw = k * ws_c
        c_s[h] = a * c + lax.dot_general(kw.astype(BF16), vb, (((0,), (0,)), ((), ())),
                                         preferred_element_type=F32)
        n_s[h:h + 1, :] = a * n + jnp.sum(kw, axis=0, keepdims=True)
        m_s[h:h + 1, :] = jnp.broadcast_to(m_new, (1, LANES))
        hn = hh * lax.rsqrt(jnp.mean(hh * hh, axis=1, keepdims=True) + RMS_EPS)
        o_ref[:, sl] = (hn * g_ref[:, sl] * _sigmoid(om_ref[:, sl])).astype(o_ref.dtype)

    @pl.when(j == pl.num_programs(1) - 1)
    def _():
        c_out[...] = c_s[...]
        n_out[...] = n_s[...]
        m_out[...] = m_s[...]


def mlstm(zp, bias, norm_g, c0, n0, m0, batch, t, chunk, cols):
    nc = t // chunk
    q_col, k_col, v_col, small_col, om_col = cols
    wide = lambda col: pl.BlockSpec((chunk, MIX_WIDTH), lambda b, j: (b * nc + j, col // MIX_WIDTH))
    st = lambda shape: pl.BlockSpec((None,) + shape, lambda b, j: (b,) + (0,) * len(shape))
    return pl.pallas_call(
        functools.partial(_mlstm_kernel, chunk=chunk),
        grid=(batch, nc),
        in_specs=[wide(q_col), wide(k_col), wide(v_col),
                  pl.BlockSpec((chunk, LANES), lambda b, j: (b * nc + j, small_col // LANES)),
                  wide(om_col),
                  pl.BlockSpec((1, LANES), lambda b, j: (0, 0)),
                  pl.BlockSpec((1, MIX_WIDTH), lambda b, j: (0, 0)),
                  st((MLSTM_HEADS, MLSTM_DK, MLSTM_DK)), st((8, LANES)), st((8, LANES))],
        out_specs=[pl.BlockSpec((chunk, MIX_WIDTH), lambda b, j: (b * nc + j, 0)),
                   st((MLSTM_HEADS, MLSTM_DK, MLSTM_DK)), st((8, LANES)), st((8, LANES))],
        out_shape=[jax.ShapeDtypeStruct((batch * t, MIX_WIDTH), BF16),
                   jax.ShapeDtypeStruct((batch, MLSTM_HEADS, MLSTM_DK, MLSTM_DK), F32),
                   jax.ShapeDtypeStruct((batch, 8, LANES), F32),
                   jax.ShapeDtypeStruct((batch, 8, LANES), F32)],
        scratch_shapes=[pltpu.VMEM((MLSTM_HEADS, MLSTM_DK, MLSTM_DK), F32),
                        pltpu.VMEM((8, LANES), F32), pltpu.VMEM((8, LANES), F32)],
        compiler_params=_cp("parallel", "arbitrary"),
        name="mlstm",
    )(zp, zp, zp, zp, zp, bias, norm_g, c0, n0, m0)


INT_MIN = -2 ** 31


def _count(mask):
    return jnp.sum(mask.astype(F32), axis=1, keepdims=True)


def _sort_key(score):
    bits = pltpu.bitcast(score, jnp.int32)
    key = bits ^ ((bits >> 31) & 0x7FFFFFFF)
    return jnp.where(score == 0.0, 0, key)


def _count_paged(mask):
    x = mask.astype(F32)
    pages = x.shape[0]
    if pages % 8 == 0:
        x = jnp.sum(x.reshape((pages // 8, 8) + x.shape[1:]), axis=0)
    return jnp.sum(jnp.sum(x, axis=0, keepdims=True), axis=2, keepdims=True)


def _topk_mask(score, k, paged=False):
    count = _count_paged if paged else _count
    if paged:
        idx = (lax.broadcasted_iota(jnp.int32, score.shape, 0) * LANES
               + lax.broadcasted_iota(jnp.int32, score.shape, 2))
        s = score.shape[0] * LANES
    else:
        idx = lax.broadcasted_iota(jnp.int32, score.shape, 1)
        s = score.shape[1]
    key = _sort_key(score)
    kf = float(k)
    prefix = jnp.where(count(key >= 0) >= kf, 0, INT_MIN).astype(jnp.int32)

    def value_bit(b, prefix):
        cand = prefix | lax.shift_left(jnp.int32(1), 30 - b)
        return jnp.where(count(key >= cand) >= kf, cand, prefix)

    thr = lax.fori_loop(0, 31, value_bit, prefix)
    above = key > thr
    tie = key == thr
    need = kf - count(above)
    nbits = max(1, (s - 1).bit_length())

    def index_bit(b, p):
        cand = p | lax.shift_left(jnp.int32(1), nbits - 1 - b)
        return jnp.where(count(tie & (idx < cand)) < need, cand, p)

    last = lax.fori_loop(0, nbits, index_bit, jnp.zeros_like(prefix))
    return above | (tie & (idx <= last))


ATT_SCALE = 1.0 / math.sqrt(HEAD_DIM)


KEY_CHUNK = 512
DSA_QUERY_BLOCK = 256
NSA_QUERY_BLOCK = 256


def _chunk(c, ck):
    return pl.ds(pl.multiple_of(c * ck, ck), ck)


def _fold_rows(x, op):
    r, q = x.shape
    if r % 64 == 0:
        x = op(x.reshape(8, r // 64, 8, q), axis=1)
    else:
        x = x.reshape(r // 8, 8, q)
    return op(x, axis=0)


def _col_reduce(x, op):
    return op(_fold_rows(x, op), axis=0, keepdims=True)


def _flash_work_t(n_heads, qb, ck):
    return [pltpu.VMEM((n_heads, ck, qb), F32), pltpu.VMEM((n_heads, ck, qb), BF16)]


def _flash_chunks_t(q_ref, heads, k_of, v_of, bias_s, m_s, l_s, acc_s, work, n_act, ck):
    s_s, p_s = work
    for h in heads:
        m_s[h:h + 1, :] = jnp.full((1, m_s.shape[1]), NEG_INF, F32)
        l_s[h:h + 1, :] = jnp.zeros((1, l_s.shape[1]), F32)
        acc_s[h] = jnp.zeros(acc_s.shape[1:], F32)

    def chunk_step(c, carry):
        rows = pl.ds(pl.multiple_of(c * ck, ck), ck)
        bias = bias_s[c]
        for j, h in enumerate(heads):
            s_s[j] = _dot_nt(k_of(h, rows), q_ref[h] * ATT_SCALE) + bias
        m_new = [jnp.maximum(m_s[h:h + 1, :], _col_reduce(s_s[j], jnp.max)) for j, h in enumerate(heads)]
        col_sum = []
        for j, h in enumerate(heads):
            p = jnp.exp(s_s[j] - m_new[j])
            p_s[j] = p.astype(BF16)
            col_sum.append(_col_reduce(p, jnp.sum))
        alpha = []
        for j, h in enumerate(heads):
            alpha.append(jnp.exp(m_s[h:h + 1, :] - m_new[j]))
            l_s[h:h + 1, :] = alpha[j] * l_s[h:h + 1, :] + col_sum[j]
            m_s[h:h + 1, :] = m_new[j]
        for j, h in enumerate(heads):
            acc_s[h] = alpha[j] * acc_s[h] + _dot(v_of(h, c), p_s[j])
        return carry

    lax.fori_loop(0, n_act, chunk_step, 0)
    for h in heads:
        acc_s[h] = acc_s[h] / l_s[h:h + 1, :]


def _dsa_prompt_kernel(q_ref, qi_ref, small_ref, ki_ref, k_ref, v_ref, o_ref, key_s, bias_s, acc_s, m_s, l_s,
                       s_s, p_s, *, topk, ck):
    i = pl.program_id(1)
    nch, _, qb = key_s.shape
    t = nch * ck
    n_act = (i * qb + qb - 1) // ck + 1
    tq = i * qb + lax.broadcasted_iota(jnp.int32, (1, qb), 1)
    krow = lax.broadcasted_iota(jnp.int32, (ck, qb), 0)
    kf = float(topk)
    small_t = small_ref[...].T
    wi = [small_t[WI_LANE + h:WI_LANE + h + 1, :] for h in range(IDX_HEADS)]

    def score_chunk(c, carry):
        ki = ki_ref[pl.ds(pl.multiple_of(c * ck, ck), ck), :]
        score = None
        for h in range(IDX_HEADS):
            term = wi[h] * jnp.maximum(_dot_nt(ki, qi_ref[h]), 0.0)
            score = term if score is None else score + term
        key_s[c] = _sort_key(jnp.where(c * ck + krow <= tq, score, NEG_INF))
        return carry

    lax.fori_loop(0, n_act, score_chunk, 0)

    def count_keys(pred):
        def body(c, acc):
            return acc + _fold_rows(pred(key_s[c], c).astype(F32), jnp.sum)
        return jnp.sum(lax.fori_loop(0, n_act, body, jnp.zeros((8, qb), F32)), axis=0, keepdims=True)

    prefix = jnp.where(count_keys(lambda k, c: k >= 0) >= kf, 0, INT_MIN).astype(jnp.int32)

    def value_bit(b, prefix):
        cand = prefix | lax.shift_left(jnp.int32(1), 30 - b)
        return jnp.where(count_keys(lambda k, c: k >= cand) >= kf, cand, prefix)

    thr = lax.fori_loop(0, 31, value_bit, prefix)
    need = kf - count_keys(lambda k, c: k > thr)
    n_tie = count_keys(lambda k, c: k == thr)
    nbits = max(1, (t - 1).bit_length())

    def resolve_ties():
        def index_bit(b, p):
            cand = p | lax.shift_left(jnp.int32(1), nbits - 1 - b)
            below = count_keys(lambda k, c: (k == thr) & (c * ck + krow < cand))
            return jnp.where(below < need, cand, p)
        return lax.fori_loop(0, nbits, index_bit, jnp.zeros((1, qb), jnp.int32))

    last = lax.cond(jnp.max(n_tie - need) > 0.0, resolve_ties, lambda: jnp.full((1, qb), t, jnp.int32))

    def bias_chunk(c, carry):
        k = key_s[c]
        kpos = c * ck + krow
        keep = ((k > thr) | ((k == thr) & (kpos <= last))) & (kpos <= tq)
        bias_s[c] = jnp.where(keep, 0.0, NEG_INF)
        return carry

    lax.fori_loop(0, n_act, bias_chunk, 0)

    _flash_chunks_t(q_ref, range(DSA_HEADS), lambda h, rows: k_ref[h, rows, :],
                    lambda h, c: v_ref[h * HEAD_DIM:(h + 1) * HEAD_DIM, _chunk(c, ck)],
                    bias_s, m_s, l_s, acc_s, (s_s, p_s), n_act, ck)
    o_ref[...] = acc_s[...].reshape(DSA_HEADS * HEAD_DIM, qb).T.astype(o_ref.dtype)


def dsa_prompt(qh, zp, small_col, krow, kv_t, batch, t):
    qb = min(DSA_QUERY_BLOCK, t)
    nq = t // qb
    topk = min(DSA_TOPK, t // 4)
    ck = min(KEY_CHUNK, t)
    nch = t // ck
    assert t % ck == 0 and ck >= topk and ck % qb == 0
    return pl.pallas_call(
        functools.partial(_dsa_prompt_kernel, topk=topk, ck=ck),
        grid=(batch, nq),
        in_specs=[pl.BlockSpec((DSA_HEADS, qb, HEAD_DIM), lambda b, i: (0, b * nq + i, 0)),
                  pl.BlockSpec((IDX_HEADS, qb, HEAD_DIM), lambda b, i: (4, b * nq + i, 0)),
                  pl.BlockSpec((qb, LANES), lambda b, i: (b * nq + i, small_col // LANES)),
                  pl.BlockSpec((None, None, t, HEAD_DIM), lambda b, i: (b, KROW_IDX, 0, 0)),
                  pl.BlockSpec((None, DSA_HEADS, t, HEAD_DIM), lambda b, i: (b, 0, 0, 0)),
                  pl.BlockSpec((None, MIX_WIDTH, t), lambda b, i: (b, 1, 0))],
        out_specs=pl.BlockSpec((qb, MIX_WIDTH), lambda b, i: (b * nq + i, 0)),
        out_shape=jax.ShapeDtypeStruct((batch * t, MIX_WIDTH), BF16),
        scratch_shapes=[pltpu.VMEM((nch, ck, qb), jnp.int32), pltpu.VMEM((nch, ck, qb), F32),
                        pltpu.VMEM((DSA_HEADS, HEAD_DIM, qb), F32),
                        pltpu.VMEM((DSA_HEADS, qb), F32), pltpu.VMEM((DSA_HEADS, qb), F32)]
        + _flash_work_t(DSA_HEADS, qb, ck),
        compiler_params=_cp("parallel", "arbitrary"),
        name="dsa_prompt",
    )(qh, qh, zp, krow, krow, kv_t)


NSA_GROUP_ROWS = NSA_KV_HEADS * HEAD_DIM


def _nsa_pool_kernel(x_ref, o_ref):
    t = x_ref.shape[1]
    nsub = t // CMP_STRIDE
    tok = lax.broadcasted_iota(jnp.int32, (t, nsub), 0)
    c = lax.broadcasted_iota(jnp.int32, (t, nsub), 1)
    inside = (tok >= c * CMP_STRIDE) & (tok < c * CMP_STRIDE + CMP_LEN) & (c < nsub - 1)
    pool = jnp.where(inside, 1.0 / CMP_LEN, 0.0).astype(F32)
    o_ref[...] = jnp.dot(x_ref[...], pool, preferred_element_type=F32, precision=HIGHEST).astype(o_ref.dtype)


def nsa_pool(nsa_t, batch, t):
    rows = 2 * NSA_GROUP_ROWS
    return pl.pallas_call(
        _nsa_pool_kernel,
        grid=(batch,),
        in_specs=[pl.BlockSpec((None, rows, t), lambda b: (b, 0, 0))],
        out_specs=pl.BlockSpec((None, rows, t // CMP_STRIDE), lambda b: (b, 0, 0)),
        out_shape=jax.ShapeDtypeStruct((batch, rows, t // CMP_STRIDE), BF16),
        compiler_params=_cp("parallel"),
        name="nsa_pool",
    )(nsa_t)


def _nsa_select_blocks(pcsum, tq, ns, n_sel, lanes=None):
    rows, nc = pcsum.shape
    lanes = ns if lanes is None else lanes
    c = lax.broadcasted_iota(jnp.int32, (nc, lanes), 0)
    j = lax.broadcasted_iota(jnp.int32, (nc, lanes), 1)
    pool = ((c >= j * CMP_PER_SEL) & (c < (j + 1) * CMP_PER_SEL)).astype(F32)
    imp = jnp.dot(pcsum, pool, preferred_element_type=F32, precision=HIGHEST)
    blk = lax.broadcasted_iota(jnp.int32, (rows, lanes), 1)
    forced = (blk == tq // SEL_BLOCK) | (blk == 0)
    imp = jnp.where(forced, FORCE_SCORE, imp)
    imp = jnp.where((blk * SEL_BLOCK <= tq) & (blk < ns), imp, NEG_INF)
    return _topk_mask(imp, n_sel) & (blk < ns)


def _top_rows(v, n):
    ns = v.shape[0]
    row = lax.broadcasted_iota(jnp.int32, v.shape, 0)
    rank = jnp.zeros(v.shape, F32)
    for i in range(ns):
        vi = v[i:i + 1, :]
        rank = rank + ((vi > v) | ((vi == v) & (row > i))).astype(F32)
    return rank < float(n)


def _nsa_prompt_kernel(q_ref, small_ref, cm_ref, ksel0_ref, ksel1_ref, vsel_ref, *rest, ns, n_sel, ck, nwin):
    wins, o_ref = rest[:nwin], rest[nwin]
    bias_s, ocmp_s, osel_s, m_s, l_s, s_s, p_s = rest[nwin + 1:]
    i = pl.program_id(1)
    qb = bias_s.shape[2]
    n_act = (i * qb + qb - 1) // ck + 1
    tq_row = i * qb + lax.broadcasted_iota(jnp.int32, (ns, qb), 1)
    tq1 = tq_row[0:1, :]
    blk = lax.broadcasted_iota(jnp.int32, (ns, qb), 0)
    ncp = cm_ref.shape[1]
    cidx = lax.broadcasted_iota(jnp.int32, (ncp, qb), 0)
    cvalid = (cidx * CMP_STRIDE + CMP_LEN - 1 <= tq1) & (cidx < ncp - 1)
    for g in range(NSA_KV_HEADS):
        grows = slice(g * HEAD_DIM, (g + 1) * HEAD_DIM)
        vrows = slice(NSA_GROUP_ROWS + g * HEAD_DIM, NSA_GROUP_ROWS + (g + 1) * HEAD_DIM)
        kcm = cm_ref[grows, :].astype(F32).T.astype(BF16)
        vcm = cm_ref[vrows, :]
        pcsum = jnp.zeros((ncp, qb), F32)
        for r in range(NSA_REP):
            h = g * NSA_REP + r
            lc = jnp.where(cvalid, _dot_nt(kcm, q_ref[h] * ATT_SCALE), NEG_INF)
            p = jnp.exp(lc - _col_reduce(lc, jnp.max))
            pc = jnp.where(cvalid, p / _col_reduce(p, jnp.sum), 0.0)
            ocmp_s[h] = _dot(vcm, pc.astype(BF16))
            pcsum = pcsum + pc
        pj = lax.broadcasted_iota(jnp.int32, (ns, ncp), 0)
        pc_ = lax.broadcasted_iota(jnp.int32, (ns, ncp), 1)
        pool = ((pc_ >= pj * CMP_PER_SEL) & (pc_ < (pj + 1) * CMP_PER_SEL)).astype(F32)
        imp = jnp.dot(pool, pcsum, preferred_element_type=F32, precision=HIGHEST)
        imp = jnp.where((blk == tq_row // SEL_BLOCK) | (blk == 0), FORCE_SCORE, imp)
        imp = jnp.where(blk * SEL_BLOCK <= tq_row, imp, NEG_INF)
        sel = _top_rows(imp, n_sel).astype(BF16)

        def bias_chunk(c, carry, sel=sel):
            tk = c * ck + lax.broadcasted_iota(jnp.int32, (ck, ns), 0)
            bj = lax.broadcasted_iota(jnp.int32, (ck, ns), 1)
            expand = (tk // SEL_BLOCK == bj).astype(BF16)
            kpos = c * ck + lax.broadcasted_iota(jnp.int32, (ck, qb), 0)
            keep = (_dot(expand, sel) > 0.5) & (kpos <= tq1)
            bias_s[c] = jnp.where(keep, 0.0, NEG_INF)
            return carry

        lax.fori_loop(0, n_act, bias_chunk, 0)

        ksel_ref = (ksel0_ref, ksel1_ref)[g]
        _flash_chunks_t(q_ref, range(g * NSA_REP, (g + 1) * NSA_REP), lambda h, rows, k=ksel_ref: k[rows, :],
                        lambda h, c, grows=grows: vsel_ref[grows, _chunk(c, ck)],
                        bias_s, m_s, l_s, osel_s, (s_s, p_s), n_act, ck)

    kw = jnp.concatenate([w[...] for w in wins], axis=1)
    wpos = (i - (nwin - 1)) * qb + lax.broadcasted_iota(jnp.int32, (nwin * qb, qb), 0)
    wok = (wpos >= 0) & (wpos <= tq1) & (wpos > tq1 - WINDOW)
    gates = _sigmoid(small_ref[...]).T
    for g in range(NSA_KV_HEADS):
        kwin = kw[g * HEAD_DIM:(g + 1) * HEAD_DIM].astype(F32).T.astype(BF16)
        vwin = kw[NSA_GROUP_ROWS + g * HEAD_DIM:NSA_GROUP_ROWS + (g + 1) * HEAD_DIM]
        for r in range(NSA_REP):
            h = g * NSA_REP + r
            lw_ = jnp.where(wok, _dot_nt(kwin, q_ref[h] * ATT_SCALE), NEG_INF)
            p = jnp.exp(lw_ - _col_reduce(lw_, jnp.max))
            o_w = _dot(vwin, p.astype(BF16)) / _col_reduce(p, jnp.sum)
            gl = GN_LANE + 3 * h
            ocmp_s[h] = (gates[gl:gl + 1, :] * ocmp_s[h] + gates[gl + 1:gl + 2, :] * osel_s[h]
                         + gates[gl + 2:gl + 3, :] * o_w)
    o_ref[...] = ocmp_s[...].reshape(NSA_HEADS * HEAD_DIM, qb).T.astype(o_ref.dtype)


def nsa_prompt(qh, zp, small_col, cm_t, krow, nsa_t, win_t, batch, t):
    qb = min(NSA_QUERY_BLOCK, t)
    nq = t // qb
    ns = -(-t // SEL_BLOCK)
    nwin = -(-WINDOW // qb) + 1
    win_specs = [pl.BlockSpec((None, 2 * NSA_GROUP_ROWS, qb),
                              lambda b, i, j=j: (b, 0, jnp.maximum(i - (nwin - 1) + j, 0)))
                 for j in range(nwin)]
    ck = min(KEY_CHUNK, t)
    assert t % ck == 0 and ck % qb == 0 and t % SEL_BLOCK == 0
    return pl.pallas_call(
        functools.partial(_nsa_prompt_kernel, ns=ns, n_sel=min(SEL_TOPN, ns), ck=ck, nwin=nwin),
        grid=(batch, nq),
        in_specs=[pl.BlockSpec((NSA_HEADS, qb, HEAD_DIM), lambda b, i: (1, b * nq + i, 0)),
                  pl.BlockSpec((qb, LANES), lambda b, i: (b * nq + i, small_col // LANES)),
                  pl.BlockSpec((None, 2 * NSA_GROUP_ROWS, cm_t.shape[2]), lambda b, i: (b, 0, 0)),
                  pl.BlockSpec((None, None, t, HEAD_DIM), lambda b, i: (b, KROW_SEL, 0, 0)),
                  pl.BlockSpec((None, None, t, HEAD_DIM), lambda b, i: (b, KROW_SEL + 1, 0, 0)),
                  pl.BlockSpec((None, NSA_GROUP_ROWS, t), lambda b, i: (b, 3, 0))] + win_specs,
        out_specs=pl.BlockSpec((qb, MIX_WIDTH), lambda b, i: (b * nq + i, 0)),
        out_shape=jax.ShapeDtypeStruct((batch * t, MIX_WIDTH), BF16),
        scratch_shapes=[pltpu.VMEM((t // ck, ck, qb), F32), pltpu.VMEM((NSA_HEADS, HEAD_DIM, qb), F32),
                        pltpu.VMEM((NSA_HEADS, HEAD_DIM, qb), F32),
                        pltpu.VMEM((NSA_HEADS, qb), F32), pltpu.VMEM((NSA_HEADS, qb), F32)]
        + _flash_work_t(NSA_REP, qb, ck),
        compiler_params=_cp("parallel", "arbitrary"),
        name="nsa_prompt",
    )(qh, zp, cm_t, krow, krow, nsa_t, *([win_t] * nwin))


PAGES_PER_STEP = 8


def _page_specs(block, layer, n_pages, slot=None, per_step=PAGES_PER_STEP):
    def spec(j):
        def index(b, s, pt):
            page = pt[b, jnp.minimum(s * per_step + j, n_pages - 1)]
            lead = (layer, page) if slot is None else (layer, page, slot)
            return lead + (0,) * (len(block) - len(lead))
        return pl.BlockSpec(block, index)
    return [spec(j) for j in range(per_step)]


SCORE_PAGES_PER_STEP = 32


def _dsa_sample_scores_kernel(pt_ref, qi_ref, small_ref, kinew_ref, *rest, n_steps):
    pages, o_ref = rest[:SCORE_PAGES_PER_STEP], rest[SCORE_PAGES_PER_STEP]
    s = pl.program_id(1)
    tnew = qi_ref.shape[1]

    def score(keys_t=None, keys=None):
        acc = None
        for h in range(IDX_HEADS):
            d = _dot(qi_ref[h], keys_t) if keys is None else _dot_nt(qi_ref[h], keys)
            term = small_ref[:, WI_LANE + h:WI_LANE + h + 1] * jnp.maximum(d, 0.0)
            acc = term if acc is None else acc + term
        return acc

    @pl.when(s < n_steps - 1)
    def _():
        for j, page in enumerate(pages):
            o_ref[j] = score(keys_t=page[...].astype(BF16))

    @pl.when(s == n_steps - 1)
    def _():
        sc = score(keys=kinew_ref[...])
        q = lax.broadcasted_iota(jnp.int32, sc.shape, 0)
        k = lax.broadcasted_iota(jnp.int32, sc.shape, 1)
        o_ref[0] = jnp.where((k <= q) & (k < tnew), sc, NEG_INF)
        for j in range(1, SCORE_PAGES_PER_STEP):
            o_ref[j] = jnp.full(sc.shape, NEG_INF, F32)


def dsa_sample_scores(page_table, qh, zp, ki_new, kidx_view, layer):
    batch, n_pages = page_table.shape
    tnew = qh.shape[1] // batch
    assert n_pages % SCORE_PAGES_PER_STEP == 0
    n_steps = n_pages // SCORE_PAGES_PER_STEP + 1
    return pl.pallas_call(
        functools.partial(_dsa_sample_scores_kernel, n_steps=n_steps),
        grid_spec=pltpu.PrefetchScalarGridSpec(
            num_scalar_prefetch=1,
            grid=(batch, n_steps),
            in_specs=[pl.BlockSpec((IDX_HEADS, tnew, HEAD_DIM), lambda b, s, pt: (4, b, 0)),
                      pl.BlockSpec((tnew, LANES), lambda b, s, pt: (b, ZP_SMALL // LANES)),
                      pl.BlockSpec((None, LANES, IDX_DIM), lambda b, s, pt: (b, 0, 0))]
            + _page_specs((None, None, IDX_DIM, LANES), layer, n_pages, per_step=SCORE_PAGES_PER_STEP),
            out_specs=pl.BlockSpec((None, SCORE_PAGES_PER_STEP, tnew, LANES), lambda b, s, pt: (b, s, 0, 0))),
        out_shape=jax.ShapeDtypeStruct((batch, SCORE_PAGES_PER_STEP * n_steps, tnew, LANES), F32),
        compiler_params=_cp("parallel", "arbitrary"),
        name="dsa_sample_scores",
    )(page_table, qh, zp, ki_new, *([kidx_view] * SCORE_PAGES_PER_STEP))


def _online_softmax_step(logits, keep, v_t, m_ref, l_ref, acc_ref, v_rows=None):
    lm = jnp.where(keep, logits, NEG_INF)
    m_old = m_ref[...]
    m_new = jnp.maximum(m_old, jnp.max(lm, axis=1, keepdims=True))
    alpha = jnp.exp(m_old - m_new)
    p = jnp.where(keep, jnp.exp(lm - m_new), 0.0)
    pv = _dot_nt(p.astype(BF16), v_t) if v_rows is None else _dot(p.astype(BF16), v_rows)
    l_ref[...] = alpha * l_ref[...] + jnp.sum(p, axis=1, keepdims=True)
    acc_ref[...] = alpha * acc_ref[...] + pv
    m_ref[...] = m_new


def _dsa_sample_attn_kernel(pt_ref, score_ref, qbd_ref, kvnew_ref, *rest, n_steps, topk):
    pages, o_ref = rest[:PAGES_PER_STEP], rest[PAGES_PER_STEP]
    keep_s, m_s, l_s, acc_s = rest[PAGES_PER_STEP + 1:]
    s = pl.program_id(1)
    tnew = score_ref.shape[1]

    @pl.when(s == 0)
    def _():
        keep_s[...] = _topk_mask(score_ref[...], topk, paged=True).astype(F32)
        m_s[...] = jnp.full(m_s.shape, NEG_INF, F32)
        l_s[...] = jnp.zeros(l_s.shape, F32)
        acc_s[...] = jnp.zeros(acc_s.shape, F32)

    qbd = qbd_ref[...] * ATT_SCALE

    def keep_rows(page):
        return jnp.concatenate([keep_s[page]] * DSA_HEADS, axis=0) > 0.5

    @pl.when(s < n_steps - 1)
    def _():
        k_t = jnp.concatenate([p[0].reshape(MIX_WIDTH, LANES).astype(BF16) for p in pages], axis=1)
        v_t = jnp.concatenate([p[1].reshape(MIX_WIDTH, LANES).astype(BF16) for p in pages], axis=1)
        keep = jnp.concatenate([keep_rows(s * PAGES_PER_STEP + j) for j in range(PAGES_PER_STEP)], axis=1)
        _online_softmax_step(_dot(qbd, k_t), keep, v_t, m_s, l_s, acc_s)

    @pl.when(s == n_steps - 1)
    def _():
        kv = kvnew_ref[...]
        logits = _dot_nt(qbd, kv[:, :MIX_WIDTH])
        q = lax.broadcasted_iota(jnp.int32, logits.shape, 0) % tnew
        k = lax.broadcasted_iota(jnp.int32, logits.shape, 1)
        keep = keep_rows((n_steps - 1) * PAGES_PER_STEP) & (k <= q) & (k < tnew)
        _online_softmax_step(logits, keep, None, m_s, l_s, acc_s, v_rows=kv[:, MIX_WIDTH:])
        out = acc_s[...] / l_s[...]
        for h in range(DSA_HEADS):
            o_ref[:, h * HEAD_DIM:(h + 1) * HEAD_DIM] = (
                out[h * tnew:(h + 1) * tnew, h * HEAD_DIM:(h + 1) * HEAD_DIM].astype(o_ref.dtype))


def dsa_sample_attn(page_table, scores, qbd, kv_new, kv_view, layer):
    batch, n_pages = page_table.shape
    tnew = scores.shape[2]
    n_steps = n_pages // PAGES_PER_STEP + 1
    topk = min(DSA_TOPK, (n_pages * LANES + tnew) // 4)
    rows = DSA_HEADS * tnew
    return pl.pallas_call(
        functools.partial(_dsa_sample_attn_kernel, n_steps=n_steps, topk=topk),
        grid_spec=pltpu.PrefetchScalarGridSpec(
            num_scalar_prefetch=1,
            grid=(batch, n_steps),
            in_specs=[pl.BlockSpec((None,) + scores.shape[1:], lambda b, s, pt: (b, 0, 0, 0)),
                      pl.BlockSpec((None, rows, MIX_WIDTH), lambda b, s, pt: (b, 0, 0)),
                      pl.BlockSpec((None, LANES, 2 * MIX_WIDTH), lambda b, s, pt: (b, 0, 0))]
            + _page_specs((None, None, 2, DSA_HEADS, HEAD_DIM, LANES), layer, n_pages),
            out_specs=pl.BlockSpec((tnew, MIX_WIDTH), lambda b, s, pt: (b, 0)),
            scratch_shapes=[pltpu.VMEM(scores.shape[1:], F32), pltpu.VMEM((rows, 1), F32),
                            pltpu.VMEM((rows, 1), F32), pltpu.VMEM((rows, MIX_WIDTH), F32)]),
        out_shape=jax.ShapeDtypeStruct((batch * tnew, MIX_WIDTH), BF16),
        compiler_params=_cp("parallel", "arbitrary"),
        name="dsa_sample_attn",
    )(page_table, scores, qbd, kv_new, *([kv_view] * PAGES_PER_STEP))


CMP_PAGES_PER_STEP = 16


def _nsa_sample_cmp_kernel(pt_ref, q_ref, *rest, n_steps, past, ns, n_sel):
    pages = rest[:CMP_PAGES_PER_STEP]
    ocmp_ref, sel_ref, sub_s = rest[CMP_PAGES_PER_STEP:]
    s = pl.program_id(1)
    tnew = q_ref.shape[1]
    rows = 2 * NSA_GROUP_ROWS
    n_tok = CMP_PAGES_PER_STEP * LANES
    tok = lax.broadcasted_iota(jnp.int32, (n_tok, LANES), 0)
    col = lax.broadcasted_iota(jnp.int32, (n_tok, LANES), 1)
    pool = jnp.where(col == tok // CMP_STRIDE, 1.0 / CMP_STRIDE, 0.0).astype(BF16)
    x = jnp.concatenate([page[...].reshape(rows, LANES) for page in pages], axis=1)
    x_hi = x.astype(BF16)
    r1 = x - x_hi.astype(F32)
    x_mid = r1.astype(BF16)
    x_lo = (r1 - x_mid.astype(F32)).astype(BF16)
    sub_s[s] = _dot(x_hi, pool) + _dot(x_mid, pool) + _dot(x_lo, pool)

    @pl.when(s == n_steps - 1)
    def _():
        sub_all = jnp.concatenate([sub_s[i] for i in range(n_steps)], axis=1)
        ncp = sub_all.shape[1]
        cm = (0.5 * (sub_all + pltpu.roll(sub_all, ncp - 1, axis=1))).astype(BF16)
        nc = (past + tnew) // CMP_STRIDE - 1
        tq = past + lax.broadcasted_iota(jnp.int32, (tnew, 1), 0)
        cidx = lax.broadcasted_iota(jnp.int32, (tnew, ncp), 1)
        cvalid = (cidx * CMP_STRIDE + CMP_LEN - 1 <= tq) & (cidx < nc)
        for g in range(NSA_KV_HEADS):
            kcm = cm[g * HEAD_DIM:(g + 1) * HEAD_DIM]
            vcm = cm[NSA_GROUP_ROWS + g * HEAD_DIM:NSA_GROUP_ROWS + (g + 1) * HEAD_DIM]
            pcsum = jnp.zeros((tnew, ncp), F32)
            for r in range(NSA_REP):
                h = g * NSA_REP + r
                lc = jnp.where(cvalid, _dot(q_ref[h], kcm) * ATT_SCALE, NEG_INF)
                p = jnp.exp(lc - jnp.max(lc, axis=1, keepdims=True))
                pc = jnp.where(cvalid, p / jnp.sum(p, axis=1, keepdims=True), 0.0)
                ocmp_ref[h] = _dot_nt(pc.astype(BF16), vcm)
                pcsum = pcsum + pc
            sel_ref[g] = _nsa_select_blocks(pcsum, tq, ns, n_sel, lanes=sel_ref.shape[-1]).astype(F32)


def nsa_sample_cmp(page_table, qh, nsa_view, layer):
    batch, n_pages = page_table.shape
    tnew = qh.shape[1] // batch
    past = n_pages * LANES
    assert tnew < CMP_STRIDE and n_pages % CMP_PAGES_PER_STEP == 0
    n_steps = n_pages // CMP_PAGES_PER_STEP
    ns = -(-(past + tnew) // SEL_BLOCK)
    ns_lanes = -(-ns // LANES) * LANES
    return pl.pallas_call(
        functools.partial(_nsa_sample_cmp_kernel, n_steps=n_steps, past=past, ns=ns, n_sel=min(SEL_TOPN, ns)),
        grid_spec=pltpu.PrefetchScalarGridSpec(
            num_scalar_prefetch=1,
            grid=(batch, n_steps),
            in_specs=[pl.BlockSpec((NSA_HEADS, tnew, HEAD_DIM), lambda b, s, pt: (1, b, 0))]
            + _page_specs((None, None, 2, NSA_KV_HEADS, HEAD_DIM, LANES), layer, n_pages, slot=0,
                          per_step=CMP_PAGES_PER_STEP),
            out_specs=[pl.BlockSpec((None, NSA_HEADS, tnew, HEAD_DIM), lambda b, s, pt: (b, 0, 0, 0)),
                       pl.BlockSpec((None, NSA_KV_HEADS, tnew, ns_lanes), lambda b, s, pt: (b, 0, 0, 0))],
            scratch_shapes=[pltpu.VMEM((n_steps, 2 * NSA_GROUP_ROWS, LANES), F32)]),
        out_shape=[jax.ShapeDtypeStruct((batch, NSA_HEADS, tnew, HEAD_DIM), F32),
                   jax.ShapeDtypeStruct((batch, NSA_KV_HEADS, tnew, ns_lanes), F32)],
        compiler_params=_cp("parallel", "arbitrary"),
        name="nsa_sample_cmp",
    )(page_table, qh, *([nsa_view] * CMP_PAGES_PER_STEP))


def _nsa_sample_sel_kernel(pt_ref, q_ref, sel_ref, ocmp_ref, small_ref, new_ref, wbuf_ref, wnew_ref, *rest,
                           n_steps, past):
    pages, o_ref = rest[:PAGES_PER_STEP], rest[PAGES_PER_STEP]
    m_s, l_s, acc_s = rest[PAGES_PER_STEP + 1:]
    s = pl.program_id(1)
    tnew = q_ref.shape[1]
    grp_rows = NSA_REP * tnew

    @pl.when(s == 0)
    def _():
        m_s[...] = jnp.full(m_s.shape, NEG_INF, F32)
        l_s[...] = jnp.zeros(l_s.shape, F32)
        acc_s[...] = jnp.zeros(acc_s.shape, F32)

    q_all = q_ref[...].reshape(NSA_HEADS * tnew, HEAD_DIM) * ATT_SCALE

    @pl.when(s < n_steps - 1)
    def _():
        step_tokens = PAGES_PER_STEP * LANES
        nsl = sel_ref.shape[2]
        bj = lax.broadcasted_iota(jnp.int32, (nsl, step_tokens), 0)
        tk = lax.broadcasted_iota(jnp.int32, (nsl, step_tokens), 1)
        expand = (bj == s * (step_tokens // SEL_BLOCK) + tk // SEL_BLOCK).astype(BF16)
        for g in range(NSA_KV_HEADS):
            keep_g = _dot(sel_ref[g].astype(BF16), expand) > 0.5
            keep = jnp.concatenate([keep_g] * NSA_REP, axis=0)
            qg = q_all[g * grp_rows:(g + 1) * grp_rows]
            k_t = jnp.concatenate([p[0, g].astype(BF16) for p in pages], axis=1)
            v_t = jnp.concatenate([p[1, g].astype(BF16) for p in pages], axis=1)
            _online_softmax_step(_dot(qg, k_t), keep, v_t, m_s.at[g], l_s.at[g], acc_s.at[g])

    @pl.when(s == n_steps - 1)
    def _():
        new = new_ref[...]
        new_block = past // SEL_BLOCK
        osel = []
        for g in range(NSA_KV_HEADS):
            qg = q_all[g * grp_rows:(g + 1) * grp_rows]
            k_new = new[:, (2 * NSA_KV_HEADS + g) * HEAD_DIM:(2 * NSA_KV_HEADS + g + 1) * HEAD_DIM]
            v_new = new[:, (3 * NSA_KV_HEADS + g) * HEAD_DIM:(3 * NSA_KV_HEADS + g + 1) * HEAD_DIM]
            logits = _dot_nt(qg, k_new)
            q = lax.broadcasted_iota(jnp.int32, logits.shape, 0) % tnew
            k = lax.broadcasted_iota(jnp.int32, logits.shape, 1)
            chosen = jnp.concatenate([sel_ref[g][:, new_block:new_block + 1]] * NSA_REP, axis=0) > 0.5
            _online_softmax_step(logits, chosen & (k <= q) & (k < tnew), None, m_s.at[g], l_s.at[g],
                                 acc_s.at[g], v_rows=v_new)
            osel.append(acc_s[g] / l_s[g])
        wbuf = wbuf_ref[...].reshape(2 * NSA_GROUP_ROWS, wbuf_ref.shape[-1]).astype(BF16)
        wnew = wnew_ref[...]
        wb = wbuf.shape[1]
        tq = past + lax.broadcasted_iota(jnp.int32, (tnew, 1), 0)
        pos_buf = past - wb + lax.broadcasted_iota(jnp.int32, (tnew, wb), 1)
        kn = lax.broadcasted_iota(jnp.int32, (tnew, LANES), 1)
        ok = jnp.concatenate([(pos_buf <= tq) & (pos_buf > tq - WINDOW),
                              (kn < tnew) & (past + kn <= tq) & (past + kn > tq - WINDOW)], axis=1)
        gates = _sigmoid(small_ref[...])
        for h in range(NSA_HEADS):
            g, r = divmod(h, NSA_REP)
            qh_ = q_ref[h]
            k_buf = wbuf[g * HEAD_DIM:(g + 1) * HEAD_DIM]
            v_buf = wbuf[NSA_GROUP_ROWS + g * HEAD_DIM:NSA_GROUP_ROWS + (g + 1) * HEAD_DIM]
            k_new = wnew[:, g * HEAD_DIM:(g + 1) * HEAD_DIM]
            v_new = wnew[:, NSA_GROUP_ROWS + g * HEAD_DIM:NSA_GROUP_ROWS + (g + 1) * HEAD_DIM]
            logits = jnp.concatenate([_dot(qh_, k_buf), _dot_nt(qh_, k_new)], axis=1) * ATT_SCALE
            logits = jnp.where(ok, logits, NEG_INF)
            p = jnp.exp(logits - jnp.max(logits, axis=1, keepdims=True))
            pb = p.astype(BF16)
            o_w = (_dot_nt(pb[:, :wb], v_buf) + _dot(pb[:, wb:], v_new)) / jnp.sum(p, axis=1, keepdims=True)
            gl = GN_LANE + 3 * h
            o = (gates[:, gl:gl + 1] * ocmp_ref[h] + gates[:, gl + 1:gl + 2] * osel[g][r * tnew:(r + 1) * tnew]
                 + gates[:, gl + 2:gl + 3] * o_w)
            o_ref[:, h * HEAD_DIM:(h + 1) * HEAD_DIM] = o.astype(o_ref.dtype)


def nsa_sample_sel(page_table, qh, sel, ocmp, zp, nsa_new, win_view, win_new, nsa_view, layer):
    batch, n_pages = page_table.shape
    tnew = qh.shape[1] // batch
    past = n_pages * LANES
    assert past % SEL_BLOCK == 0 and tnew <= SEL_BLOCK
    n_steps = n_pages // PAGES_PER_STEP + 1
    grp_rows = NSA_REP * tnew
    full = lambda a: pl.BlockSpec((None,) + a.shape[1:], lambda b, s, pt: (b,) + (0,) * (a.ndim - 1))
    return pl.pallas_call(
        functools.partial(_nsa_sample_sel_kernel, n_steps=n_steps, past=past),
        grid_spec=pltpu.PrefetchScalarGridSpec(
            num_scalar_prefetch=1,
            grid=(batch, n_steps),
            in_specs=[pl.BlockSpec((NSA_HEADS, tnew, HEAD_DIM), lambda b, s, pt: (1, b, 0)),
                      full(sel), full(ocmp),
                      pl.BlockSpec((tnew, LANES), lambda b, s, pt: (b, ZP_SMALL // LANES)),
                      full(nsa_new),
                      pl.BlockSpec((None, None) + win_view.shape[2:], lambda b, s, pt: (layer, b, 0, 0, 0, 0)),
                      full(win_new)]
            + _page_specs((None, None, 2, NSA_KV_HEADS, HEAD_DIM, LANES), layer, n_pages, slot=1),
            out_specs=pl.BlockSpec((tnew, MIX_WIDTH), lambda b, s, pt: (b, 0)),
            scratch_shapes=[pltpu.VMEM((NSA_KV_HEADS, grp_rows, 1), F32), pltpu.VMEM((NSA_KV_HEADS, grp_rows, 1), F32),
                            pltpu.VMEM((NSA_KV_HEADS, grp_rows, HEAD_DIM), F32)]),
        out_shape=jax.ShapeDtypeStruct((batch * tnew, MIX_WIDTH), BF16),
        compiler_params=_cp("parallel", "arbitrary"),
        name="nsa_sample_sel",
    )(page_table, qh, sel, ocmp, zp, nsa_new, win_view, win_new, *([nsa_view] * PAGES_PER_STEP))


def _merge_kernel(h_ref, b0, b1, b2, b3, w_ref, g0, g1, g2, g3, o_ref):
    h = h_ref[...]
    acc = None
    for k, (b_ref, g_ref) in enumerate(zip((b0, b1, b2, b3), (g0, g1, g2, g3))):
        term = _sigmoid(_dot_nt(h, g_ref[...])) * _dot(b_ref[...], w_ref[k])
        acc = term if acc is None else acc + term
    o_ref[...] = acc.astype(o_ref.dtype)


def merge_branches(h, branches, w, wt_gate):
    m, d = h.shape
    tm = min(1024, m)
    tn = 512
    nj = D_MODEL // tn
    gate_specs = [pl.BlockSpec((tn, d), lambda i, j, k=k: (k * nj + j, 0)) for k in range(N_BRANCH)]
    return pl.pallas_call(
        _merge_kernel,
        grid=(m // tm, nj),
        in_specs=[pl.BlockSpec((tm, d), lambda i, j: (i, 0))]
        + [pl.BlockSpec((tm, MIX_WIDTH), lambda i, j: (i, 0))] * N_BRANCH
        + [pl.BlockSpec((N_BRANCH, MIX_WIDTH, tn), lambda i, j: (0, 0, j))] + gate_specs,
        out_specs=pl.BlockSpec((tm, tn), lambda i, j: (i, j)),
        out_shape=jax.ShapeDtypeStruct((m, D_MODEL), BF16),
        compiler_params=_cp("parallel", "parallel"),
        name="merge_branches",
    )(h, *branches, w, *([wt_gate] * N_BRANCH))


IN_SIZES = (MIX_WIDTH, 3 * MIX_WIDTH, IDX_HEADS * IDX_DIM, IDX_DIM, IDX_HEADS, MIX_WIDTH,
            6 * NSA_KV_HEADS * HEAD_DIM, 3 * NSA_HEADS, 3 * MIX_WIDTH, 2 * MLSTM_HEADS, MIX_WIDTH,
            N_BRANCH * D_MODEL)
(OFF_U, OFF_QKVB, OFF_QI, OFF_KI, OFF_WI, OFF_QN, OFF_KVN, OFF_GN, OFF_QKVM, OFF_GIF, OFF_OM,
 OFF_GBR) = np.concatenate([[0], np.cumsum(IN_SIZES)[:-1]]).tolist()

ZP_U, ZP_OM, ZP_Q, ZP_K, ZP_V, ZP_SMALL, ZP_WIDTH = 0, 512, 1024, 1536, 2048, 2560, 3072
Q_WIDTH = 2 * MIX_WIDTH + IDX_HEADS * IDX_DIM
Q_HEADS = Q_WIDTH // HEAD_DIM


def _block_diag(blocks):
    g, r, c = blocks.shape
    eye = jnp.eye(g, dtype=blocks.dtype)
    return (blocks[:, :, None, :] * eye[:, None, :, None]).reshape(g * r, g * c)


def _s5_discretize(lam_re, lam_im, log_dt, b_re, b_im):
    dt = jnp.exp(log_dt)[:, None]
    mag = jnp.exp(lam_re * dt)
    a_re, a_im = mag * jnp.cos(lam_im * dt), mag * jnp.sin(lam_im * dt)
    den = lam_re * lam_re + lam_im * lam_im
    nr = a_re - 1.0
    coef_re = (nr * lam_re + a_im * lam_im) / den
    coef_im = (a_im * lam_re - nr * lam_im) / den
    bb_re = coef_re[..., None] * b_re - coef_im[..., None] * b_im
    bb_im = coef_re[..., None] * b_im + coef_im[..., None] * b_re
    return a_re, a_im, bb_re, bb_im


def _prep_layer(l, p):
    wt = jnp.transpose(p['w_in'], (2, 0, 1))[:, l, :]
    seg = lambda off, n: wt[off:off + n]
    wt_q = jnp.concatenate([seg(OFF_QKVB, MIX_WIDTH), seg(OFF_QN, MIX_WIDTH),
                            seg(OFF_QI, IDX_HEADS * IDX_DIM)]).astype(BF16)
    wt_kv = jnp.concatenate([seg(OFF_QKVB + MIX_WIDTH, 2 * MIX_WIDTH), seg(OFF_KI, IDX_DIM),
                             seg(OFF_KVN, 6 * NSA_KV_HEADS * HEAD_DIM)]).astype(BF16)
    small = jnp.concatenate([seg(OFF_WI, IDX_HEADS), seg(OFF_GN, 3 * NSA_HEADS), seg(OFF_GIF, 2 * MLSTM_HEADS)])
    pad = jnp.zeros((ZP_WIDTH - ZP_SMALL - small.shape[0], D_MODEL), F32)
    wt_plain = jnp.concatenate([seg(OFF_U, MIX_WIDTH), seg(OFF_OM, MIX_WIDTH), seg(OFF_QKVM, 3 * MIX_WIDTH),
                                small, pad]).astype(BF16)
    wt_gate = seg(OFF_GBR, N_BRANCH * D_MODEL).astype(BF16)
    a_re, a_im, bb_re, bb_im = _s5_discretize(p['s5_lam_re'][l], p['s5_lam_im'][l], p['s5_log_dt'][l],
                                              p['s5_b_re'][l], p['s5_b_im'][l])
    s5_b = jnp.concatenate([_block_diag(bb_re.transpose(0, 2, 1)), _block_diag(bb_im.transpose(0, 2, 1))],
                           axis=1).astype(BF16)
    s5_c = jnp.concatenate([_block_diag(p['s5_c_re'][l].transpose(0, 2, 1)),
                            -_block_diag(p['s5_c_im'][l].transpose(0, 2, 1))], axis=0).astype(BF16)
    gate_bias = jnp.zeros((1, LANES), F32)
    gate_bias = gate_bias.at[0, IG_LANE:IG_LANE + MLSTM_HEADS].set(p['mlstm_b_i'][l])
    gate_bias = gate_bias.at[0, FG_LANE:FG_LANE + MLSTM_HEADS].set(p['mlstm_b_f'][l])
    return dict(
        norm_g=p['norm_g'][l][:, None, :],
        w_ffn1_in=p['w_ffn1_in'][l].astype(BF16), w_ffn1_out=p['w_ffn1_out'][l].astype(BF16),
        w_ffn2_in=p['w_ffn2_in'][l].astype(BF16), w_ffn2_out=p['w_ffn2_out'][l].astype(BF16),
        wt_q=wt_q, wt_kv=wt_kv, wt_plain=wt_plain, wt_gate=wt_gate,
        s5_a=jnp.stack([a_re, a_im]).reshape(2, S5_ROWS, LANES), s5_b=s5_b, s5_c=s5_c,
        s5_d=p['s5_d'][l][None, :], w_s5_glu=p['w_s5_glu'][l].astype(BF16),
        gate_bias=gate_bias, mlstm_norm_g=p['mlstm_norm_g'][l][None, :],
        w_branch=p['w_branch'][l].astype(BF16), w_out=p['w_out'][l].astype(BF16))


def _rope_tables(pos):
    inv = ROPE_THETA ** (-jnp.arange(HALF, dtype=F32) / HALF)
    ang = pos.astype(F32)[:, None] * inv[None, :]
    return jnp.cos(ang), jnp.sin(ang)


def _row_tables(cos, sin, rotated):
    one, zero = jnp.ones_like(cos), jnp.zeros_like(sin)
    c = jnp.concatenate([x for r in rotated for x in ((cos, cos) if r else (one, one))], axis=1)
    s = jnp.concatenate([x for r in rotated for x in ((-sin, sin) if r else (zero, zero))], axis=1)
    return c, s


def _ffn(x, ada, sub, g, w_in, w_out, t):
    if t >= 512:
        act = norm_swiglu_in(x, g, ada, sub, w_in, t)
    else:
        act = swiglu_in(modnorm(x, g, ada, sub, t, BF16), w_in)
    return mm_resid(act, w_out, x, ada, 3 * sub + 2, 0.5, t)


def _s5_mixer(zp, lw, h0, batch, t):
    xs = mm(zp, lw['s5_b'])
    s, fin = s5_scan(xs.reshape(batch, t, 2, S5_ROWS, LANES), lw['s5_a'], h0)
    o = s5_out(s.reshape(batch * t, 2 * S5_N), lw['s5_c'], zp, lw['s5_d'], lw['w_s5_glu'])
    return o, fin[:, 0].reshape(batch, S5_GROUPS, S5_STATE), fin[:, 1].reshape(batch, S5_GROUPS, S5_STATE)


def _finish_layer(x, ada, lw, h, branches, t):
    merged = merge_branches(h, branches, lw['w_branch'], lw['wt_gate'])
    x = mm_resid(merged, lw['w_out'], x, ada, 5, 1.0, t)
    return _ffn(x, ada, 2, lw['norm_g'][2], lw['w_ffn2_in'], lw['w_ffn2_out'], t)


def _layer_prompt(x, ada, lw, batch, t):
    m = batch * t
    x = _ffn(x, ada, 0, lw['norm_g'][0], lw['w_ffn1_in'], lw['w_ffn1_out'], t)
    h = modnorm(x, lw['norm_g'][1], ada, 1, t, BF16)
    cos, sin = _rope_tables(jnp.arange(t, dtype=jnp.int32))
    cq, sq = _row_tables(cos, sin, (True, True))
    qr = proj_rope_rows(h, lw['wt_q'], cq, sq, t, BF16)
    qh = qr.reshape(m, Q_HEADS, HEAD_DIM).transpose(1, 0, 2)
    kv_t, ki_t, nsa_t, win_t, kv_tb, ki_tb, nsa_tb, win_tb, krow = proj_cols(h, lw['wt_kv'], cos.T, sin.T, batch, t)
    zp = mm_nt(h, lw['wt_plain'])
    o_s5, s5_re, s5_im = _s5_mixer(zp, lw, jnp.zeros((batch, 2, S5_ROWS, LANES), F32), batch, t)
    o_dsa = dsa_prompt(qh, zp, ZP_SMALL, krow, kv_tb, batch, t)
    o_nsa = nsa_prompt(qh, zp, ZP_SMALL, nsa_pool(nsa_t, batch, t), krow, nsa_tb, win_tb, batch, t)
    chunk = math.gcd(t, 256)
    o_ml, mc, mn, mm_ = mlstm(zp, lw['gate_bias'], lw['mlstm_norm_g'],
                              jnp.zeros((batch, MLSTM_HEADS, MLSTM_DK, MLSTM_DK), F32),
                              jnp.zeros((batch, 8, LANES), F32), jnp.zeros((batch, 8, LANES), F32),
                              batch, t, chunk, (ZP_Q, ZP_K, ZP_V, ZP_SMALL, ZP_OM))
    x = _finish_layer(x, ada, lw, h, [o_s5, o_dsa, o_nsa, o_ml], t)
    tokens_last = lambda a, shape: jnp.moveaxis(a.reshape((batch,) + shape + (a.shape[-1],)), -1, 1)
    wb = min(WINDOW, t)
    state = (tokens_last(kv_t, (2, DSA_HEADS, HEAD_DIM)), jnp.swapaxes(ki_t, 1, 2),
             tokens_last(nsa_t, (4, NSA_KV_HEADS, HEAD_DIM)),
             tokens_last(win_t[:, :, t - wb:], (2, NSA_KV_HEADS, HEAD_DIM)),
             mc, mn[:, :MLSTM_HEADS], mm_[:, :MLSTM_HEADS, 0], s5_re, s5_im)
    return x, state


def _layer_sample(x, ada, lw, layer, batch, t, page_table, views, past):
    m = batch * t
    kidx_view, kv_view, nsa_view, win_view = views
    past_len = page_table.shape[1] * LANES
    x = _ffn(x, ada, 0, lw['norm_g'][0], lw['w_ffn1_in'], lw['w_ffn1_out'], t)
    h = modnorm(x, lw['norm_g'][1], ada, 1, t, BF16)
    cos, sin = _rope_tables(past_len + jnp.arange(t, dtype=jnp.int32))
    cq, sq = _row_tables(cos, sin, (True, True))
    qr = proj_rope_rows(h, lw['wt_q'], cq, sq, t, BF16)
    qh = qr.reshape(m, Q_HEADS, HEAD_DIM).transpose(1, 0, 2)
    ckv, skv = _row_tables(cos, sin, KV_COLS_ROPE + (False,))
    wt_kv = jnp.concatenate([lw['wt_kv'], jnp.zeros((HEAD_DIM, D_MODEL), BF16)])
    kvr = proj_rope_rows(h, wt_kv, ckv, skv, t, F32)
    kv_rows, ki_rows = kvr[:, :2 * MIX_WIDTH], kvr[:, 2 * MIX_WIDTH:2 * MIX_WIDTH + IDX_DIM]
    nsa_rows = kvr[:, 2 * MIX_WIDTH + IDX_DIM:3 * MIX_WIDTH + IDX_DIM]
    win_rows = kvr[:, 3 * MIX_WIDTH + IDX_DIM:KV_COLS]
    zp = mm_nt(h, lw['wt_plain'])
    h0 = jnp.stack([past['s5_re'][layer], past['s5_im'][layer]], axis=1).reshape(batch, 2, S5_ROWS, LANES)
    o_s5, s5_re, s5_im = _s5_mixer(zp, lw, h0, batch, t)

    def new_rows(a):
        a = a.reshape(batch, t, a.shape[-1])
        return jnp.pad(a, ((0, 0), (0, LANES - t), (0, 0))).astype(BF16)

    scores = dsa_sample_scores(page_table, qh, zp, new_rows(ki_rows), kidx_view, layer)
    q_dsa = qr[:, :MIX_WIDTH].reshape(batch, t, DSA_HEADS, HEAD_DIM)
    qbd = jnp.einsum('bqhd,hg->bhqgd', q_dsa, jnp.eye(DSA_HEADS, dtype=BF16)).reshape(batch, DSA_HEADS * t, MIX_WIDTH)
    o_dsa = dsa_sample_attn(page_table, scores, qbd, new_rows(kv_rows), kv_view, layer)
    ocmp, sel = nsa_sample_cmp(page_table, qh, nsa_view, layer)
    o_nsa = nsa_sample_sel(page_table, qh, sel, ocmp, zp, new_rows(nsa_rows), win_view, new_rows(win_rows),
                           nsa_view, layer)
    n0 = jnp.pad(past['mlstm_n'][layer], ((0, 0), (0, 8 - MLSTM_HEADS), (0, 0)))
    m0 = jnp.pad(jnp.broadcast_to(past['mlstm_m'][layer][:, :, None], (batch, MLSTM_HEADS, LANES)),
                 ((0, 0), (0, 8 - MLSTM_HEADS), (0, 0)))
    chunk = 64 if t % 64 == 0 else t
    o_ml, mc, mn, mm_ = mlstm(zp, lw['gate_bias'], lw['mlstm_norm_g'], past['mlstm_c'][layer], n0, m0,
                              batch, t, chunk, (ZP_Q, ZP_K, ZP_V, ZP_SMALL, ZP_OM))
    x = _finish_layer(x, ada, lw, h, [o_s5, o_dsa, o_nsa, o_ml], t)
    win_buf = past['nsa_win'][layer]
    wb = win_buf.shape[1]
    win_all = jnp.concatenate([win_buf, win_rows.reshape(batch, t, 2, NSA_KV_HEADS, HEAD_DIM)], axis=1)
    state = (kv_rows.reshape(batch, t, 2, DSA_HEADS, HEAD_DIM), ki_rows.reshape(batch, t, IDX_DIM),
             nsa_rows.reshape(batch, t, 4, NSA_KV_HEADS, HEAD_DIM), win_all[:, win_all.shape[1] - wb:],
             mc, mn[:, :MLSTM_HEADS], mm_[:, :MLSTM_HEADS, 0], s5_re, s5_im)
    return x, state


def kernel(x_prompt, x_sample, cache_dsa_kv, cache_dsa_kidx, cache_nsa_kv, cache_nsa_win, state_mlstm_c,
           state_mlstm_n, state_mlstm_m, state_s5_re, state_s5_im, page_table, c_prompt, c_sample, w_ada, b_ada,
           norm_g, w_ffn1_in, w_ffn1_out, w_ffn2_in, w_ffn2_out, w_in, s5_lam_re, s5_lam_im, s5_log_dt, s5_b_re,
           s5_b_im, s5_c_re, s5_c_im, s5_d, w_s5_glu, mlstm_b_i, mlstm_b_f, mlstm_norm_g, w_branch, w_out,
           final_norm_g):
    params = dict(norm_g=norm_g, w_ffn1_in=w_ffn1_in, w_ffn1_out=w_ffn1_out, w_ffn2_in=w_ffn2_in,
                  w_ffn2_out=w_ffn2_out, w_in=w_in, s5_lam_re=s5_lam_re, s5_lam_im=s5_lam_im, s5_log_dt=s5_log_dt,
                  s5_b_re=s5_b_re, s5_b_im=s5_b_im, s5_c_re=s5_c_re, s5_c_im=s5_c_im, s5_d=s5_d,
                  w_s5_glu=w_s5_glu, mlstm_b_i=mlstm_b_i, mlstm_b_f=mlstm_b_f, mlstm_norm_g=mlstm_norm_g,
                  w_branch=w_branch, w_out=w_out)
    bp, tp, d = x_prompt.shape
    bs, ts, _ = x_sample.shape
    depth = w_ada.shape[0]
    views = (jnp.transpose(cache_dsa_kidx, (0, 1, 3, 2)), jnp.transpose(cache_dsa_kv, (0, 1, 3, 4, 5, 2)),
             jnp.transpose(cache_nsa_kv, (0, 1, 3, 4, 5, 2)), jnp.transpose(cache_nsa_win, (0, 1, 3, 4, 5, 2)))
    past = dict(nsa_win=cache_nsa_win, mlstm_c=state_mlstm_c, mlstm_n=state_mlstm_n, mlstm_m=state_mlstm_m,
                s5_re=state_s5_re, s5_im=state_s5_im)
    ada_rows = -(-(bp + bs) // 8) * 8
    c_all = jnp.pad(jnp.concatenate([c_prompt, c_sample]), ((0, ada_rows - bp - bs), (0, 0)))
    xp = x_prompt.reshape(bp * tp, d)
    xs = x_sample.reshape(bs * ts, d)
    st_p, st_s = [], []
    for l in range(depth):
        lw = _prep_layer(l, params)
        ada = ada_project(c_all, w_ada[l].astype(BF16), b_ada[l][None]).reshape(ada_rows, 9, d)
        xp, sp = _layer_prompt(xp, ada[:bp], lw, bp, tp)
        xs, ss = _layer_sample(xs, ada[bp:bp + bs], lw, l, bs, ts, page_table, views, past)
        st_p.append(sp)
        st_s.append(ss)
    g = final_norm_g[None, :]
    y_p = modnorm(xp, g, ada[:bp], None, tp, F32).reshape(bp, tp, d)
    y_s = modnorm(xs, g, ada[bp:bp + bs], None, ts, F32).reshape(bs, ts, d)
    outs = [y_p, y_s]
    for i in range(9):
        outs.append(jnp.stack([s[i] for s in st_p]))
        outs.append(jnp.stack([s[i] for s in st_s]))
    return tuple(outs)
```

```python
import functools
import math

import jax
import jax.numpy as jnp
import numpy as np
from jax import lax
from jax.experimental import pallas as pl
from jax.experimental.pallas import tpu as pltpu

F32 = jnp.float32
BF16 = jnp.bfloat16

D_MODEL = 2048
MIX_WIDTH = D_MODEL // 4
HEAD_DIM = 64
HALF = HEAD_DIM // 2
S5_GROUP = 16
S5_GROUPS = MIX_WIDTH // S5_GROUP
S5_STATE = 64
S5_N = S5_GROUPS * S5_STATE
DSA_HEADS = MIX_WIDTH // HEAD_DIM
IDX_HEADS = 4
IDX_DIM = 64
DSA_TOPK = 256
NSA_HEADS = MIX_WIDTH // HEAD_DIM
NSA_KV_HEADS = 2
NSA_REP = NSA_HEADS // NSA_KV_HEADS
CMP_STRIDE = 16
CMP_LEN = 2 * CMP_STRIDE
SEL_BLOCK = 64
SEL_TOPN = 16
CMP_PER_SEL = SEL_BLOCK // CMP_STRIDE
WINDOW = 512
MLSTM_HEADS = 4
MLSTM_DK = MIX_WIDTH // MLSTM_HEADS
D_FF = 2 * D_MODEL
ROPE_THETA = 10000.0
QUERY_BLOCK = 128
RMS_EPS = 1e-6
NEG_INF = -1e30
FORCE_SCORE = 1e4
N_BRANCH = 4

LANES = 128
VMEM_LIMIT = 56 * 1024 * 1024
HIGHEST = lax.Precision.HIGHEST


def _cp(*sem):
    return pltpu.CompilerParams(dimension_semantics=sem, vmem_limit_bytes=VMEM_LIMIT)


def _dot(a, b):
    return jnp.dot(a, b, preferred_element_type=F32)


def _dot_nt(a, b, precision=None):
    return lax.dot_general(a, b, (((1,), (1,)), ((), ())), preferred_element_type=F32,
                           precision=precision)


def _sigmoid(x):
    return 1.0 / (1.0 + jnp.exp(-x))


def _silu(x):
    return x * _sigmoid(x)


def _ada_kernel(c_ref, w_ref, b_ref, o_ref):
    c = c_ref[...]
    o_ref[...] = _dot(_silu(c).astype(BF16), w_ref[...]) + b_ref[...]


def ada_project(c, w, b):
    r, d = c.shape
    n = w.shape[1]
    tn = 2048
    return pl.pallas_call(
        _ada_kernel,
        grid=(n // tn,),
        in_specs=[pl.BlockSpec((r, d), lambda j: (0, 0)),
                  pl.BlockSpec((d, tn), lambda j: (0, j)),
                  pl.BlockSpec((1, tn), lambda j: (0, j))],
        out_specs=pl.BlockSpec((r, tn), lambda j: (0, j)),
        out_shape=jax.ShapeDtypeStruct((r, n), F32),
        compiler_params=_cp("parallel"),
        name="ada_project",
    )(c, w, b)


def _modnorm_kernel(x_ref, g_ref, ada_ref, o_ref, *, sub):
    x = x_ref[...]
    y = x * lax.rsqrt(jnp.mean(x * x, axis=-1, keepdims=True) + RMS_EPS)
    y = y * g_ref[...]
    if sub is not None:
        shift = ada_ref[3 * sub:3 * sub + 1, :]
        scale = ada_ref[3 * sub + 1:3 * sub + 2, :]
        y = y * (1.0 + scale) + shift
    o_ref[...] = y.astype(o_ref.dtype)


def modnorm(x, g, ada, sub, rows_per_batch, out_dtype):
    m, d = x.shape
    tm = min(512, rows_per_batch)
    nb = rows_per_batch // tm
    return pl.pallas_call(
        functools.partial(_modnorm_kernel, sub=sub),
        grid=(m // tm,),
        in_specs=[pl.BlockSpec((tm, d), lambda i: (i, 0)),
                  pl.BlockSpec((1, d), lambda i: (0, 0)),
                  pl.BlockSpec((None, 9, d), lambda i: (i // nb, 0, 0))],
        out_specs=pl.BlockSpec((tm, d), lambda i: (i, 0)),
        out_shape=jax.ShapeDtypeStruct((m, d), out_dtype),
        compiler_params=_cp("parallel"),
        name="modnorm",
    )(x, g, ada)


def _swiglu_in_kernel(h_ref, wa_ref, wg_ref, o_ref):
    h = h_ref[...]
    a = _dot(h, wa_ref[...])
    g = _dot(h, wg_ref[...])
    o_ref[...] = (_silu(a) * g).astype(o_ref.dtype)


def swiglu_in(h, w):
    m, d = h.shape
    f = w.shape[1] // 2
    tm = min(1024, m)
    tn = 512
    nj = f // tn
    return pl.pallas_call(
        _swiglu_in_kernel,
        grid=(m // tm, nj),
        in_specs=[pl.BlockSpec((tm, d), lambda i, j: (i, 0)),
                  pl.BlockSpec((d, tn), lambda i, j: (0, j)),
                  pl.BlockSpec((d, tn), lambda i, j: (0, j + nj))],
        out_specs=pl.BlockSpec((tm, tn), lambda i, j: (i, j)),
        out_shape=jax.ShapeDtypeStruct((m, f), BF16),
        compiler_params=_cp("parallel", "parallel"),
        name="swiglu_in",
    )(h, w, w)


def _norm_swiglu_in_kernel(x_ref, g_ref, ada_ref, wa_ref, wg_ref, o_ref, h_s, *, sub):
    @pl.when(pl.program_id(1) == 0)
    def _():
        _modnorm_kernel(x_ref, g_ref, ada_ref, h_s, sub=sub)

    _swiglu_in_kernel(h_s, wa_ref, wg_ref, o_ref)


def norm_swiglu_in(x, g, ada, sub, w, rows_per_batch):
    m, d = x.shape
    f = w.shape[1] // 2
    tm = min(1024, rows_per_batch)
    tn = 512
    nj = f // tn
    nb = rows_per_batch // tm
    return pl.pallas_call(
        functools.partial(_norm_swiglu_in_kernel, sub=sub),
        grid=(m // tm, nj),
        in_specs=[pl.BlockSpec((tm, d), lambda i, j: (i, 0)),
                  pl.BlockSpec((1, d), lambda i, j: (0, 0)),
                  pl.BlockSpec((None, 9, d), lambda i, j: (i // nb, 0, 0)),
                  pl.BlockSpec((d, tn), lambda i, j: (0, j)),
                  pl.BlockSpec((d, tn), lambda i, j: (0, j + nj))],
        out_specs=pl.BlockSpec((tm, tn), lambda i, j: (i, j)),
        out_shape=jax.ShapeDtypeStruct((m, f), BF16),
        scratch_shapes=[pltpu.VMEM((tm, d), BF16)],
        compiler_params=_cp("parallel", "arbitrary"),
        name="norm_swiglu_in",
    )(x, g, ada, w, w)


def _mm_resid_kernel(a_ref, w_ref, x_ref, ada_ref, o_ref, *, gate_row, coef):
    y = _dot(a_ref[...], w_ref[...])
    gate = ada_ref[gate_row:gate_row + 1, :]
    o_ref[...] = x_ref[...] + (coef * gate) * y


def mm_resid(a, w, x, ada, gate_row, coef, rows_per_batch):
    m, k = a.shape
    n = w.shape[1]
    tm = min(512, rows_per_batch)
    nb = rows_per_batch // tm
    return pl.pallas_call(
        functools.partial(_mm_resid_kernel, gate_row=gate_row, coef=coef),
        grid=(m // tm,),
        in_specs=[pl.BlockSpec((tm, k), lambda i: (i, 0)),
                  pl.BlockSpec((k, n), lambda i: (0, 0), pipeline_mode=pl.Buffered(1)),
                  pl.BlockSpec((tm, n), lambda i: (i, 0)),
                  pl.BlockSpec((None, 9, n), lambda i: (i // nb, 0, 0))],
        out_specs=pl.BlockSpec((tm, n), lambda i: (i, 0)),
        out_shape=jax.ShapeDtypeStruct((m, n), F32),
        compiler_params=_cp("parallel"),
        name="mm_resid",
    )(a, w, x, ada)


def _mm_kernel(a_ref, w_ref, o_ref):
    o_ref[...] = _dot(a_ref[...].astype(BF16), w_ref[...]).astype(o_ref.dtype)


def mm(a, w, out_dtype=F32, tn=2048):
    m = a.shape[0]
    k, n = w.shape
    tm = min(1024, m)
    return pl.pallas_call(
        _mm_kernel,
        grid=(m // tm, n // tn),
        in_specs=[pl.BlockSpec((tm, k), lambda i, j: (i, 0)),
                  pl.BlockSpec((k, tn), lambda i, j: (0, j))],
        out_specs=pl.BlockSpec((tm, tn), lambda i, j: (i, j)),
        out_shape=jax.ShapeDtypeStruct((m, n), out_dtype),
        compiler_params=_cp("parallel", "parallel"),
        name="mm",
    )(a, w)


def _mm_nt_kernel(a_ref, wt_ref, o_ref):
    o_ref[...] = _dot_nt(a_ref[...], wt_ref[...]).astype(o_ref.dtype)


def mm_nt(a, wt, tn=1024):
    m, k = a.shape
    n = wt.shape[0]
    tm = min(1024, m)
    return pl.pallas_call(
        _mm_nt_kernel,
        grid=(m // tm, n // tn),
        in_specs=[pl.BlockSpec((tm, k), lambda i, j: (i, 0)),
                  pl.BlockSpec((tn, k), lambda i, j: (j, 0))],
        out_specs=pl.BlockSpec((tm, tn), lambda i, j: (i, j)),
        out_shape=jax.ShapeDtypeStruct((m, n), F32),
        compiler_params=_cp("parallel", "parallel"),
        name="mm_nt",
    )(a, wt)


def _rope_rows(x, cos, sin_signed):
    w = x.shape[1]
    lane = lax.broadcasted_iota(jnp.int32, x.shape, 1)
    fwd = pltpu.roll(x, w - HALF, axis=1)
    bwd = pltpu.roll(x, HALF, axis=1)
    swapped = jnp.where((lane % HEAD_DIM) < HALF, fwd, bwd)
    return x * cos + swapped * sin_signed


def _proj_rope_rows_kernel(h_ref, wt_ref, cos_ref, sin_ref, o_ref, *, periodic):
    z = _dot_nt(h_ref[...], wt_ref[...])
    if periodic:
        reps = z.shape[1] // LANES
        cos = jnp.concatenate([cos_ref[...]] * reps, axis=1)
        sin = jnp.concatenate([sin_ref[...]] * reps, axis=1)
    else:
        cos, sin = cos_ref[...], sin_ref[...]
    o_ref[...] = _rope_rows(z, cos, sin).astype(o_ref.dtype)


def proj_rope_rows(h, wt, cos, sin, rows_per_batch, out_dtype):
    m, k = h.shape
    n = wt.shape[0]
    tm = min(512, rows_per_batch)
    nb = rows_per_batch // tm
    tw = cos.shape[1]
    return pl.pallas_call(
        functools.partial(_proj_rope_rows_kernel, periodic=(tw != n)),
        grid=(m // tm,),
        in_specs=[pl.BlockSpec((tm, k), lambda i: (i, 0)),
                  pl.BlockSpec((n, k), lambda i: (0, 0)),
                  pl.BlockSpec((tm, tw), lambda i: (i % nb, 0)),
                  pl.BlockSpec((tm, tw), lambda i: (i % nb, 0))],
        out_specs=pl.BlockSpec((tm, n), lambda i: (i, 0)),
        out_shape=jax.ShapeDtypeStruct((m, n), out_dtype),
        compiler_params=_cp("parallel"),
        name="proj_rope_rows",
    )(h, wt, cos, sin)


KV_COLS_ROPE = (True,) * 8 + (False,) * 8 + (True,) + (True, True, False, False) * 3
KV_COLS = HEAD_DIM * len(KV_COLS_ROPE)


KROW_HEADS = tuple(range(DSA_HEADS)) + (16, 21, 22)
KROW_IDX, KROW_SEL = DSA_HEADS, DSA_HEADS + 1


def _proj_cols_kernel(h_ref, wt_ref, cos_ref, sin_ref, kv_ref, ki_ref, nsa_ref, win_ref,
                      kvb_ref, kib_ref, nsab_ref, winb_ref, krow_ref):
    zt = _dot_nt(wt_ref[...], h_ref[...])
    cos, sin = cos_ref[...], sin_ref[...]
    parts = []
    for r, rot in enumerate(KV_COLS_ROPE):
        x1 = zt[r * HEAD_DIM:r * HEAD_DIM + HALF]
        x2 = zt[r * HEAD_DIM + HALF:(r + 1) * HEAD_DIM]
        if rot:
            parts += [x1 * cos - x2 * sin, x1 * sin + x2 * cos]
        else:
            parts += [x1, x2]
    out = jnp.concatenate(parts, axis=0)
    bounds = (0, 1024, 1088, 1600, 1856)
    for lo, hi, f_ref, b_ref in zip(bounds[:-1], bounds[1:], (kv_ref, ki_ref, nsa_ref, win_ref),
                                    (kvb_ref, kib_ref, nsab_ref, winb_ref)):
        f_ref[...] = out[lo:hi]
        b_ref[...] = out[lo:hi].astype(BF16)
    for j, r in enumerate(KROW_HEADS):
        krow_ref[j] = out[r * HEAD_DIM:(r + 1) * HEAD_DIM].T.astype(BF16)


def proj_cols(h, wt, cos_t, sin_t, batch, t):
    k = h.shape[1]
    tm = min(512, t)
    nt = t // tm
    widths = (1024, 64, 512, 256)
    out_shape = ([jax.ShapeDtypeStruct((batch, w, t), F32) for w in widths]
                 + [jax.ShapeDtypeStruct((batch, w, t), BF16) for w in widths]
                 + [jax.ShapeDtypeStruct((batch, len(KROW_HEADS), t, HEAD_DIM), BF16)])
    out_specs = ([pl.BlockSpec((None, w, tm), lambda b, i: (b, 0, i)) for w in widths] * 2
                 + [pl.BlockSpec((None, len(KROW_HEADS), tm, HEAD_DIM), lambda b, i: (b, 0, i, 0))])
    return pl.pallas_call(
        _proj_cols_kernel,
        grid=(batch, nt),
        in_specs=[pl.BlockSpec((tm, k), lambda b, i: (b * nt + i, 0)),
                  pl.BlockSpec((KV_COLS, k), lambda b, i: (0, 0)),
                  pl.BlockSpec((HALF, tm), lambda b, i: (0, i)),
                  pl.BlockSpec((HALF, tm), lambda b, i: (0, i))],
        out_specs=out_specs,
        out_shape=out_shape,
        compiler_params=_cp("parallel", "parallel"),
        name="proj_cols",
    )(h, wt, cos_t, sin_t)


S5_ROWS = S5_N // LANES


def _s5_scan_kernel(x_ref, a_ref, h0_ref, s_ref, fin_ref, carry_ref, *, tc):
    j = pl.program_id(1)

    @pl.when(j == 0)
    def _():
        carry_ref[...] = h0_ref[...]

    ar, ai = a_ref[0], a_ref[1]

    def step(t, carry):
        hr, hi = carry
        nr = ar * hr - ai * hi + x_ref[t, 0]
        ni = ar * hi + ai * hr + x_ref[t, 1]
        s_ref[t, 0] = nr
        s_ref[t, 1] = ni
        return nr, ni

    hr, hi = lax.fori_loop(0, tc, step, (carry_ref[0], carry_ref[1]), unroll=8)
    carry_ref[0] = hr
    carry_ref[1] = hi

    @pl.when(j == pl.num_programs(1) - 1)
    def _():
        fin_ref[0] = hr
        fin_ref[1] = hi


def s5_scan(x, a, h0):
    b, t = x.shape[:2]
    tc = min(256, t)
    blk = (None, tc, 2, S5_ROWS, LANES)
    st = (None, 2, S5_ROWS, LANES)
    return pl.pallas_call(
        functools.partial(_s5_scan_kernel, tc=tc),
        grid=(b, t // tc),
        in_specs=[pl.BlockSpec(blk, lambda i, j: (i, j, 0, 0, 0)),
                  pl.BlockSpec((2, S5_ROWS, LANES), lambda i, j: (0, 0, 0)),
                  pl.BlockSpec(st, lambda i, j: (i, 0, 0, 0))],
        out_specs=[pl.BlockSpec(blk, lambda i, j: (i, j, 0, 0, 0)),
                   pl.BlockSpec(st, lambda i, j: (i, 0, 0, 0))],
        out_shape=[jax.ShapeDtypeStruct(x.shape, F32), jax.ShapeDtypeStruct(h0.shape, F32)],
        scratch_shapes=[pltpu.VMEM((2, S5_ROWS, LANES), F32)],
        compiler_params=_cp("parallel", "arbitrary"),
        name="s5_scan",
    )(x, a, h0)


def _gelu_tanh(x):
    return 0.5 * x * (1.0 + jnp.tanh(math.sqrt(2.0 / math.pi) * (x + 0.044715 * (x * x * x))))


def _s5_out_kernel(s_ref, c_ref, u_ref, d_ref, w_ref, o_ref):
    y = _dot(s_ref[...].astype(BF16), c_ref[...]) + d_ref[...] * u_ref[...]
    y = _gelu_tanh(y)
    o_ref[...] = (y * _sigmoid(_dot(y.astype(BF16), w_ref[...]))).astype(o_ref.dtype)


def s5_out(s, cmat, zp, d, wglu):
    m = s.shape[0]
    tm = min(512, m)
    return pl.pallas_call(
        _s5_out_kernel,
        grid=(m // tm,),
        in_specs=[pl.BlockSpec((tm, 2 * S5_N), lambda i: (i, 0)),
                  pl.BlockSpec((2 * S5_N, MIX_WIDTH), lambda i: (0, 0)),
                  pl.BlockSpec((tm, MIX_WIDTH), lambda i: (i, 0)),
                  pl.BlockSpec((1, MIX_WIDTH), lambda i: (0, 0)),
                  pl.BlockSpec((MIX_WIDTH, MIX_WIDTH), lambda i: (0, 0))],
        out_specs=pl.BlockSpec((tm, MIX_WIDTH), lambda i: (i, 0)),
        out_shape=jax.ShapeDtypeStruct((m, MIX_WIDTH), BF16),
        compiler_params=_cp("parallel"),
        name="s5_out",
    )(s, cmat, zp, d, wglu)


WI_LANE = 0
GN_LANE = WI_LANE + IDX_HEADS
IG_LANE = GN_LANE + 3 * NSA_HEADS
FG_LANE = IG_LANE + MLSTM_HEADS


def _mlstm_kernel(q_ref, k_ref, v_ref, gates_ref, om_ref, bias_ref, g_ref, c0_ref, n0_ref, m0_ref,
                  o_ref, c_out, n_out, m_out, c_s, n_s, m_s, *, chunk):
    j = pl.program_id(1)

    @pl.when(j == 0)
    def _():
        c_s[...] = c0_ref[...]
        n_s[...] = n0_ref[...]
        m_s[...] = m0_ref[...]

    gates = gates_ref[...] + bias_ref[...]
    lane = lax.broadcasted_iota(jnp.int32, gates.shape, 1)
    is_f = (lane >= FG_LANE) & (lane < FG_LANE + MLSTM_HEADS)
    logsig = jnp.minimum(gates, 0.0) - jnp.log(1.0 + jnp.exp(-jnp.abs(gates)))
    gl = jnp.where(is_f, logsig, gates)
    row = lax.broadcasted_iota(jnp.int32, (chunk, chunk), 0)
    col = lax.broadcasted_iota(jnp.int32, (chunk, chunk), 1)
    causal = col <= row
    cum = jnp.dot(causal.astype(F32), gl, preferred_element_type=F32, precision=HIGHEST)
    gl_t = gl.T
    cum_t = cum.T
    scale = 1.0 / math.sqrt(MLSTM_DK)
    for h in range(MLSTM_HEADS):
        sl = slice(h * MLSTM_DK, (h + 1) * MLSTM_DK)
        q = q_ref[:, sl]
        k = k_ref[:, sl] * scale
        v = v_ref[:, sl]
        qb, kb, vb = q.astype(BF16), k.astype(BF16), v.astype(BF16)
        c = c_s[h]
        n = n_s[h:h + 1, :]
        m = m_s[h:h + 1, 0:1]
        cum_c = cum[:, FG_LANE + h:FG_LANE + h + 1]
        cum_r = cum_t[FG_LANE + h:FG_LANE + h + 1, :]
        ig_c = gl[:, IG_LANE + h:IG_LANE + h + 1]
        ig_r = gl_t[IG_LANE + h:IG_LANE + h + 1, :]
        logd = jnp.where(causal, cum_c - cum_r + ig_r, NEG_INF)
        log_state = cum_c + m
        m_t = jnp.maximum(log_state, jnp.max(logd, axis=1, keepdims=True))
        w_in = jnp.exp(logd - m_t)
        w_st = jnp.exp(log_state - m_t)
        s = _dot_nt(qb, kb) * w_in
        num = w_st * _dot(qb, c.astype(BF16)) + _dot(s.astype(BF16), vb)
        den = w_st * jnp.sum(q * n, axis=1, keepdims=True) + jnp.sum(s, axis=1, keepdims=True)
        hh = num / jnp.maximum(jnp.abs(den), jnp.exp(-m_t))
        total = cum_r[:, chunk - 1:chunk]
        m_new = jnp.maximum(total + m, jnp.max(total - cum_r + ig_r, axis=1, keepdims=True))
        a = jnp.exp(total + m - m_new)
        ws_c = jnp.exp(total - cum_c + ig_c - m_new)
        kw = k * ws_c
        c_s[h] = a * c + lax.dot_general(kw.astype(BF16), vb, (((0,), (0,)), ((), ())),
                                         preferred_element_type=F32)
        n_s[h:h + 1, :] = a * n + jnp.sum(kw, axis=0, keepdims=True)
        m_s[h:h + 1, :] = jnp.broadcast_to(m_new, (1, LANES))
        hn = hh * lax.rsqrt(jnp.mean(hh * hh, axis=1, keepdims=True) + RMS_EPS)
        o_ref[:, sl] = (hn * g_ref[:, sl] * _sigmoid(om_ref[:, sl])).astype(o_ref.dtype)

    @pl.when(j == pl.num_programs(1) - 1)
    def _():
        c_out[...] = c_s[...]
        n_out[...] = n_s[...]
        m_out[...] = m_s[...]


def mlstm(zp, bias, norm_g, c0, n0, m0, batch, t, chunk, cols):
    nc = t // chunk
    q_col, k_col, v_col, small_col, om_col = cols
    wide = lambda col: pl.BlockSpec((chunk, MIX_WIDTH), lambda b, j: (b * nc + j, col // MIX_WIDTH))
    st = lambda shape: pl.BlockSpec((None,) + shape, lambda b, j: (b,) + (0,) * len(shape))
    return pl.pallas_call(
        functools.partial(_mlstm_kernel, chunk=chunk),
        grid=(batch, nc),
        in_specs=[wide(q_col), wide(k_col), wide(v_col),
                  pl.BlockSpec((chunk, LANES), lambda b, j: (b * nc + j, small_col // LANES)),
                  wide(om_col),
                  pl.BlockSpec((1, LANES), lambda b, j: (0, 0)),
                  pl.BlockSpec((1, MIX_WIDTH), lambda b, j: (0, 0)),
                  st((MLSTM_HEADS, MLSTM_DK, MLSTM_DK)), st((8, LANES)), st((8, LANES))],
        out_specs=[pl.BlockSpec((chunk, MIX_WIDTH), lambda b, j: (b * nc + j, 0)),
                   st((MLSTM_HEADS, MLSTM_DK, MLSTM_DK)), st((8, LANES)), st((8, LANES))],
        out_shape=[jax.ShapeDtypeStruct((batch * t, MIX_WIDTH), BF16),
                   jax.ShapeDtypeStruct((batch, MLSTM_HEADS, MLSTM_DK, MLSTM_DK), F32),
                   jax.ShapeDtypeStruct((batch, 8, LANES), F32),
                   jax.ShapeDtypeStruct((batch, 8, LANES), F32)],
        scratch_shapes=[pltpu.VMEM((MLSTM_HEADS, MLSTM_DK, MLSTM_DK), F32),
                        pltpu.VMEM((8, LANES), F32), pltpu.VMEM((8, LANES), F32)],
        compiler_params=_cp("parallel", "arbitrary"),
        name="mlstm",
    )(zp, zp, zp, zp, zp, bias, norm_g, c0, n0, m0)


INT_MIN = -2 ** 31


def _count(mask):
    return jnp.sum(mask.astype(F32), axis=1, keepdims=True)


def _sort_key(score):
    bits = pltpu.bitcast(score, jnp.int32)
    key = bits ^ ((bits >> 31) & 0x7FFFFFFF)
    return jnp.where(score == 0.0, 0, key)


def _count_paged(mask):
    x = mask.astype(F32)
    pages = x.shape[0]
    if pages % 8 == 0:
        x = jnp.sum(x.reshape((pages // 8, 8) + x.shape[1:]), axis=0)
    return jnp.sum(jnp.sum(x, axis=0, keepdims=True), axis=2, keepdims=True)


def _topk_mask(score, k, paged=False):
    count = _count_paged if paged else _count
    if paged:
        idx = (lax.broadcasted_iota(jnp.int32, score.shape, 0) * LANES
               + lax.broadcasted_iota(jnp.int32, score.shape, 2))
        s = score.shape[0] * LANES
    else:
        idx = lax.broadcasted_iota(jnp.int32, score.shape, 1)
        s = score.shape[1]
    key = _sort_key(score)
    kf = float(k)
    prefix = jnp.where(count(key >= 0) >= kf, 0, INT_MIN).astype(jnp.int32)

    def value_bit(b, prefix):
        cand = prefix | lax.shift_left(jnp.int32(1), 30 - b)
        return jnp.where(count(key >= cand) >= kf, cand, prefix)

    thr = lax.fori_loop(0, 31, value_bit, prefix)
    above = key > thr
    tie = key == thr
    need = kf - count(above)
    nbits = max(1, (s - 1).bit_length())

    def index_bit(b, p):
        cand = p | lax.shift_left(jnp.int32(1), nbits - 1 - b)
        return jnp.where(count(tie & (idx < cand)) < need, cand, p)

    last = lax.fori_loop(0, nbits, index_bit, jnp.zeros_like(prefix))
    return above | (tie & (idx <= last))


ATT_SCALE = 1.0 / math.sqrt(HEAD_DIM)


KEY_CHUNK = 512
DSA_QUERY_BLOCK = 256
NSA_QUERY_BLOCK = 256


def _chunk(c, ck):
    return pl.ds(pl.multiple_of(c * ck, ck), ck)


def _fold_rows(x, op):
    r, q = x.shape
    if r % 64 == 0:
        x = op(x.reshape(8, r // 64, 8, q), axis=1)
    else:
        x = x.reshape(r // 8, 8, q)
    return op(x, axis=0)


def _col_reduce(x, op):
    return op(_fold_rows(x, op), axis=0, keepdims=True)


def _flash_work_t(n_heads, qb, ck):
    return [pltpu.VMEM((n_heads, ck, qb), F32), pltpu.VMEM((n_heads, ck, qb), BF16)]


def _flash_chunks_t(q_ref, heads, k_of, v_of, bias_s, m_s, l_s, acc_s, work, n_act, ck):
    s_s, p_s = work
    for h in heads:
        m_s[h:h + 1, :] = jnp.full((1, m_s.shape[1]), NEG_INF, F32)
        l_s[h:h + 1, :] = jnp.zeros((1, l_s.shape[1]), F32)
        acc_s[h] = jnp.zeros(acc_s.shape[1:], F32)

    def chunk_step(c, carry):
        rows = pl.ds(pl.multiple_of(c * ck, ck), ck)
        bias = bias_s[c]
        for j, h in enumerate(heads):
            s_s[j] = _dot_nt(k_of(h, rows), q_ref[h] * ATT_SCALE) + bias
        m_new = [jnp.maximum(m_s[h:h + 1, :], _col_reduce(s_s[j], jnp.max)) for j, h in enumerate(heads)]
        col_sum = []
        for j, h in enumerate(heads):
            p = jnp.exp(s_s[j] - m_new[j])
            p_s[j] = p.astype(BF16)
            col_sum.append(_col_reduce(p, jnp.sum))
        alpha = []
        for j, h in enumerate(heads):
            alpha.append(jnp.exp(m_s[h:h + 1, :] - m_new[j]))
            l_s[h:h + 1, :] = alpha[j] * l_s[h:h + 1, :] + col_sum[j]
            m_s[h:h + 1, :] = m_new[j]
        for j, h in enumerate(heads):
            acc_s[h] = alpha[j] * acc_s[h] + _dot(v_of(h, c), p_s[j])
        return carry

    lax.fori_loop(0, n_act, chunk_step, 0)
    for h in heads:
        acc_s[h] = acc_s[h] / l_s[h:h + 1, :]


def _dsa_prompt_kernel(q_ref, qi_ref, small_ref, ki_ref, k_ref, v_ref, o_ref, key_s, bias_s, acc_s, m_s, l_s,
                       s_s, p_s, *, topk, ck):
    i = pl.program_id(1)
    nch, _, qb = key_s.shape
    t = nch * ck
    n_act = (i * qb + qb - 1) // ck + 1
    tq = i * qb + lax.broadcasted_iota(jnp.int32, (1, qb), 1)
    krow = lax.broadcasted_iota(jnp.int32, (ck, qb), 0)
    kf = float(topk)
    small_t = small_ref[...].T
    wi = [small_t[WI_LANE + h:WI_LANE + h + 1, :] for h in range(IDX_HEADS)]

    def score_chunk(c, carry):
        ki = ki_ref[pl.ds(pl.multiple_of(c * ck, ck), ck), :]
        score = None
        for h in range(IDX_HEADS):
            term = wi[h] * jnp.maximum(_dot_nt(ki, qi_ref[h]), 0.0)
            score = term if score is None else score + term
        key_s[c] = _sort_key(jnp.where(c * ck + krow <= tq, score, NEG_INF))
        return carry

    lax.fori_loop(0, n_act, score_chunk, 0)

    def count_keys(pred):
        def body(c, acc):
            return acc + _fold_rows(pred(key_s[c], c).astype(F32), jnp.sum)
        return jnp.sum(lax.fori_loop(0, n_act, body, jnp.zeros((8, qb), F32)), axis=0, keepdims=True)

    n_nonneg = count_keys(lambda k, c: k >= 0)
    prefix = jnp.where(n_nonneg >= kf, 0, INT_MIN).astype(jnp.int32)
    at_least = jnp.where(n_nonneg >= kf, n_nonneg, (n_act * ck).astype(F32))

    def value_bit(b, carry):
        prefix, at_least, done = carry

        def refine():
            cand = prefix | lax.shift_left(jnp.int32(1), 30 - b)
            cnt = count_keys(lambda k, c: k >= cand)
            take = cnt >= kf
            new_at_least = jnp.where(take, cnt, at_least)
            return (jnp.where(take, cand, prefix), new_at_least,
                    (jnp.max(new_at_least) <= kf).astype(jnp.int32))

        return lax.cond(done == 1, lambda: (prefix, at_least, done), refine)

    thr, _, _ = lax.fori_loop(0, 31, value_bit, (prefix, at_least, jnp.int32(0)))
    need = kf - count_keys(lambda k, c: k > thr)
    n_tie = count_keys(lambda k, c: k == thr)
    nbits = max(1, (t - 1).bit_length())

    def resolve_ties():
        def index_bit(b, p):
            cand = p | lax.shift_left(jnp.int32(1), nbits - 1 - b)
            below = count_keys(lambda k, c: (k == thr) & (c * ck + krow < cand))
            return jnp.where(below < need, cand, p)
        return lax.fori_loop(0, nbits, index_bit, jnp.zeros((1, qb), jnp.int32))

    last = lax.cond(jnp.max(n_tie - need) > 0.0, resolve_ties, lambda: jnp.full((1, qb), t, jnp.int32))

    def bias_chunk(c, carry):
        k = key_s[c]
        kpos = c * ck + krow
        keep = ((k > thr) | ((k == thr) & (kpos <= last))) & (kpos <= tq)
        bias_s[c] = jnp.where(keep, 0.0, NEG_INF)
        return carry

    lax.fori_loop(0, n_act, bias_chunk, 0)

    _flash_chunks_t(q_ref, range(DSA_HEADS), lambda h, rows: k_ref[h, rows, :],
                    lambda h, c: v_ref[h * HEAD_DIM:(h + 1) * HEAD_DIM, _chunk(c, ck)],
                    bias_s, m_s, l_s, acc_s, (s_s, p_s), n_act, ck)
    o_ref[...] = acc_s[...].reshape(DSA_HEADS * HEAD_DIM, qb).T.astype(o_ref.dtype)


def dsa_prompt(qh, zp, small_col, krow, kv_t, batch, t):
    qb = min(DSA_QUERY_BLOCK, t)
    nq = t // qb
    topk = min(DSA_TOPK, t // 4)
    ck = min(KEY_CHUNK, t)
    nch = t // ck
    assert t % ck == 0 and ck >= topk and ck % qb == 0
    return pl.pallas_call(
        functools.partial(_dsa_prompt_kernel, topk=topk, ck=ck),
        grid=(batch, nq),
        in_specs=[pl.BlockSpec((DSA_HEADS, qb, HEAD_DIM), lambda b, i: (0, b * nq + i, 0)),
                  pl.BlockSpec((IDX_HEADS, qb, HEAD_DIM), lambda b, i: (4, b * nq + i, 0)),
                  pl.BlockSpec((qb, LANES), lambda b, i: (b * nq + i, small_col // LANES)),
                  pl.BlockSpec((None, None, t, HEAD_DIM), lambda b, i: (b, KROW_IDX, 0, 0)),
                  pl.BlockSpec((None, DSA_HEADS, t, HEAD_DIM), lambda b, i: (b, 0, 0, 0)),
                  pl.BlockSpec((None, MIX_WIDTH, t), lambda b, i: (b, 1, 0))],
        out_specs=pl.BlockSpec((qb, MIX_WIDTH), lambda b, i: (b * nq + i, 0)),
        out_shape=jax.ShapeDtypeStruct((batch * t, MIX_WIDTH), BF16),
        scratch_shapes=[pltpu.VMEM((nch, ck, qb), jnp.int32), pltpu.VMEM((nch, ck, qb), F32),
                        pltpu.VMEM((DSA_HEADS, HEAD_DIM, qb), F32),
                        pltpu.VMEM((DSA_HEADS, qb), F32), pltpu.VMEM((DSA_HEADS, qb), F32)]
        + _flash_work_t(DSA_HEADS, qb, ck),
        compiler_params=_cp("parallel", "arbitrary"),
        name="dsa_prompt",
    )(qh, qh, zp, krow, krow, kv_t)


NSA_GROUP_ROWS = NSA_KV_HEADS * HEAD_DIM


def _nsa_pool_kernel(x_ref, o_ref):
    t = x_ref.shape[1]
    nsub = t // CMP_STRIDE
    tok = lax.broadcasted_iota(jnp.int32, (t, nsub), 0)
    c = lax.broadcasted_iota(jnp.int32, (t, nsub), 1)
    inside = (tok >= c * CMP_STRIDE) & (tok < c * CMP_STRIDE + CMP_LEN) & (c < nsub - 1)
    pool = jnp.where(inside, 1.0 / CMP_LEN, 0.0).astype(F32)
    o_ref[...] = jnp.dot(x_ref[...], pool, preferred_element_type=F32, precision=HIGHEST).astype(o_ref.dtype)


def nsa_pool(nsa_t, batch, t):
    rows = 2 * NSA_GROUP_ROWS
    return pl.pallas_call(
        _nsa_pool_kernel,
        grid=(batch,),
        in_specs=[pl.BlockSpec((None, rows, t), lambda b: (b, 0, 0))],
        out_specs=pl.BlockSpec((None, rows, t // CMP_STRIDE), lambda b: (b, 0, 0)),
        out_shape=jax.ShapeDtypeStruct((batch, rows, t // CMP_STRIDE), BF16),
        compiler_params=_cp("parallel"),
        name="nsa_pool",
    )(nsa_t)


def _nsa_select_blocks(pcsum, tq, ns, n_sel, lanes=None):
    rows, nc = pcsum.shape
    lanes = ns if lanes is None else lanes
    c = lax.broadcasted_iota(jnp.int32, (nc, lanes), 0)
    j = lax.broadcasted_iota(jnp.int32, (nc, lanes), 1)
    pool = ((c >= j * CMP_PER_SEL) & (c < (j + 1) * CMP_PER_SEL)).astype(F32)
    imp = jnp.dot(pcsum, pool, preferred_element_type=F32, precision=HIGHEST)
    blk = lax.broadcasted_iota(jnp.int32, (rows, lanes), 1)
    forced = (blk == tq // SEL_BLOCK) | (blk == 0)
    imp = jnp.where(forced, FORCE_SCORE, imp)
    imp = jnp.where((blk * SEL_BLOCK <= tq) & (blk < ns), imp, NEG_INF)
    return _topk_mask(imp, n_sel) & (blk < ns)


def _top_rows(v, n):
    ns = v.shape[0]
    row = lax.broadcasted_iota(jnp.int32, v.shape, 0)
    rank = jnp.zeros(v.shape, F32)
    for i in range(ns):
        vi = v[i:i + 1, :]
        rank = rank + ((vi > v) | ((vi == v) & (row > i))).astype(F32)
    return rank < float(n)


def _nsa_prompt_kernel(q_ref, small_ref, cm_ref, ksel0_ref, ksel1_ref, vsel_ref, *rest, ns, n_sel, ck, nwin):
    wins, o_ref = rest[:nwin], rest[nwin]
    bias_s, ocmp_s, osel_s, m_s, l_s, s_s, p_s = rest[nwin + 1:]
    i = pl.program_id(1)
    qb = bias_s.shape[2]
    n_act = (i * qb + qb - 1) // ck + 1
    tq_row = i * qb + lax.broadcasted_iota(jnp.int32, (ns, qb), 1)
    tq1 = tq_row[0:1, :]
    blk = lax.broadcasted_iota(jnp.int32, (ns, qb), 0)
    ncp = cm_ref.shape[1]
    cidx = lax.broadcasted_iota(jnp.int32, (ncp, qb), 0)
    cvalid = (cidx * CMP_STRIDE + CMP_LEN - 1 <= tq1) & (cidx < ncp - 1)
    for g in range(NSA_KV_HEADS):
        grows = slice(g * HEAD_DIM, (g + 1) * HEAD_DIM)
        vrows = slice(NSA_GROUP_ROWS + g * HEAD_DIM, NSA_GROUP_ROWS + (g + 1) * HEAD_DIM)
        kcm = cm_ref[grows, :].astype(F32).T.astype(BF16)
        vcm = cm_ref[vrows, :]
        pcsum = jnp.zeros((ncp, qb), F32)
        for r in range(NSA_REP):
            h = g * NSA_REP + r
            lc = jnp.where(cvalid, _dot_nt(kcm, q_ref[h] * ATT_SCALE), NEG_INF)
            p = jnp.exp(lc - _col_reduce(lc, jnp.max))
            pc = jnp.where(cvalid, p / _col_reduce(p, jnp.sum), 0.0)
            ocmp_s[h] = _dot(vcm, pc.astype(BF16))
            pcsum = pcsum + pc
        pj = lax.broadcasted_iota(jnp.int32, (ns, ncp), 0)
        pc_ = lax.broadcasted_iota(jnp.int32, (ns, ncp), 1)
        pool = ((pc_ >= pj * CMP_PER_SEL) & (pc_ < (pj + 1) * CMP_PER_SEL)).astype(F32)
        imp = jnp.dot(pool, pcsum, preferred_element_type=F32, precision=HIGHEST)
        imp = jnp.where((blk == tq_row // SEL_BLOCK) | (blk == 0), FORCE_SCORE, imp)
        imp = jnp.where(blk * SEL_BLOCK <= tq_row, imp, NEG_INF)
        sel = _top_rows(imp, n_sel).astype(BF16)

        def bias_chunk(c, carry, sel=sel):
            tk = c * ck + lax.broadcasted_iota(jnp.int32, (ck, ns), 0)
            bj = lax.broadcasted_iota(jnp.int32, (ck, ns), 1)
            expand = (tk // SEL_BLOCK == bj).astype(BF16)
            kpos = c * ck + lax.broadcasted_iota(jnp.int32, (ck, qb), 0)
            keep = (_dot(expand, sel) > 0.5) & (kpos <= tq1)
            bias_s[c] = jnp.where(keep, 0.0, NEG_INF)
            return carry

        lax.fori_loop(0, n_act, bias_chunk, 0)

        ksel_ref = (ksel0_ref, ksel1_ref)[g]
        _flash_chunks_t(q_ref, range(g * NSA_REP, (g + 1) * NSA_REP), lambda h, rows, k=ksel_ref: k[rows, :],
                        lambda h, c, grows=grows: vsel_ref[grows, _chunk(c, ck)],
                        bias_s, m_s, l_s, osel_s, (s_s, p_s), n_act, ck)

    kw = jnp.concatenate([w[...] for w in wins], axis=1)
    wpos = (i - (nwin - 1)) * qb + lax.broadcasted_iota(jnp.int32, (nwin * qb, qb), 0)
    wok = (wpos >= 0) & (wpos <= tq1) & (wpos > tq1 - WINDOW)
    gates = _sigmoid(small_ref[...]).T
    for g in range(NSA_KV_HEADS):
        kwin = kw[g * HEAD_DIM:(g + 1) * HEAD_DIM].astype(F32).T.astype(BF16)
        vwin = kw[NSA_GROUP_ROWS + g * HEAD_DIM:NSA_GROUP_ROWS + (g + 1) * HEAD_DIM]
        for r in range(NSA_REP):
            h = g * NSA_REP + r
            lw_ = jnp.where(wok, _dot_nt(kwin, q_ref[h] * ATT_SCALE), NEG_INF)
            p = jnp.exp(lw_ - _col_reduce(lw_, jnp.max))
            o_w = _dot(vwin, p.astype(BF16)) / _col_reduce(p, jnp.sum)
            gl = GN_LANE + 3 * h
            ocmp_s[h] = (gates[gl:gl + 1, :] * ocmp_s[h] + gates[gl + 1:gl + 2, :] * osel_s[h]
                         + gates[gl + 2:gl + 3, :] * o_w)
    o_ref[...] = ocmp_s[...].reshape(NSA_HEADS * HEAD_DIM, qb).T.astype(o_ref.dtype)


def nsa_prompt(qh, zp, small_col, cm_t, krow, nsa_t, win_t, batch, t):
    qb = min(NSA_QUERY_BLOCK, t)
    nq = t // qb
    ns = -(-t // SEL_BLOCK)
    nwin = -(-WINDOW // qb) + 1
    win_specs = [pl.BlockSpec((None, 2 * NSA_GROUP_ROWS, qb),
                              lambda b, i, j=j: (b, 0, jnp.maximum(i - (nwin - 1) + j, 0)))
                 for j in range(nwin)]
    ck = min(KEY_CHUNK, t)
    assert t % ck == 0 and ck % qb == 0 and t % SEL_BLOCK == 0
    return pl.pallas_call(
        functools.partial(_nsa_prompt_kernel, ns=ns, n_sel=min(SEL_TOPN, ns), ck=ck, nwin=nwin),
        grid=(batch, nq),
        in_specs=[pl.BlockSpec((NSA_HEADS, qb, HEAD_DIM), lambda b, i: (1, b * nq + i, 0)),
                  pl.BlockSpec((qb, LANES), lambda b, i: (b * nq + i, small_col // LANES)),
                  pl.BlockSpec((None, 2 * NSA_GROUP_ROWS, cm_t.shape[2]), lambda b, i: (b, 0, 0)),
                  pl.BlockSpec((None, None, t, HEAD_DIM), lambda b, i: (b, KROW_SEL, 0, 0)),
                  pl.BlockSpec((None, None, t, HEAD_DIM), lambda b, i: (b, KROW_SEL + 1, 0, 0)),
                  pl.BlockSpec((None, NSA_GROUP_ROWS, t), lambda b, i: (b, 3, 0))] + win_specs,
        out_specs=pl.BlockSpec((qb, MIX_WIDTH), lambda b, i: (b * nq + i, 0)),
        out_shape=jax.ShapeDtypeStruct((batch * t, MIX_WIDTH), BF16),
        scratch_shapes=[pltpu.VMEM((t // ck, ck, qb), F32), pltpu.VMEM((NSA_HEADS, HEAD_DIM, qb), F32),
                        pltpu.VMEM((NSA_HEADS, HEAD_DIM, qb), F32),
                        pltpu.VMEM((NSA_HEADS, qb), F32), pltpu.VMEM((NSA_HEADS, qb), F32)]
        + _flash_work_t(NSA_REP, qb, ck),
        compiler_params=_cp("parallel", "arbitrary"),
        name="nsa_prompt",
    )(qh, zp, cm_t, krow, krow, nsa_t, *([win_t] * nwin))


PAGES_PER_STEP = 8


def _page_specs(block, layer, n_pages, slot=None, per_step=PAGES_PER_STEP):
    def spec(j):
        def index(b, s, pt):
            page = pt[b, jnp.minimum(s * per_step + j, n_pages - 1)]
            lead = (layer, page) if slot is None else (layer, page, slot)
            return lead + (0,) * (len(block) - len(lead))
        return pl.BlockSpec(block, index)
    return [spec(j) for j in range(per_step)]


SCORE_PAGES_PER_STEP = 32


def _dsa_sample_scores_kernel(pt_ref, qi_ref, small_ref, kinew_ref, *rest, n_steps):
    pages, o_ref = rest[:SCORE_PAGES_PER_STEP], rest[SCORE_PAGES_PER_STEP]
    s = pl.program_id(1)
    tnew = qi_ref.shape[1]

    def score(keys_t=None, keys=None):
        acc = None
        for h in range(IDX_HEADS):
            d = _dot(qi_ref[h], keys_t) if keys is None else _dot_nt(qi_ref[h], keys)
            term = small_ref[:, WI_LANE + h:WI_LANE + h + 1] * jnp.maximum(d, 0.0)
            acc = term if acc is None else acc + term
        return acc

    @pl.when(s < n_steps - 1)
    def _():
        for j, page in enumerate(pages):
            o_ref[j] = score(keys_t=page[...].astype(BF16))

    @pl.when(s == n_steps - 1)
    def _():
        sc = score(keys=kinew_ref[...])
        q = lax.broadcasted_iota(jnp.int32, sc.shape, 0)
        k = lax.broadcasted_iota(jnp.int32, sc.shape, 1)
        o_ref[0] = jnp.where((k <= q) & (k < tnew), sc, NEG_INF)
        for j in range(1, SCORE_PAGES_PER_STEP):
            o_ref[j] = jnp.full(sc.shape, NEG_INF, F32)


def dsa_sample_scores(page_table, qh, zp, ki_new, kidx_view, layer):
    batch, n_pages = page_table.shape
    tnew = qh.shape[1] // batch
    assert n_pages % SCORE_PAGES_PER_STEP == 0
    n_steps = n_pages // SCORE_PAGES_PER_STEP + 1
    return pl.pallas_call(
        functools.partial(_dsa_sample_scores_kernel, n_steps=n_steps),
        grid_spec=pltpu.PrefetchScalarGridSpec(
            num_scalar_prefetch=1,
            grid=(batch, n_steps),
            in_specs=[pl.BlockSpec((IDX_HEADS, tnew, HEAD_DIM), lambda b, s, pt: (4, b, 0)),
                      pl.BlockSpec((tnew, LANES), lambda b, s, pt: (b, ZP_SMALL // LANES)),
                      pl.BlockSpec((None, LANES, IDX_DIM), lambda b, s, pt: (b, 0, 0))]
            + _page_specs((None, None, IDX_DIM, LANES), layer, n_pages, per_step=SCORE_PAGES_PER_STEP),
            out_specs=pl.BlockSpec((None, SCORE_PAGES_PER_STEP, tnew, LANES), lambda b, s, pt: (b, s, 0, 0))),
        out_shape=jax.ShapeDtypeStruct((batch, SCORE_PAGES_PER_STEP * n_steps, tnew, LANES), F32),
        compiler_params=_cp("parallel", "arbitrary"),
        name="dsa_sample_scores",
    )(page_table, qh, zp, ki_new, *([kidx_view] * SCORE_PAGES_PER_STEP))


def _online_softmax_step(logits, keep, v_t, m_ref, l_ref, acc_ref, v_rows=None):
    lm = jnp.where(keep, logits, NEG_INF)
    m_old = m_ref[...]
    m_new = jnp.maximum(m_old, jnp.max(lm, axis=1, keepdims=True))
    alpha = jnp.exp(m_old - m_new)
    p = jnp.where(keep, jnp.exp(lm - m_new), 0.0)
    pv = _dot_nt(p.astype(BF16), v_t) if v_rows is None else _dot(p.astype(BF16), v_rows)
    l_ref[...] = alpha * l_ref[...] + jnp.sum(p, axis=1, keepdims=True)
    acc_ref[...] = alpha * acc_ref[...] + pv
    m_ref[...] = m_new


def _dsa_sample_attn_kernel(pt_ref, score_ref, qbd_ref, kvnew_ref, *rest, n_steps, topk):
    pages, o_ref = rest[:PAGES_PER_STEP], rest[PAGES_PER_STEP]
    keep_s, m_s, l_s, acc_s = rest[PAGES_PER_STEP + 1:]
    s = pl.program_id(1)
    tnew = score_ref.shape[1]

    @pl.when(s == 0)
    def _():
        keep_s[...] = _topk_mask(score_ref[...], topk, paged=True).astype(F32)
        m_s[...] = jnp.full(m_s.shape, NEG_INF, F32)
        l_s[...] = jnp.zeros(l_s.shape, F32)
        acc_s[...] = jnp.zeros(acc_s.shape, F32)

    qbd = qbd_ref[...] * ATT_SCALE

    def keep_rows(page):
        return jnp.concatenate([keep_s[page]] * DSA_HEADS, axis=0) > 0.5

    @pl.when(s < n_steps - 1)
    def _():
        k_t = jnp.concatenate([p[0].reshape(MIX_WIDTH, LANES).astype(BF16) for p in pages], axis=1)
        v_t = jnp.concatenate([p[1].reshape(MIX_WIDTH, LANES).astype(BF16) for p in pages], axis=1)
        keep = jnp.concatenate([keep_rows(s * PAGES_PER_STEP + j) for j in range(PAGES_PER_STEP)], axis=1)
        _online_softmax_step(_dot(qbd, k_t), keep, v_t, m_s, l_s, acc_s)

    @pl.when(s == n_steps - 1)
    def _():
        kv = kvnew_ref[...]
        logits = _dot_nt(qbd, kv[:, :MIX_WIDTH])
        q = lax.broadcasted_iota(jnp.int32, logits.shape, 0) % tnew
        k = lax.broadcasted_iota(jnp.int32, logits.shape, 1)
        keep = keep_rows((n_steps - 1) * PAGES_PER_STEP) & (k <= q) & (k < tnew)
        _online_softmax_step(logits, keep, None, m_s, l_s, acc_s, v_rows=kv[:, MIX_WIDTH:])
        out = acc_s[...] / l_s[...]
        for h in range(DSA_HEADS):
            o_ref[:, h * HEAD_DIM:(h + 1) * HEAD_DIM] = (
                out[h * tnew:(h + 1) * tnew, h * HEAD_DIM:(h + 1) * HEAD_DIM].astype(o_ref.dtype))


def dsa_sample_attn(page_table, scores, qbd, kv_new, kv_view, layer):
    batch, n_pages = page_table.shape
    tnew = scores.shape[2]
    n_steps = n_pages // PAGES_PER_STEP + 1
    topk = min(DSA_TOPK, (n_pages * LANES + tnew) // 4)
    rows = DSA_HEADS * tnew
    return pl.pallas_call(
        functools.partial(_dsa_sample_attn_kernel, n_steps=n_steps, topk=topk),
        grid_spec=pltpu.PrefetchScalarGridSpec(
            num_scalar_prefetch=1,
            grid=(batch, n_steps),
            in_specs=[pl.BlockSpec((None,) + scores.shape[1:], lambda b, s, pt: (b, 0, 0, 0)),
                      pl.BlockSpec((None, rows, MIX_WIDTH), lambda b, s, pt: (b, 0, 0)),
                      pl.BlockSpec((None, LANES, 2 * MIX_WIDTH), lambda b, s, pt: (b, 0, 0))]
            + _page_specs((None, None, 2, DSA_HEADS, HEAD_DIM, LANES), layer, n_pages),
            out_specs=pl.BlockSpec((tnew, MIX_WIDTH), lambda b, s, pt: (b, 0)),
            scratch_shapes=[pltpu.VMEM(scores.shape[1:], F32), pltpu.VMEM((rows, 1), F32),
                            pltpu.VMEM((rows, 1), F32), pltpu.VMEM((rows, MIX_WIDTH), F32)]),
        out_shape=jax.ShapeDtypeStruct((batch * tnew, MIX_WIDTH), BF16),
        compiler_params=_cp("parallel", "arbitrary"),
        name="dsa_sample_attn",
    )(page_table, scores, qbd, kv_new, *([kv_view] * PAGES_PER_STEP))


CMP_PAGES_PER_STEP = 16


def _nsa_sample_cmp_kernel(pt_ref, q_ref, *rest, n_steps, past, ns, n_sel):
    pages = rest[:CMP_PAGES_PER_STEP]
    ocmp_ref, sel_ref, sub_s = rest[CMP_PAGES_PER_STEP:]
    s = pl.program_id(1)
    tnew = q_ref.shape[1]
    rows = 2 * NSA_GROUP_ROWS
    n_tok = CMP_PAGES_PER_STEP * LANES
    tok = lax.broadcasted_iota(jnp.int32, (n_tok, LANES), 0)
    col = lax.broadcasted_iota(jnp.int32, (n_tok, LANES), 1)
    pool = jnp.where(col == tok // CMP_STRIDE, 1.0 / CMP_STRIDE, 0.0).astype(BF16)
    x = jnp.concatenate([page[...].reshape(rows, LANES) for page in pages], axis=1)
    x_hi = x.astype(BF16)
    r1 = x - x_hi.astype(F32)
    x_mid = r1.astype(BF16)
    x_lo = (r1 - x_mid.astype(F32)).astype(BF16)
    sub_s[s] = _dot(x_hi, pool) + _dot(x_mid, pool) + _dot(x_lo, pool)

    @pl.when(s == n_steps - 1)
    def _():
        sub_all = jnp.concatenate([sub_s[i] for i in range(n_steps)], axis=1)
        ncp = sub_all.shape[1]
        cm = (0.5 * (sub_all + pltpu.roll(sub_all, ncp - 1, axis=1))).astype(BF16)
        nc = (past + tnew) // CMP_STRIDE - 1
        tq = past + lax.broadcasted_iota(jnp.int32, (tnew, 1), 0)
        cidx = lax.broadcasted_iota(jnp.int32, (tnew, ncp), 1)
        cvalid = (cidx * CMP_STRIDE + CMP_LEN - 1 <= tq) & (cidx < nc)
        for g in range(NSA_KV_HEADS):
            kcm = cm[g * HEAD_DIM:(g + 1) * HEAD_DIM]
            vcm = cm[NSA_GROUP_ROWS + g * HEAD_DIM:NSA_GROUP_ROWS + (g + 1) * HEAD_DIM]
            pcsum = jnp.zeros((tnew, ncp), F32)
            for r in range(NSA_REP):
                h = g * NSA_REP + r
                lc = jnp.where(cvalid, _dot(q_ref[h], kcm) * ATT_SCALE, NEG_INF)
                p = jnp.exp(lc - jnp.max(lc, axis=1, keepdims=True))
                pc = jnp.where(cvalid, p / jnp.sum(p, axis=1, keepdims=True), 0.0)
                ocmp_ref[h] = _dot_nt(pc.astype(BF16), vcm)
                pcsum = pcsum + pc
            sel_ref[g] = _nsa_select_blocks(pcsum, tq, ns, n_sel, lanes=sel_ref.shape[-1]).astype(F32)


def nsa_sample_cmp(page_table, qh, nsa_view, layer):
    batch, n_pages = page_table.shape
    tnew = qh.shape[1] // batch
    past = n_pages * LANES
    assert tnew < CMP_STRIDE and n_pages % CMP_PAGES_PER_STEP == 0
    n_steps = n_pages // CMP_PAGES_PER_STEP
    ns = -(-(past + tnew) // SEL_BLOCK)
    ns_lanes = -(-ns // LANES) * LANES
    return pl.pallas_call(
        functools.partial(_nsa_sample_cmp_kernel, n_steps=n_steps, past=past, ns=ns, n_sel=min(SEL_TOPN, ns)),
        grid_spec=pltpu.PrefetchScalarGridSpec(
            num_scalar_prefetch=1,
            grid=(batch, n_steps),
            in_specs=[pl.BlockSpec((NSA_HEADS, tnew, HEAD_DIM), lambda b, s, pt: (1, b, 0))]
            + _page_specs((None, None, 2, NSA_KV_HEADS, HEAD_DIM, LANES), layer, n_pages, slot=0,
                          per_step=CMP_PAGES_PER_STEP),
            out_specs=[pl.BlockSpec((None, NSA_HEADS, tnew, HEAD_DIM), lambda b, s, pt: (b, 0, 0, 0)),
                       pl.BlockSpec((None, NSA_KV_HEADS, tnew, ns_lanes), lambda b, s, pt: (b, 0, 0, 0))],
            scratch_shapes=[pltpu.VMEM((n_steps, 2 * NSA_GROUP_ROWS, LANES), F32)]),
        out_shape=[jax.ShapeDtypeStruct((batch, NSA_HEADS, tnew, HEAD_DIM), F32),
                   jax.ShapeDtypeStruct((batch, NSA_KV_HEADS, tnew, ns_lanes), F32)],
        compiler_params=_cp("parallel", "arbitrary"),
        name="nsa_sample_cmp",
    )(page_table, qh, *([nsa_view] * CMP_PAGES_PER_STEP))


def _nsa_sample_sel_kernel(pt_ref, q_ref, sel_ref, ocmp_ref, small_ref, new_ref, wbuf_ref, wnew_ref, *rest,
                           n_steps, past):
    pages, o_ref = rest[:PAGES_PER_STEP], rest[PAGES_PER_STEP]
    m_s, l_s, acc_s = rest[PAGES_PER_STEP + 1:]
    s = pl.program_id(1)
    tnew = q_ref.shape[1]
    grp_rows = NSA_REP * tnew

    @pl.when(s == 0)
    def _():
        m_s[...] = jnp.full(m_s.shape, NEG_INF, F32)
        l_s[...] = jnp.zeros(l_s.shape, F32)
        acc_s[...] = jnp.zeros(acc_s.shape, F32)

    q_all = q_ref[...].reshape(NSA_HEADS * tnew, HEAD_DIM) * ATT_SCALE

    @pl.when(s < n_steps - 1)
    def _():
        step_tokens = PAGES_PER_STEP * LANES
        nsl = sel_ref.shape[2]
        bj = lax.broadcasted_iota(jnp.int32, (nsl, step_tokens), 0)
        tk = lax.broadcasted_iota(jnp.int32, (nsl, step_tokens), 1)
        expand = (bj == s * (step_tokens // SEL_BLOCK) + tk // SEL_BLOCK).astype(BF16)
        for g in range(NSA_KV_HEADS):
            keep_g = _dot(sel_ref[g].astype(BF16), expand) > 0.5
            keep = jnp.concatenate([keep_g] * NSA_REP, axis=0)
            qg = q_all[g * grp_rows:(g + 1) * grp_rows]
            k_t = jnp.concatenate([p[0, g].astype(BF16) for p in pages], axis=1)
            v_t = jnp.concatenate([p[1, g].astype(BF16) for p in pages], axis=1)
            _online_softmax_step(_dot(qg, k_t), keep, v_t, m_s.at[g], l_s.at[g], acc_s.at[g])

    @pl.when(s == n_steps - 1)
    def _():
        new = new_ref[...]
        new_block = past // SEL_BLOCK
        osel = []
        for g in range(NSA_KV_HEADS):
            qg = q_all[g * grp_rows:(g + 1) * grp_rows]
            k_new = new[:, (2 * NSA_KV_HEADS + g) * HEAD_DIM:(2 * NSA_KV_HEADS + g + 1) * HEAD_DIM]
            v_new = new[:, (3 * NSA_KV_HEADS + g) * HEAD_DIM:(3 * NSA_KV_HEADS + g + 1) * HEAD_DIM]
            logits = _dot_nt(qg, k_new)
            q = lax.broadcasted_iota(jnp.int32, logits.shape, 0) % tnew
            k = lax.broadcasted_iota(jnp.int32, logits.shape, 1)
            chosen = jnp.concatenate([sel_ref[g][:, new_block:new_block + 1]] * NSA_REP, axis=0) > 0.5
            _online_softmax_step(logits, chosen & (k <= q) & (k < tnew), None, m_s.at[g], l_s.at[g],
                                 acc_s.at[g], v_rows=v_new)
            osel.append(acc_s[g] / l_s[g])
        wbuf = wbuf_ref[...].reshape(2 * NSA_GROUP_ROWS, wbuf_ref.shape[-1]).astype(BF16)
        wnew = wnew_ref[...]
        wb = wbuf.shape[1]
        tq = past + lax.broadcasted_iota(jnp.int32, (tnew, 1), 0)
        pos_buf = past - wb + lax.broadcasted_iota(jnp.int32, (tnew, wb), 1)
        kn = lax.broadcasted_iota(jnp.int32, (tnew, LANES), 1)
        ok = jnp.concatenate([(pos_buf <= tq) & (pos_buf > tq - WINDOW),
                              (kn < tnew) & (past + kn <= tq) & (past + kn > tq - WINDOW)], axis=1)
        gates = _sigmoid(small_ref[...])
        for h in range(NSA_HEADS):
            g, r = divmod(h, NSA_REP)
            qh_ = q_ref[h]
            k_buf = wbuf[g * HEAD_DIM:(g + 1) * HEAD_DIM]
            v_buf = wbuf[NSA_GROUP_ROWS + g * HEAD_DIM:NSA_GROUP_ROWS + (g + 1) * HEAD_DIM]
            k_new = wnew[:, g * HEAD_DIM:(g + 1) * HEAD_DIM]
            v_new = wnew[:, NSA_GROUP_ROWS + g * HEAD_DIM:NSA_GROUP_ROWS + (g + 1) * HEAD_DIM]
            logits = jnp.concatenate([_dot(qh_, k_buf), _dot_nt(qh_, k_new)], axis=1) * ATT_SCALE
            logits = jnp.where(ok, logits, NEG_INF)
            p = jnp.exp(logits - jnp.max(logits, axis=1, keepdims=True))
            pb = p.astype(BF16)
            o_w = (_dot_nt(pb[:, :wb], v_buf) + _dot(pb[:, wb:], v_new)) / jnp.sum(p, axis=1, keepdims=True)
            gl = GN_LANE + 3 * h
            o = (gates[:, gl:gl + 1] * ocmp_ref[h] + gates[:, gl + 1:gl + 2] * osel[g][r * tnew:(r + 1) * tnew]
                 + gates[:, gl + 2:gl + 3] * o_w)
            o_ref[:, h * HEAD_DIM:(h + 1) * HEAD_DIM] = o.astype(o_ref.dtype)


def nsa_sample_sel(page_table, qh, sel, ocmp, zp, nsa_new, win_view, win_new, nsa_view, layer):
    batch, n_pages = page_table.shape
    tnew = qh.shape[1] // batch
    past = n_pages * LANES
    assert past % SEL_BLOCK == 0 and tnew <= SEL_BLOCK
    n_steps = n_pages // PAGES_PER_STEP + 1
    grp_rows = NSA_REP * tnew
    full = lambda a: pl.BlockSpec((None,) + a.shape[1:], lambda b, s, pt: (b,) + (0,) * (a.ndim - 1))
    return pl.pallas_call(
        functools.partial(_nsa_sample_sel_kernel, n_steps=n_steps, past=past),
        grid_spec=pltpu.PrefetchScalarGridSpec(
            num_scalar_prefetch=1,
            grid=(batch, n_steps),
            in_specs=[pl.BlockSpec((NSA_HEADS, tnew, HEAD_DIM), lambda b, s, pt: (1, b, 0)),
                      full(sel), full(ocmp),
                      pl.BlockSpec((tnew, LANES), lambda b, s, pt: (b, ZP_SMALL // LANES)),
                      full(nsa_new),
                      pl.BlockSpec((None, None) + win_view.shape[2:], lambda b, s, pt: (layer, b, 0, 0, 0, 0)),
                      full(win_new)]
            + _page_specs((None, None, 2, NSA_KV_HEADS, HEAD_DIM, LANES), layer, n_pages, slot=1),
            out_specs=pl.BlockSpec((tnew, MIX_WIDTH), lambda b, s, pt: (b, 0)),
            scratch_shapes=[pltpu.VMEM((NSA_KV_HEADS, grp_rows, 1), F32), pltpu.VMEM((NSA_KV_HEADS, grp_rows, 1), F32),
                            pltpu.VMEM((NSA_KV_HEADS, grp_rows, HEAD_DIM), F32)]),
        out_shape=jax.ShapeDtypeStruct((batch * tnew, MIX_WIDTH), BF16),
        compiler_params=_cp("parallel", "arbitrary"),
        name="nsa_sample_sel",
    )(page_table, qh, sel, ocmp, zp, nsa_new, win_view, win_new, *([nsa_view] * PAGES_PER_STEP))


def _merge_kernel(h_ref, b0, b1, b2, b3, w_ref, g0, g1, g2, g3, o_ref):
    h = h_ref[...]
    acc = None
    for k, (b_ref, g_ref) in enumerate(zip((b0, b1, b2, b3), (g0, g1, g2, g3))):
        term = _sigmoid(_dot_nt(h, g_ref[...])) * _dot(b_ref[...], w_ref[k])
        acc = term if acc is None else acc + term
    o_ref[...] = acc.astype(o_ref.dtype)


def merge_branches(h, branches, w, wt_gate):
    m, d = h.shape
    tm = min(1024, m)
    tn = 512
    nj = D_MODEL // tn
    gate_specs = [pl.BlockSpec((tn, d), lambda i, j, k=k: (k * nj + j, 0)) for k in range(N_BRANCH)]
    return pl.pallas_call(
        _merge_kernel,
        grid=(m // tm, nj),
        in_specs=[pl.BlockSpec((tm, d), lambda i, j: (i, 0))]
        + [pl.BlockSpec((tm, MIX_WIDTH), lambda i, j: (i, 0))] * N_BRANCH
        + [pl.BlockSpec((N_BRANCH, MIX_WIDTH, tn), lambda i, j: (0, 0, j))] + gate_specs,
        out_specs=pl.BlockSpec((tm, tn), lambda i, j: (i, j)),
        out_shape=jax.ShapeDtypeStruct((m, D_MODEL), BF16),
        compiler_params=_cp("parallel", "parallel"),
        name="merge_branches",
    )(h, *branches, w, *([wt_gate] * N_BRANCH))


IN_SIZES = (MIX_WIDTH, 3 * MIX_WIDTH, IDX_HEADS * IDX_DIM, IDX_DIM, IDX_HEADS, MIX_WIDTH,
            6 * NSA_KV_HEADS * HEAD_DIM, 3 * NSA_HEADS, 3 * MIX_WIDTH, 2 * MLSTM_HEADS, MIX_WIDTH,
            N_BRANCH * D_MODEL)
(OFF_U, OFF_QKVB, OFF_QI, OFF_KI, OFF_WI, OFF_QN, OFF_KVN, OFF_GN, OFF_QKVM, OFF_GIF, OFF_OM,
 OFF_GBR) = np.concatenate([[0], np.cumsum(IN_SIZES)[:-1]]).tolist()

ZP_U, ZP_OM, ZP_Q, ZP_K, ZP_V, ZP_SMALL, ZP_WIDTH = 0, 512, 1024, 1536, 2048, 2560, 3072
Q_WIDTH = 2 * MIX_WIDTH + IDX_HEADS * IDX_DIM
Q_HEADS = Q_WIDTH // HEAD_DIM


def _block_diag(blocks):
    g, r, c = blocks.shape
    eye = jnp.eye(g, dtype=blocks.dtype)
    return (blocks[:, :, None, :] * eye[:, None, :, None]).reshape(g * r, g * c)


def _s5_discretize(lam_re, lam_im, log_dt, b_re, b_im):
    dt = jnp.exp(log_dt)[:, None]
    mag = jnp.exp(lam_re * dt)
    a_re, a_im = mag * jnp.cos(lam_im * dt), mag * jnp.sin(lam_im * dt)
    den = lam_re * lam_re + lam_im * lam_im
    nr = a_re - 1.0
    coef_re = (nr * lam_re + a_im * lam_im) / den
    coef_im = (a_im * lam_re - nr * lam_im) / den
    bb_re = coef_re[..., None] * b_re - coef_im[..., None] * b_im
    bb_im = coef_re[..., None] * b_im + coef_im[..., None] * b_re
    return a_re, a_im, bb_re, bb_im


def _prep_layer(l, p):
    wt = jnp.transpose(p['w_in'], (2, 0, 1))[:, l, :]
    seg = lambda off, n: wt[off:off + n]
    wt_q = jnp.concatenate([seg(OFF_QKVB, MIX_WIDTH), seg(OFF_QN, MIX_WIDTH),
                            seg(OFF_QI, IDX_HEADS * IDX_DIM)]).astype(BF16)
    wt_kv = jnp.concatenate([seg(OFF_QKVB + MIX_WIDTH, 2 * MIX_WIDTH), seg(OFF_KI, IDX_DIM),
                             seg(OFF_KVN, 6 * NSA_KV_HEADS * HEAD_DIM)]).astype(BF16)
    small = jnp.concatenate([seg(OFF_WI, IDX_HEADS), seg(OFF_GN, 3 * NSA_HEADS), seg(OFF_GIF, 2 * MLSTM_HEADS)])
    pad = jnp.zeros((ZP_WIDTH - ZP_SMALL - small.shape[0], D_MODEL), F32)
    wt_plain = jnp.concatenate([seg(OFF_U, MIX_WIDTH), seg(OFF_OM, MIX_WIDTH), seg(OFF_QKVM, 3 * MIX_WIDTH),
                                small, pad]).astype(BF16)
    wt_gate = seg(OFF_GBR, N_BRANCH * D_MODEL).astype(BF16)
    a_re, a_im, bb_re, bb_im = _s5_discretize(p['s5_lam_re'][l], p['s5_lam_im'][l], p['s5_log_dt'][l],
                                              p['s5_b_re'][l], p['s5_b_im'][l])
    s5_b = jnp.concatenate([_block_diag(bb_re.transpose(0, 2, 1)), _block_diag(bb_im.transpose(0, 2, 1))],
                           axis=1).astype(BF16)
    s5_c = jnp.concatenate([_block_diag(p['s5_c_re'][l].transpose(0, 2, 1)),
                            -_block_diag(p['s5_c_im'][l].transpose(0, 2, 1))], axis=0).astype(BF16)
    gate_bias = jnp.zeros((1, LANES), F32)
    gate_bias = gate_bias.at[0, IG_LANE:IG_LANE + MLSTM_HEADS].set(p['mlstm_b_i'][l])
    gate_bias = gate_bias.at[0, FG_LANE:FG_LANE + MLSTM_HEADS].set(p['mlstm_b_f'][l])
    return dict(
        norm_g=p['norm_g'][l][:, None, :],
        w_ffn1_in=p['w_ffn1_in'][l].astype(BF16), w_ffn1_out=p['w_ffn1_out'][l].astype(BF16),
        w_ffn2_in=p['w_ffn2_in'][l].astype(BF16), w_ffn2_out=p['w_ffn2_out'][l].astype(BF16),
        wt_q=wt_q, wt_kv=wt_kv, wt_plain=wt_plain, wt_gate=wt_gate,
        s5_a=jnp.stack([a_re, a_im]).reshape(2, S5_ROWS, LANES), s5_b=s5_b, s5_c=s5_c,
        s5_d=p['s5_d'][l][None, :], w_s5_glu=p['w_s5_glu'][l].astype(BF16),
        gate_bias=gate_bias, mlstm_norm_g=p['mlstm_norm_g'][l][None, :],
        w_branch=p['w_branch'][l].astype(BF16), w_out=p['w_out'][l].astype(BF16))


def _rope_tables(pos):
    inv = ROPE_THETA ** (-jnp.arange(HALF, dtype=F32) / HALF)
    ang = pos.astype(F32)[:, None] * inv[None, :]
    return jnp.cos(ang), jnp.sin(ang)


def _row_tables(cos, sin, rotated):
    one, zero = jnp.ones_like(cos), jnp.zeros_like(sin)
    c = jnp.concatenate([x for r in rotated for x in ((cos, cos) if r else (one, one))], axis=1)
    s = jnp.concatenate([x for r in rotated for x in ((-sin, sin) if r else (zero, zero))], axis=1)
    return c, s


def _ffn(x, ada, sub, g, w_in, w_out, t):
    if t >= 512:
        act = norm_swiglu_in(x, g, ada, sub, w_in, t)
    else:
        act = swiglu_in(modnorm(x, g, ada, sub, t, BF16), w_in)
    return mm_resid(act, w_out, x, ada, 3 * sub + 2, 0.5, t)


def _s5_mixer(zp, lw, h0, batch, t):
    xs = mm(zp, lw['s5_b'])
    s, fin = s5_scan(xs.reshape(batch, t, 2, S5_ROWS, LANES), lw['s5_a'], h0)
    o = s5_out(s.reshape(batch * t, 2 * S5_N), lw['s5_c'], zp, lw['s5_d'], lw['w_s5_glu'])
    return o, fin[:, 0].reshape(batch, S5_GROUPS, S5_STATE), fin[:, 1].reshape(batch, S5_GROUPS, S5_STATE)


def _finish_layer(x, ada, lw, h, branches, t):
    merged = merge_branches(h, branches, lw['w_branch'], lw['wt_gate'])
    x = mm_resid(merged, lw['w_out'], x, ada, 5, 1.0, t)
    return _ffn(x, ada, 2, lw['norm_g'][2], lw['w_ffn2_in'], lw['w_ffn2_out'], t)


def _layer_prompt(x, ada, lw, batch, t):
    m = batch * t
    x = _ffn(x, ada, 0, lw['norm_g'][0], lw['w_ffn1_in'], lw['w_ffn1_out'], t)
    h = modnorm(x, lw['norm_g'][1], ada, 1, t, BF16)
    cos, sin = _rope_tables(jnp.arange(t, dtype=jnp.int32))
    cq, sq = _row_tables(cos, sin, (True, True))
    qr = proj_rope_rows(h, lw['wt_q'], cq, sq, t, BF16)
    qh = qr.reshape(m, Q_HEADS, HEAD_DIM).transpose(1, 0, 2)
    kv_t, ki_t, nsa_t, win_t, kv_tb, ki_tb, nsa_tb, win_tb, krow = proj_cols(h, lw['wt_kv'], cos.T, sin.T, batch, t)
    zp = mm_nt(h, lw['wt_plain'])
    o_s5, s5_re, s5_im = _s5_mixer(zp, lw, jnp.zeros((batch, 2, S5_ROWS, LANES), F32), batch, t)
    o_dsa = dsa_prompt(qh, zp, ZP_SMALL, krow, kv_tb, batch, t)
    o_nsa = nsa_prompt(qh, zp, ZP_SMALL, nsa_pool(nsa_t, batch, t), krow, nsa_tb, win_tb, batch, t)
    chunk = math.gcd(t, 256)
    o_ml, mc, mn, mm_ = mlstm(zp, lw['gate_bias'], lw['mlstm_norm_g'],
                              jnp.zeros((batch, MLSTM_HEADS, MLSTM_DK, MLSTM_DK), F32),
                              jnp.zeros((batch, 8, LANES), F32), jnp.zeros((batch, 8, LANES), F32),
                              batch, t, chunk, (ZP_Q, ZP_K, ZP_V, ZP_SMALL, ZP_OM))
    x = _finish_layer(x, ada, lw, h, [o_s5, o_dsa, o_nsa, o_ml], t)
    tokens_last = lambda a, shape: jnp.moveaxis(a.reshape((batch,) + shape + (a.shape[-1],)), -1, 1)
    wb = min(WINDOW, t)
    state = (tokens_last(kv_t, (2, DSA_HEADS, HEAD_DIM)), jnp.swapaxes(ki_t, 1, 2),
             tokens_last(nsa_t, (4, NSA_KV_HEADS, HEAD_DIM)),
             tokens_last(win_t[:, :, t - wb:], (2, NSA_KV_HEADS, HEAD_DIM)),
             mc, mn[:, :MLSTM_HEADS], mm_[:, :MLSTM_HEADS, 0], s5_re, s5_im)
    return x, state


def _layer_sample(x, ada, lw, layer, batch, t, page_table, views, past):
    m = batch * t
    kidx_view, kv_view, nsa_view, win_view = views
    past_len = page_table.shape[1] * LANES
    x = _ffn(x, ada, 0, lw['norm_g'][0], lw['w_ffn1_in'], lw['w_ffn1_out'], t)
    h = modnorm(x, lw['norm_g'][1], ada, 1, t, BF16)
    cos, sin = _rope_tables(past_len + jnp.arange(t, dtype=jnp.int32))
    cq, sq = _row_tables(cos, sin, (True, True))
    qr = proj_rope_rows(h, lw['wt_q'], cq, sq, t, BF16)
    qh = qr.reshape(m, Q_HEADS, HEAD_DIM).transpose(1, 0, 2)
    ckv, skv = _row_tables(cos, sin, KV_COLS_ROPE + (False,))
    wt_kv = jnp.concatenate([lw['wt_kv'], jnp.zeros((HEAD_DIM, D_MODEL), BF16)])
    kvr = proj_rope_rows(h, wt_kv, ckv, skv, t, F32)
    kv_rows, ki_rows = kvr[:, :2 * MIX_WIDTH], kvr[:, 2 * MIX_WIDTH:2 * MIX_WIDTH + IDX_DIM]
    nsa_rows = kvr[:, 2 * MIX_WIDTH + IDX_DIM:3 * MIX_WIDTH + IDX_DIM]
    win_rows = kvr[:, 3 * MIX_WIDTH + IDX_DIM:KV_COLS]
    zp = mm_nt(h, lw['wt_plain'])
    h0 = jnp.stack([past['s5_re'][layer], past['s5_im'][layer]], axis=1).reshape(batch, 2, S5_ROWS, LANES)
    o_s5, s5_re, s5_im = _s5_mixer(zp, lw, h0, batch, t)

    def new_rows(a):
        a = a.reshape(batch, t, a.shape[-1])
        return jnp.pad(a, ((0, 0), (0, LANES - t), (0, 0))).astype(BF16)

    scores = dsa_sample_scores(page_table, qh, zp, new_rows(ki_rows), kidx_view, layer)
    q_dsa = qr[:, :MIX_WIDTH].reshape(batch, t, DSA_HEADS, HEAD_DIM)
    qbd = jnp.einsum('bqhd,hg->bhqgd', q_dsa, jnp.eye(DSA_HEADS, dtype=BF16)).reshape(batch, DSA_HEADS * t, MIX_WIDTH)
    o_dsa = dsa_sample_attn(page_table, scores, qbd, new_rows(kv_rows), kv_view, layer)
    ocmp, sel = nsa_sample_cmp(page_table, qh, nsa_view, layer)
    o_nsa = nsa_sample_sel(page_table, qh, sel, ocmp, zp, new_rows(nsa_rows), win_view, new_rows(win_rows),
                           nsa_view, layer)
    n0 = jnp.pad(past['mlstm_n'][layer], ((0, 0), (0, 8 - MLSTM_HEADS), (0, 0)))
    m0 = jnp.pad(jnp.broadcast_to(past['mlstm_m'][layer][:, :, None], (batch, MLSTM_HEADS, LANES)),
                 ((0, 0), (0, 8 - MLSTM_HEADS), (0, 0)))
    chunk = 64 if t % 64 == 0 else t
    o_ml, mc, mn, mm_ = mlstm(zp, lw['gate_bias'], lw['mlstm_norm_g'], past['mlstm_c'][layer], n0, m0,
                              batch, t, chunk, (ZP_Q, ZP_K, ZP_V, ZP_SMALL, ZP_OM))
    x = _finish_layer(x, ada, lw, h, [o_s5, o_dsa, o_nsa, o_ml], t)
    win_buf = past['nsa_win'][layer]
    wb = win_buf.shape[1]
    win_all = jnp.concatenate([win_buf, win_rows.reshape(batch, t, 2, NSA_KV_HEADS, HEAD_DIM)], axis=1)
    state = (kv_rows.reshape(batch, t, 2, DSA_HEADS, HEAD_DIM), ki_rows.reshape(batch, t, IDX_DIM),
             nsa_rows.reshape(batch, t, 4, NSA_KV_HEADS, HEAD_DIM), win_all[:, win_all.shape[1] - wb:],
             mc, mn[:, :MLSTM_HEADS], mm_[:, :MLSTM_HEADS, 0], s5_re, s5_im)
    return x, state


def kernel(x_prompt, x_sample, cache_dsa_kv, cache_dsa_kidx, cache_nsa_kv, cache_nsa_win, state_mlstm_c,
           state_mlstm_n, state_mlstm_m, state_s5_re, state_s5_im, page_table, c_prompt, c_sample, w_ada, b_ada,
           norm_g, w_ffn1_in, w_ffn1_out, w_ffn2_in, w_ffn2_out, w_in, s5_lam_re, s5_lam_im, s5_log_dt, s5_b_re,
           s5_b_im, s5_c_re, s5_c_im, s5_d, w_s5_glu, mlstm_b_i, mlstm_b_f, mlstm_norm_g, w_branch, w_out,
           final_norm_g):
    params = dict(norm_g=norm_g, w_ffn1_in=w_ffn1_in, w_ffn1_out=w_ffn1_out, w_ffn2_in=w_ffn2_in,
                  w_ffn2_out=w_ffn2_out, w_in=w_in, s5_lam_re=s5_lam_re, s5_lam_im=s5_lam_im, s5_log_dt=s5_log_dt,
                  s5_b_re=s5_b_re, s5_b_im=s5_b_im, s5_c_re=s5_c_re, s5_c_im=s5_c_im, s5_d=s5_d,
                  w_s5_glu=w_s5_glu, mlstm_b_i=mlstm_b_i, mlstm_b_f=mlstm_b_f, mlstm_norm_g=mlstm_norm_g,
                  w_branch=w_branch, w_out=w_out)
    bp, tp, d = x_prompt.shape
    bs, ts, _ = x_sample.shape
    depth = w_ada.shape[0]
    views = (jnp.transpose(cache_dsa_kidx, (0, 1, 3, 2)), jnp.transpose(cache_dsa_kv, (0, 1, 3, 4, 5, 2)),
             jnp.transpose(cache_nsa_kv, (0, 1, 3, 4, 5, 2)), jnp.transpose(cache_nsa_win, (0, 1, 3, 4, 5, 2)))
    past = dict(nsa_win=cache_nsa_win, mlstm_c=state_mlstm_c, mlstm_n=state_mlstm_n, mlstm_m=state_mlstm_m,
                s5_re=state_s5_re, s5_im=state_s5_im)
    ada_rows = -(-(bp + bs) // 8) * 8
    c_all = jnp.pad(jnp.concatenate([c_prompt, c_sample]), ((0, ada_rows - bp - bs), (0, 0)))
    xp = x_prompt.reshape(bp * tp, d)
    xs = x_sample.reshape(bs * ts, d)
    st_p, st_s = [], []
    for l in range(depth):
        lw = _prep_layer(l, params)
        ada = ada_project(c_all, w_ada[l].astype(BF16), b_ada[l][None]).reshape(ada_rows, 9, d)
        xp, sp = _layer_prompt(xp, ada[:bp], lw, bp, tp)
        xs, ss = _layer_sample(xs, ada[bp:bp + bs], lw, l, bs, ts, page_table, views, past)
        st_p.append(sp)
        st_s.append(ss)
    g = final_norm_g[None, :]
    y_p = modnorm(xp, g, ada[:bp], None, tp, F32).reshape(bp, tp, d)
    y_s = modnorm(xs, g, ada[bp:bp + bs], None, ts, F32).reshape(bs, ts, d)
    outs = [y_p, y_s]
    for i in range(9):
        outs.append(jnp.stack([s[i] for s in st_p]))
        outs.append(jnp.stack([s[i] for s in st_s]))
    return tuple(outs)
```
